```python
import math, functools
import jax, jax.numpy as jnp
from jax import lax
import numpy as np

D_MODEL = 1024
BATCH = 16
SEQ = 256
DEPTH = 2
DEC_BATCH = 2
DEC_SEQ = 4096
PAST_LEN = 512

GRID_W = 64
HEAD_DIM = 64
NAT_HEADS = 4
NAT_ROWS = 8
NAT_COLS = 16
GQA_HEADS = 4
GQA_KV_HEADS = 2
DIFF_HEADS = 4
DIFF_DIM = 32
SGU_WIDTH = 256
SGU_GROUPS = 4
CHUNK = 128
Q_BLOCK = 128
N_BRANCH = 4
BRANCH_W = 256
IN_W = 3 * NAT_HEADS * HEAD_DIM + (GQA_HEADS + 2 * GQA_KV_HEADS) * HEAD_DIM + DIFF_HEADS * (4 * DIFF_DIM + HEAD_DIM) + 2 * SGU_WIDTH
PEER_HEADS = 8
PEER_KEYS = 128
PEER_EXPERTS = PEER_KEYS * PEER_KEYS
PEER_QDIM = 256
PEER_TOPK = 16
PEER_TOK_BLOCK = 128
ROPE_BASE = 10000.0
EPS = 1e-6
F32 = jnp.float32

kernel_name = 'hybrid_dit_prefix_ctx_step'


def _rms(x, g):
    xf = x.astype(F32)
    y = xf * lax.rsqrt(jnp.mean(xf * xf, axis=-1, keepdims=True) + EPS)
    return (y * g.astype(F32)).astype(x.dtype)


def _axial_rope(x):
    L, d = x.shape[1], x.shape[-1]
    half = d // 2
    quarter = half // 2
    t = jnp.arange(L)
    inv = ROPE_BASE ** (-jnp.arange(quarter, dtype=F32) * 2.0 / half)
    shape = (1, L) + (1,) * (x.ndim - 3) + (quarter,)
    xf = x.astype(F32)
    parts = []
    for i, pos in enumerate((t // GRID_W, t % GRID_W)):
        ang = pos.astype(F32)[:, None] * inv[None, :]
        cos = jnp.cos(ang).reshape(shape)
        sin = jnp.sin(ang).reshape(shape)
        x1 = xf[..., i * half: i * half + quarter]
        x2 = xf[..., i * half + quarter: (i + 1) * half]
        parts += [x1 * cos - x2 * sin, x2 * cos + x1 * sin]
    return jnp.concatenate(parts, axis=-1).astype(x.dtype)


def _split_blocks(a):
    B, L = a.shape[:2]
    return a.reshape((B, L // Q_BLOCK, Q_BLOCK) + a.shape[2:]).swapaxes(0, 1)


def _merge_blocks(a):
    nb, B, Q = a.shape[:3]
    return a.swapaxes(0, 1).reshape((B, nb * Q) + a.shape[3:])


def _sweep(fn, q, *kv):
    return _merge_blocks(lax.map(lambda qb: fn(qb, *kv), _split_blocks(q)))


def _gqa_block(qb, k, v):
    B, Q, H, d = qb.shape
    G = k.shape[2]
    qg = qb.reshape(B, Q, G, H // G, d)
    s = jnp.einsum('bqgrd,bkgd->bgrqk', qg, k).astype(F32) * (d ** -0.5)
    p = jax.nn.softmax(s, axis=-1).astype(v.dtype)
    return jnp.einsum('bgrqk,bkgd->bqgrd', p, v).reshape(B, Q, H, d)


def _diff_block(qb, k, v, lam):
    s = jnp.einsum('bqhmd,bkhmd->bhmqk', qb, k).astype(F32) * (DIFF_DIM ** -0.5)
    p = jax.nn.softmax(s, axis=-1)
    pd = (p[:, :, 0] - lam * p[:, :, 1]).astype(v.dtype)
    return jnp.einsum('bhqk,bkhd->bqhd', pd, v)


def _nat_latent(q, k, v, kc, vc, rpb):
    B, L, H, d = q.shape
    rows = L // GRID_W
    wr = min(NAT_ROWS, rows)
    col = np.arange(GRID_W)
    col_start = np.clip(col - NAT_COLS // 2, 0, GRID_W - NAT_COLS)
    col_idx_np = col_start[:, None] + np.arange(NAT_COLS)[None, :]
    col_bias_idx = jnp.asarray(col_idx_np - col[:, None] + NAT_COLS - 1)
    col_idx = jnp.asarray(col_idx_np)
    kg = k.reshape(B, rows, GRID_W, H, d)
    vg = v.reshape(B, rows, GRID_W, H, d)
    scale = d ** -0.5
    n_loc = wr * NAT_COLS

    def row_fn(args):
        r, q_r = args
        rs = jnp.clip(r - wr // 2, 0, rows - wr)
        k_win = lax.dynamic_slice_in_dim(kg, rs, wr, axis=1)[:, :, col_idx]
        v_win = lax.dynamic_slice_in_dim(vg, rs, wr, axis=1)[:, :, col_idx]
        s_loc = jnp.einsum('bqhd,bwqjhd->bhqwj', q_r, k_win).astype(F32) * scale
        row_bias_idx = rs + jnp.arange(wr) - r + NAT_ROWS - 1
        bias = rpb[:, row_bias_idx][:, :, col_bias_idx].transpose(0, 2, 1, 3)
        s_loc = s_loc + bias[None].astype(F32)
        s_ctx = jnp.einsum('bqhd,bkhd->bhqk', q_r, kc).astype(F32) * scale
        s = jnp.concatenate([s_loc.reshape(B, H, GRID_W, n_loc), s_ctx], axis=-1)
        p = jax.nn.softmax(s, axis=-1).astype(v.dtype)
        p_loc = p[..., :n_loc].reshape(B, H, GRID_W, wr, NAT_COLS)
        return (jnp.einsum('bhqwj,bwqjhd->bqhd', p_loc, v_win)
                + jnp.einsum('bhqk,bkhd->bqhd', p[..., n_loc:], vc))

    qr = q.reshape(B, rows, GRID_W, H, d).swapaxes(0, 1)
    out = lax.map(row_fn, (jnp.arange(rows), qr))
    return out.swapaxes(0, 1).reshape(B, L, H, d)


def _sgu(u, v, g, w_s, b_s):
    B, L, _ = v.shape
    vc = _rms(v, g).reshape(B, L // CHUNK, CHUNK, SGU_GROUPS, SGU_WIDTH // SGU_GROUPS)
    s = jnp.einsum('gij,bnjgc->bnigc', w_s, vc) + b_s.T[None, None, :, :, None]
    return u * s.reshape(B, L, SGU_WIDTH)


def _peer(h, w_q, sub_keys, u_tab, v_tab):
    B, L, D = h.shape
    K = PEER_TOPK

    def tok_block(xb):
        T = xb.shape[0]
        q = (xb @ w_q).reshape(T, PEER_HEADS, 2, PEER_QDIM // 2)
        s = jnp.einsum('thpd,pkd->thpk', q, sub_keys).astype(F32)
        s1, i1 = lax.top_k(s[:, :, 0], K)
        s2, i2 = lax.top_k(s[:, :, 1], K)
        cand = (s1[..., :, None] + s2[..., None, :]).reshape(T, PEER_HEADS, K * K)
        cidx = (i1[..., :, None] * PEER_KEYS + i2[..., None, :]).reshape(T, PEER_HEADS, K * K)
        top, pos = lax.top_k(cand, K)
        idx = jnp.take_along_axis(cidx, pos, axis=-1)
        g = jax.nn.softmax(top, axis=-1)
        u = u_tab[idx]
        v = v_tab[idx]
        a = jax.nn.gelu(jnp.einsum('td,thkd->thk', xb, u).astype(F32))
        return jnp.einsum('thk,thkd->td', (g * a).astype(v.dtype), v)

    xs = h.reshape(B * L // PEER_TOK_BLOCK, PEER_TOK_BLOCK, D)
    return lax.map(tok_block, xs).reshape(B, L, D)


def _mixers(h, lp, layer, ctx):
    B, L, _ = h.shape
    sizes = ([NAT_HEADS * HEAD_DIM] * 3
             + [GQA_HEADS * HEAD_DIM, GQA_KV_HEADS * HEAD_DIM, GQA_KV_HEADS * HEAD_DIM]
             + [DIFF_HEADS * 2 * DIFF_DIM] * 2 + [DIFF_HEADS * HEAD_DIM]
             + [SGU_WIDTH] * 2)
    cuts = [int(i) for i in np.cumsum(sizes)[:-1]]
    nq, nk, nv, gq, gk, gv, dq, dk, dv, su, sv = jnp.split(h @ lp['w_in'], cuts, axis=-1)
    nq = _rms(nq.reshape(B, L, NAT_HEADS, HEAD_DIM), lp['nat_qk_g'][0])
    nk = _rms(nk.reshape(B, L, NAT_HEADS, HEAD_DIM), lp['nat_qk_g'][1])
    nv = nv.reshape(B, L, NAT_HEADS, HEAD_DIM)
    gq = _rms(gq.reshape(B, L, GQA_HEADS, HEAD_DIM), lp['gqa_qk_g'][0])
    gk = _rms(gk.reshape(B, L, GQA_KV_HEADS, HEAD_DIM), lp['gqa_qk_g'][1])
    gv = gv.reshape(B, L, GQA_KV_HEADS, HEAD_DIM)
    dq = _rms(dq.reshape(B, L, DIFF_HEADS, 2, DIFF_DIM), lp['diff_qk_g'][0])
    dk = _rms(dk.reshape(B, L, DIFF_HEADS, 2, DIFF_DIM), lp['diff_qk_g'][1])
    dv = dv.reshape(B, L, DIFF_HEADS, HEAD_DIM)
    lam_init = 0.8 - 0.6 * math.exp(-0.3 * layer)
    lv = lp['diff_lambda'].astype(F32)
    lam = jnp.exp(jnp.sum(lv[0] * lv[1])) - jnp.exp(jnp.sum(lv[2] * lv[3])) + lam_init
    diff_fn = functools.partial(_diff_block, lam=lam)
    if ctx is None:
        own = (nk, nv, gk, gv, dk, dv)
        o_nat = _sweep(_gqa_block, nq, nk, nv)
        o_gqa = _sweep(_gqa_block, gq, gk, gv)
        o_dif = _sweep(diff_fn, dq, dk, dv)
    else:
        own = None
        cnk, cnv, cgk, cgv, cdk, cdv = ctx
        o_nat = _nat_latent(nq, nk, nv, cnk, cnv, lp['nat_rpb'])
        o_gqa = _sweep(_gqa_block, _axial_rope(gq),
                       jnp.concatenate([_axial_rope(gk), cgk], axis=1),
                       jnp.concatenate([gv, cgv], axis=1))
        o_dif = _sweep(diff_fn, _axial_rope(dq),
                       jnp.concatenate([_axial_rope(dk), cdk], axis=1),
                       jnp.concatenate([dv, cdv], axis=1))
    o_dif = _rms(o_dif, lp['diff_sub_g']) * (1.0 - lam_init)
    o_sgu = _sgu(jax.nn.gelu(su), jax.nn.gelu(sv), lp['sgu_norm_g'], lp['sgu_w'], lp['sgu_b'])
    branches = jnp.stack([o_nat.reshape(B, L, BRANCH_W), o_gqa.reshape(B, L, BRANCH_W),
                          o_dif.reshape(B, L, BRANCH_W), o_sgu], axis=2)
    proj = jnp.einsum('blnc,ncd->blnd', branches, lp['w_branch'])
    gates = jax.nn.sigmoid((h @ lp['w_gate'] + lp['b_gate']).astype(F32)).reshape(B, L, N_BRANCH, D_MODEL)
    merged = jnp.sum(gates.astype(h.dtype) * proj, axis=2)
    return merged @ lp['w_out'], own


def _layer(x, cvec, lp, layer, ctx):
    m = jax.nn.silu(cvec) @ lp['w_mod'] + lp['b_mod']
    sh1, sc1, g1, sh2, sc2, g2 = jnp.split(m, 6, axis=-1)
    h = _rms(x, lp['norm1_g']) * (1 + sc1) + sh1
    mix, own = _mixers(h, lp, layer, ctx)
    x = x + g1 * mix
    h = _rms(x, lp['norm2_g']) * (1 + sc2) + sh2
    x = x + g2 * _peer(h, lp['peer_wq'], lp['peer_subkeys'], lp['peer_u'], lp['peer_v'])
    return x, own


def setup_inputs(seed: int = 0) -> dict:
    key = jax.random.key(seed)
    ks = iter(jax.random.split(key, 40))

    def nrm(shape, s):
        return jax.random.normal(next(ks), shape, F32) * s

    D = D_MODEL
    inp = {}
    inp['x_prompt'] = nrm((BATCH, SEQ, D), 1.0)
    inp['x_sample'] = nrm((DEC_BATCH, DEC_SEQ, D), 1.0)
    inp['c'] = nrm((DEC_BATCH, D), 1.0)
    inp['cache_nat_k'] = nrm((DEC_BATCH, DEPTH, PAST_LEN, NAT_HEADS, HEAD_DIM), 1.0)
    inp['cache_nat_v'] = nrm((DEC_BATCH, DEPTH, PAST_LEN, NAT_HEADS, HEAD_DIM), 1.0)
    inp['cache_gqa_k'] = nrm((DEC_BATCH, DEPTH, PAST_LEN, GQA_KV_HEADS, HEAD_DIM), 1.0)
    inp['cache_gqa_v'] = nrm((DEC_BATCH, DEPTH, PAST_LEN, GQA_KV_HEADS, HEAD_DIM), 1.0)
    inp['cache_diff_k'] = nrm((DEC_BATCH, DEPTH, PAST_LEN, DIFF_HEADS, 2, DIFF_DIM), 1.0)
    inp['cache_diff_v'] = nrm((DEC_BATCH, DEPTH, PAST_LEN, DIFF_HEADS, HEAD_DIM), 1.0)
    inp['c_ctx'] = nrm((D,), 1.0)
    inp['w_mod'] = nrm((DEPTH, D, 6 * D), 0.5 * D ** -0.5)
    inp['b_mod'] = nrm((DEPTH, 6 * D), 0.01)
    inp['norm1_g'] = 1.0 + nrm((DEPTH, D), 0.02)
    inp['norm2_g'] = 1.0 + nrm((DEPTH, D), 0.02)
    inp['w_in'] = nrm((DEPTH, D, IN_W), D ** -0.5)
    inp['nat_qk_g'] = 1.0 + nrm((DEPTH, 2, HEAD_DIM), 0.02)
    inp['nat_rpb'] = nrm((DEPTH, NAT_HEADS, 2 * NAT_ROWS - 1, 2 * NAT_COLS - 1), 0.02)
    inp['gqa_qk_g'] = 1.0 + nrm((DEPTH, 2, HEAD_DIM), 0.02)
    inp['diff_qk_g'] = 1.0 + nrm((DEPTH, 2, DIFF_DIM), 0.02)
    inp['diff_lambda'] = nrm((DEPTH, 4, DIFF_DIM), 0.1)
    inp['diff_sub_g'] = 1.0 + nrm((DEPTH, HEAD_DIM), 0.02)
    inp['sgu_norm_g'] = 1.0 + nrm((DEPTH, SGU_WIDTH), 0.02)
    inp['sgu_w'] = nrm((DEPTH, SGU_GROUPS, CHUNK, CHUNK), CHUNK ** -0.5)
    inp['sgu_b'] = 1.0 + nrm((DEPTH, SGU_GROUPS, CHUNK), 0.01)
    inp['w_branch'] = nrm((DEPTH, N_BRANCH, BRANCH_W, D), BRANCH_W ** -0.5)
    inp['w_gate'] = nrm((DEPTH, D, N_BRANCH * D), D ** -0.5)
    inp['b_gate'] = nrm((DEPTH, N_BRANCH * D), 0.01)
    inp['w_out'] = nrm((DEPTH, D, D), D ** -0.5)
    inp['peer_wq'] = nrm((DEPTH, D, PEER_HEADS * PEER_QDIM), D ** -0.5)
    inp['peer_subkeys'] = nrm((DEPTH, 2, PEER_KEYS, PEER_QDIM // 2), (PEER_QDIM // 2) ** -0.5)
    inp['peer_u'] = nrm((DEPTH, PEER_EXPERTS, D), D ** -0.5)
    inp['peer_v'] = nrm((DEPTH, PEER_EXPERTS, D), PEER_HEADS ** -0.5)
    return inp


def reference(x_prompt, x_sample, c, cache_nat_k, cache_nat_v, cache_gqa_k, cache_gqa_v,
              cache_diff_k, cache_diff_v, c_ctx, w_mod, b_mod, norm1_g, norm2_g, w_in,
              nat_qk_g, nat_rpb, gqa_qk_g, diff_qk_g, diff_lambda, diff_sub_g, sgu_norm_g,
              sgu_w, sgu_b, w_branch, w_gate, b_gate, w_out, peer_wq, peer_subkeys,
              peer_u, peer_v):
    xp = x_prompt
    xs = x_sample
    c_prompt = c_ctx[None, None, :]
    c_sample = c[:, None, :]
    nat_k_l, nat_v_l, gqa_k_l, gqa_v_l, diff_k_l, diff_v_l = [], [], [], [], [], []
    for l in range(DEPTH):
        lp = dict(w_mod=w_mod[l], b_mod=b_mod[l], norm1_g=norm1_g[l], norm2_g=norm2_g[l],
                  w_in=w_in[l], nat_qk_g=nat_qk_g[l], nat_rpb=nat_rpb[l], gqa_qk_g=gqa_qk_g[l],
                  diff_qk_g=diff_qk_g[l], diff_lambda=diff_lambda[l], diff_sub_g=diff_sub_g[l],
                  sgu_norm_g=sgu_norm_g[l], sgu_w=sgu_w[l], sgu_b=sgu_b[l],
                  w_branch=w_branch[l], w_gate=w_gate[l], b_gate=b_gate[l], w_out=w_out[l],
                  peer_wq=peer_wq[l], peer_subkeys=peer_subkeys[l], peer_u=peer_u[l],
                  peer_v=peer_v[l])
        xp, own = _layer(xp, c_prompt, lp, l, None)
        nat_k_l.append(own[0]); nat_v_l.append(own[1])
        gqa_k_l.append(own[2]); gqa_v_l.append(own[3])
        diff_k_l.append(own[4]); diff_v_l.append(own[5])
        ctx = (cache_nat_k[:, l], cache_nat_v[:, l], cache_gqa_k[:, l], cache_gqa_v[:, l],
               cache_diff_k[:, l], cache_diff_v[:, l])
        xs, _ = _layer(xs, c_sample, lp, l, ctx)
    new_nat_k = jnp.stack(nat_k_l, axis=1)
    new_nat_v = jnp.stack(nat_v_l, axis=1)
    new_gqa_k = jnp.stack(gqa_k_l, axis=1)
    new_gqa_v = jnp.stack(gqa_v_l, axis=1)
    new_diff_k = jnp.stack(diff_k_l, axis=1)
    new_diff_v = jnp.stack(diff_v_l, axis=1)
    return (xp, xs, new_nat_k, new_nat_v, new_gqa_k, new_gqa_v, new_diff_k, new_diff_v)
```

```python
import functools
import math

import numpy as np
import jax
import jax.numpy as jnp
from jax import lax
from jax.experimental import pallas as pl
from jax.experimental.pallas import tpu as pltpu

F32 = jnp.float32
BF16 = jnp.bfloat16

D_MODEL = 1024
BATCH = 16
SEQ = 256
DEPTH = 2
DEC_BATCH = 2
DEC_SEQ = 4096
PAST_LEN = 512
GRID_W = 64
HEAD_DIM = 64
NAT_HEADS = 4
NAT_ROWS = 8
NAT_COLS = 16
GQA_HEADS = 4
GQA_KV_HEADS = 2
DIFF_HEADS = 4
DIFF_DIM = 32
SGU_WIDTH = 256
SGU_GROUPS = 4
CHUNK = 128
N_BRANCH = 4
BRANCH_W = 256
IN_W = 2560
PEER_HEADS = 8
PEER_KEYS = 128
PEER_QDIM = 256
PEER_TOPK = 16
ROPE_BASE = 10000.0
EPS = 1e-6

N_CTX = BATCH * SEQ
N_LAT = DEC_BATCH * DEC_SEQ
MOD_ROWS = 8
TB = 256
TQ = 256
PROJ_W = 2304
NAT_QROWS = 4
NAT_KROWS = NAT_QROWS + NAT_ROWS
TBP = 512
PEER_ET = 512
VMEM_LIMIT = 56 * 1024 * 1024
NEG = -1e30


def _cparams(*sem):
    return pltpu.CompilerParams(dimension_semantics=sem, vmem_limit_bytes=VMEM_LIMIT)


def _const_spec(shape):
    return pl.BlockSpec(shape, lambda *_: (0,) * len(shape))


def _gelu(x):
    return 0.5 * x * (1.0 + jnp.tanh(0.7978845608028654 * (x + 0.044715 * (x * x * x))))


def _sigmoid(x):
    return 1.0 / (1.0 + jnp.exp(-x))


def _dot(a, b):
    return jnp.dot(a.astype(BF16), b.astype(BF16), preferred_element_type=F32)


def _dot_nt(a, b):
    return lax.dot_general(a.astype(BF16), b.astype(BF16), (((1,), (1,)), ((), ())),
                           preferred_element_type=F32)


def _group_mean(y2, ones_bd):
    hi = y2.astype(BF16)
    lo = (y2 - hi.astype(F32)).astype(BF16)
    return (jnp.dot(hi, ones_bd, preferred_element_type=F32)
            + jnp.dot(lo, ones_bd, preferred_element_type=F32))


def _block_ones(width, group):
    idx = np.arange(width) // group
    return jnp.asarray((idx[:, None] == idx[None, :]).astype(np.float32) / group, dtype=BF16)


def _rms_mod(x, gain, scale, shift):
    xn = x * lax.rsqrt(jnp.mean(x * x, axis=-1, keepdims=True) + EPS) * gain
    return xn * (1.0 + scale) + shift


def _mod_spec(latent):
    if latent:
        return pl.BlockSpec((1, 1, 6 * D_MODEL), lambda i, *_: (1 + i // (DEC_SEQ // TB), 0, 0))
    return pl.BlockSpec((1, 1, 6 * D_MODEL), lambda i, *_: (0, 0, 0))


def _mod_kernel(c_ref, w_ref, b_ref, o_ref):
    c = c_ref[...]
    s = c * _sigmoid(c)
    o_ref[0] = _dot(s, w_ref[0]) + b_ref[0]


def _modulation(cvec, w_mod, b_mod):
    tn = 1536
    return pl.pallas_call(
        _mod_kernel,
        grid=(DEPTH, 6 * D_MODEL // tn),
        in_specs=[pl.BlockSpec((MOD_ROWS, D_MODEL), lambda l, j: (0, 0)),
                  pl.BlockSpec((1, D_MODEL, tn), lambda l, j: (l, 0, j)),
                  pl.BlockSpec((1, 1, tn), lambda l, j: (l, 0, j))],
        out_specs=pl.BlockSpec((1, MOD_ROWS, tn), lambda l, j: (l, 0, j)),
        out_shape=jax.ShapeDtypeStruct((DEPTH, MOD_ROWS, 6 * D_MODEL), F32),
        compiler_params=_cparams("arbitrary", "arbitrary"),
        name="modulation",
    )(cvec, w_mod, b_mod.reshape(DEPTH, 1, 6 * D_MODEL))


def _rope_tables():
    t = np.arange(DEC_SEQ)
    pos = (t // GRID_W, t % GRID_W)
    out = []
    for d in (HEAD_DIM, DIFF_DIM):
        half, quarter = d // 2, d // 4
        inv = ROPE_BASE ** (-np.arange(quarter, dtype=np.float32) * 2.0 / half)
        lane = np.arange(128) % d
        part, j = lane // half, lane % half
        ang = np.stack([pos[0][:, None] * inv[None, :], pos[1][:, None] * inv[None, :]], axis=1)
        a = ang[:, part, j % quarter].astype(np.float32)
        cos, sin = np.cos(a), np.sin(a)
        out += [jnp.asarray(cos, dtype=F32),
                jnp.asarray(np.where(j < quarter, -sin, 0.0), dtype=F32),
                jnp.asarray(np.where(j >= quarter, sin, 0.0), dtype=F32)]
    return out


def _rope(y, c, sa, sb, quarter):
    w = y.shape[-1]
    rep = w // 128
    if rep > 1:
        c, sa, sb = (jnp.concatenate([t] * rep, axis=-1) for t in (c, sa, sb))
    up = pltpu.roll(y, w - quarter, 1)
    dn = pltpu.roll(y, quarter, 1)
    return y * c + up * sa + dn * sb


def _inproj_kernel(*refs, latent):
    (x_ref, mod_ref, n1_ref, w_ref, g_ref, bd64_ref, bd32_ref, bd256_ref,
     sgn_ref, sgw_ref, sgb_ref) = refs[:11]
    o_ref = refs[-1]
    mod = mod_ref[0]
    h = _rms_mod(x_ref[...], n1_ref[...], mod[:, D_MODEL:2 * D_MODEL], mod[:, 0:D_MODEL])
    y = jnp.dot(h.astype(BF16), w_ref[...], preferred_element_type=F32)

    def qk_norm(lo, width, bd, gain_row):
        v = y[:, lo:lo + width]
        ms = _group_mean(v * v, bd)
        return v * lax.rsqrt(ms + EPS) * g_ref[gain_row:gain_row + 1, 0:width]

    bd64 = bd64_ref[...]
    bd32 = bd32_ref[...]
    if latent:
        c64, sa64, sb64, c32, sa32, sb32 = (r[...] for r in refs[11:17])
        rope64 = functools.partial(_rope, c=c64, sa=sa64, sb=sb64, quarter=HEAD_DIM // 4)
        rope32 = functools.partial(_rope, c=c32, sa=sa32, sb=sb32, quarter=DIFF_DIM // 4)
    else:
        rope64 = rope32 = lambda v: v
    o_ref[:, 0:256] = qk_norm(0, 256, bd64, 0)
    o_ref[:, 256:512] = qk_norm(256, 256, bd64, 1)
    o_ref[:, 512:768] = y[:, 512:768]
    o_ref[:, 768:1024] = rope64(qk_norm(768, 256, bd64, 2))
    o_ref[:, 1024:1152] = rope64(qk_norm(1024, 128, bd64[0:128, 0:128], 3))
    o_ref[:, 1152:1280] = y[:, 1152:1280]
    o_ref[:, 1280:1536] = rope32(qk_norm(1280, 256, bd32, 4))
    o_ref[:, 1536:1792] = rope32(qk_norm(1536, 256, bd32, 5))
    o_ref[:, 1792:2048] = y[:, 1792:2048]
    u = _gelu(y[:, 2048:2304])
    v = _gelu(y[:, 2304:2560])
    vn = v * lax.rsqrt(_group_mean(v * v, bd256_ref[...]) + EPS) * sgn_ref[...]
    vnb = vn.astype(BF16)
    lane_group = lax.broadcasted_iota(jnp.int32, (CHUNK, SGU_WIDTH), 1) // (SGU_WIDTH // SGU_GROUPS)
    for n in range(TB // CHUNK):
        vc = vnb[n * CHUNK:(n + 1) * CHUNK, :]
        s = sgb_ref[...]
        for g in range(SGU_GROUPS):
            sg = jnp.dot(sgw_ref[g], vc, preferred_element_type=F32)
            s = s + jnp.where(lane_group == g, sg, 0.0)
        o_ref[n * CHUNK:(n + 1) * CHUNK, 2048:2304] = u[n * CHUNK:(n + 1) * CHUNK, :] * s


def _in_projection(x, mod_l, lp, latent):
    n_tok = x.shape[0]
    in_specs = [pl.BlockSpec((TB, D_MODEL), lambda i: (i, 0)),
                _mod_spec(latent),
                _const_spec((1, D_MODEL)),
                _const_spec((D_MODEL, IN_W)),
                _const_spec((8, 256)),
                _const_spec((256, 256)), _const_spec((256, 256)), _const_spec((256, 256)),
                _const_spec((1, SGU_WIDTH)),
                _const_spec((SGU_GROUPS, CHUNK, CHUNK)),
                _const_spec((CHUNK, SGU_WIDTH))]
    args = [x, mod_l, lp["norm1_g"], lp["w_in"], lp["qk_gains"], lp["bd64"], lp["bd32"], lp["bd256"],
            lp["sgu_norm_g"], lp["sgu_w"], lp["sgu_b"]]
    if latent:
        in_specs += [pl.BlockSpec((TB, 128), lambda i: (i % (DEC_SEQ // TB), 0))] * 6
        args += lp["rope"]
    return pl.pallas_call(
        functools.partial(_inproj_kernel, latent=latent),
        grid=(n_tok // TB,),
        in_specs=in_specs,
        out_specs=pl.BlockSpec((TB, PROJ_W), lambda i: (i, 0)),
        out_shape=jax.ShapeDtypeStruct((n_tok, PROJ_W), F32),
        compiler_params=_cparams("arbitrary"),
        name="in_projection_lat" if latent else "in_projection_ctx",
    )(*args)


def _softmax_parts(scores):
    m = functools.reduce(jnp.maximum, [jnp.max(s, axis=-1, keepdims=True) for s in scores])
    ps = [jnp.exp(s - m) for s in scores]
    l = functools.reduce(jnp.add, [jnp.sum(p, axis=-1, keepdims=True) for p in ps])
    return ps, 1.0 / l


def _diff_lambda(lam_ref, lam_init):
    lv = lam_ref[...]
    a = jnp.sum(lv[0:1] * lv[1:2], axis=-1, keepdims=True)
    b = jnp.sum(lv[2:3] * lv[3:4], axis=-1, keepdims=True)
    return jnp.exp(a) - jnp.exp(b) + lam_init


def _head(ref_or_val, h, width=HEAD_DIM):
    return ref_or_val[:, h * width:(h + 1) * width]


def _mha(q, ks, vs, n_heads, kv_group, scale, biases=None):
    outs = []
    for h in range(n_heads):
        g = h // kv_group
        qh = _head(q, h) * scale
        scores = [_dot_nt(qh, _head(k, g)) for k in ks]
        if biases is not None:
            scores = [s if b is None else s + b[h] for s, b in zip(scores, biases)]
        ps, rl = _softmax_parts(scores)
        o = functools.reduce(jnp.add, [_dot(p, _head(v, g)) for p, v in zip(ps, vs)])
        outs.append(o * rl)
    return jnp.concatenate(outs, axis=-1)


def _diff_attn(q, ks, vs, lam, sub_gain, bd64, lam_init):
    scale = DIFF_DIM ** -0.5
    outs = []
    for h in range(DIFF_HEADS):
        pd = None
        for m in range(2):
            qm = _head(q, 2 * h + m, DIFF_DIM)
            scores = [_dot_nt(qm, _head(k, 2 * h + m, DIFF_DIM)) * scale for k in ks]
            ps, rl = _softmax_parts(scores)
            if m == 0:
                pd = [p * rl for p in ps]
            else:
                pd = [a - lam * (p * rl) for a, p in zip(pd, ps)]
        outs.append(functools.reduce(jnp.add, [_dot(p, _head(v, h)) for p, v in zip(pd, vs)]))
    o = jnp.concatenate(outs, axis=-1)
    ms = _group_mean(o * o, bd64)
    return o * lax.rsqrt(ms + EPS) * sub_gain * (1.0 - lam_init)


def _ctx_attn_kernel(nq, nk, nv, gq, gk, gv, dq, dk, dv, lam_ref, subg_ref, bd64_ref, o_ref, *, lam_init):
    scale = HEAD_DIM ** -0.5
    o_ref[:, 0:256] = _mha(nq[...], [nk[...]], [nv[...]], NAT_HEADS, 1, scale)
    o_ref[:, 256:512] = _mha(gq[...], [gk[...]], [gv[...]], GQA_HEADS, GQA_HEADS // GQA_KV_HEADS, scale)
    lam = _diff_lambda(lam_ref, lam_init)
    o_ref[:, 512:768] = _diff_attn(dq[...], [dk[...]], [dv[...]], lam, subg_ref[...], bd64_ref[...], lam_init)


_COL = dict(nq=0, nk=1, nv=2, gq=3, gk=8, gv=9, dq=5, dk=6, dv=7, sgu=8)


def _ctx_attention(proj, lp, lam_init):
    blk = lambda name, w: pl.BlockSpec((SEQ, w), lambda b: (b, _COL[name]))
    return pl.pallas_call(
        functools.partial(_ctx_attn_kernel, lam_init=lam_init),
        grid=(BATCH,),
        in_specs=[blk("nq", 256), blk("nk", 256), blk("nv", 256),
                  blk("gq", 256), blk("gk", 128), blk("gv", 128),
                  blk("dq", 256), blk("dk", 256), blk("dv", 256),
                  _const_spec((4, DIFF_DIM)), _const_spec((1, 256)), _const_spec((256, 256))],
        out_specs=pl.BlockSpec((SEQ, 768), lambda b: (b, 0)),
        out_shape=jax.ShapeDtypeStruct((N_CTX, 768), F32),
        compiler_params=_cparams("arbitrary"),
        name="ctx_attention",
    )(*([proj] * 9), lp["diff_lambda"], lp["diff_sub_g"], lp["bd64"])


def _gqa_lat_kernel(q_ref, k_ref, v_ref, ck_ref, cv_ref, o_ref):
    o_ref[...] = _mha(q_ref[...], [k_ref[...], ck_ref[...]], [v_ref[...], cv_ref[...]],
                      GQA_HEADS, GQA_HEADS // GQA_KV_HEADS, HEAD_DIM ** -0.5)


def _diff_lat_kernel(q_ref, k_ref, v_ref, ck_ref, cv_ref, lam_ref, subg_ref, bd64_ref, o_ref, *, lam_init):
    lam = _diff_lambda(lam_ref, lam_init)
    o_ref[...] = _diff_attn(q_ref[...], [k_ref[...], ck_ref[...]], [v_ref[...], cv_ref[...]],
                            lam, subg_ref[...], bd64_ref[...], lam_init)


def _nat_lat_kernel(q_ref, k_ref, v_ref, ck_ref, cv_ref, bias_ref, o_ref):
    i = pl.program_id(1)
    k_row0 = jnp.clip(NAT_QROWS * i - NAT_ROWS // 2, 0, GRID_W - NAT_KROWS)
    start = pl.multiple_of(k_row0 * GRID_W, GRID_W)
    kw = k_ref[pl.ds(start, NAT_KROWS * GRID_W), :]
    vw = v_ref[pl.ds(start, NAT_KROWS * GRID_W), :]
    o_ref[...] = _mha(q_ref[...], [kw, ck_ref[...]], [vw, cv_ref[...]], NAT_HEADS, 1, HEAD_DIM ** -0.5,
                      biases=[bias_ref[0], None])


def _nat_bias_table(rpb):
    rows = DEC_SEQ // GRID_W
    nblk = rows // NAT_QROWS
    qr_l, qc = np.divmod(np.arange(NAT_QROWS * GRID_W), GRID_W)
    kr_l, kc = np.divmod(np.arange(NAT_KROWS * GRID_W), GRID_W)
    cs = np.clip(qc - NAT_COLS // 2, 0, GRID_W - NAT_COLS)
    col_ok = (kc[None, :] >= cs[:, None]) & (kc[None, :] < cs[:, None] + NAT_COLS)
    b_idx = np.clip(kc[None, :] - qc[:, None] + NAT_COLS - 1, 0, 2 * NAT_COLS - 2)
    a_idx, valid = [], []
    for blk in (0, 1, nblk - 1):
        r0 = blk * NAT_QROWS
        k0 = int(np.clip(r0 - NAT_ROWS // 2, 0, rows - NAT_KROWS))
        qr = r0 + qr_l
        kr = k0 + kr_l
        rs = np.clip(qr - NAT_ROWS // 2, 0, rows - NAT_ROWS)
        row_ok = (kr[None, :] >= rs[:, None]) & (kr[None, :] < rs[:, None] + NAT_ROWS)
        a_idx.append(np.clip(kr[None, :] - qr[:, None] + NAT_ROWS - 1, 0, 2 * NAT_ROWS - 2))
        valid.append(row_ok & col_ok)
    a_idx = np.stack(a_idx)
    valid = np.stack(valid)
    tab = rpb[:, a_idx, np.broadcast_to(b_idx, a_idx.shape)]
    return jnp.where(valid[None], tab, NEG).transpose(1, 0, 2, 3)


def _lat_attention(proj, caches, lp, l, lam_init):
    cnk, cnv, cgk, cgv, cdk, cdv = caches
    nq_blocks = DEC_SEQ // TQ
    qspec = lambda name: pl.BlockSpec((TQ, 256), lambda b, i: (b * nq_blocks + i, _COL[name]))
    kvspec = lambda name, w: pl.BlockSpec((DEC_SEQ, w), lambda b, i: (b, _COL[name]))
    cspec = lambda w: pl.BlockSpec((None, None, PAST_LEN, w), lambda b, i: (b, l, 0, 0))
    ospec = pl.BlockSpec((TQ, 256), lambda b, i: (b * nq_blocks + i, 0))
    oshape = jax.ShapeDtypeStruct((N_LAT, 256), F32)
    o_gqa = pl.pallas_call(
        _gqa_lat_kernel,
        grid=(DEC_BATCH, nq_blocks),
        in_specs=[qspec("gq"), kvspec("gk", 128), kvspec("gv", 128), cspec(128), cspec(128)],
        out_specs=ospec, out_shape=oshape,
        compiler_params=_cparams("arbitrary", "arbitrary"),
        name="gqa_lat_attention",
    )(proj, proj, proj, cgk.reshape(DEC_BATCH, DEPTH, PAST_LEN, 128), cgv.reshape(DEC_BATCH, DEPTH, PAST_LEN, 128))
    o_dif = pl.pallas_call(
        functools.partial(_diff_lat_kernel, lam_init=lam_init),
        grid=(DEC_BATCH, nq_blocks),
        in_specs=[qspec("dq"), kvspec("dk", 256), kvspec("dv", 256), cspec(256), cspec(256),
                  _const_spec((4, DIFF_DIM)), _const_spec((1, 256)), _const_spec((256, 256))],
        out_specs=ospec, out_shape=oshape,
        compiler_params=_cparams("arbitrary", "arbitrary"),
        name="diff_lat_attention",
    )(proj, proj, proj, cdk.reshape(DEC_BATCH, DEPTH, PAST_LEN, 256), cdv.reshape(DEC_BATCH, DEPTH, PAST_LEN, 256),
      lp["diff_lambda"], lp["diff_sub_g"], lp["bd64"])
    nblk = DEC_SEQ // (NAT_QROWS * GRID_W)
    nat_q = NAT_QROWS * GRID_W
    o_nat = pl.pallas_call(
        _nat_lat_kernel,
        grid=(DEC_BATCH, nblk),
        in_specs=[pl.BlockSpec((nat_q, 256), lambda b, i: (b * nblk + i, _COL["nq"])),
                  kvspec("nk", 256), kvspec("nv", 256), cspec(256), cspec(256),
                  pl.BlockSpec((1, NAT_HEADS, nat_q, NAT_KROWS * GRID_W),
                               lambda b, i: (jnp.where(i == 0, 0, jnp.where(i == nblk - 1, 2, 1)), 0, 0, 0))],
        out_specs=pl.BlockSpec((nat_q, 256), lambda b, i: (b * nblk + i, 0)),
        out_shape=oshape,
        compiler_params=_cparams("arbitrary", "arbitrary"),
        name="nat_lat_attention",
    )(proj, proj, proj, cnk.reshape(DEC_BATCH, DEPTH, PAST_LEN, 256), cnv.reshape(DEC_BATCH, DEPTH, PAST_LEN, 256),
      lp["nat_bias"])
    return o_nat, o_gqa, o_dif


def _merge_kernel(x_ref, mod_ref, n1_ref, b0_ref, b1_ref, b2_ref, b3_ref, wb_ref, wg_ref, bg_ref, wo_ref, o_ref):
    x = x_ref[...]
    mod = mod_ref[0]
    hb = _rms_mod(x, n1_ref[...], mod[:, D_MODEL:2 * D_MODEL], mod[:, 0:D_MODEL]).astype(BF16)
    merged = None
    for n, b_ref in enumerate((b0_ref, b1_ref, b2_ref, b3_ref)):
        cols = slice(n * D_MODEL, (n + 1) * D_MODEL)
        gate = _sigmoid(jnp.dot(hb, wg_ref[:, cols], preferred_element_type=F32) + bg_ref[:, cols])
        term = gate * _dot(b_ref[...], wb_ref[n])
        merged = term if merged is None else merged + term
    out = _dot(merged, wo_ref[...])
    o_ref[...] = x + mod[:, 2 * D_MODEL:3 * D_MODEL] * out


def _merge(x, mod_l, lp, branches, latent):
    n_tok = x.shape[0]
    return pl.pallas_call(
        _merge_kernel,
        grid=(n_tok // TB,),
        in_specs=[pl.BlockSpec((TB, D_MODEL), lambda i: (i, 0)), _mod_spec(latent), _const_spec((1, D_MODEL))]
                 + [pl.BlockSpec((TB, BRANCH_W), functools.partial(lambda i, c: (i, c), c=col)) for _, col in branches]
                 + [_const_spec((N_BRANCH, BRANCH_W, D_MODEL)), _const_spec((D_MODEL, N_BRANCH * D_MODEL)),
                    _const_spec((1, N_BRANCH * D_MODEL)), _const_spec((D_MODEL, D_MODEL))],
        out_specs=pl.BlockSpec((TB, D_MODEL), lambda i: (i, 0)),
        out_shape=jax.ShapeDtypeStruct((n_tok, D_MODEL), F32),
        compiler_params=_cparams("arbitrary"),
        name="merge_lat" if latent else "merge_ctx",
    )(x, mod_l, lp["norm1_g"], *[a for a, _ in branches], lp["w_branch"], lp["w_gate"], lp["b_gate"], lp["w_out"])


def _top_desc(s, k):
    rows = []
    cur = s
    for _ in range(k):
        m = jnp.max(cur, axis=0, keepdims=True)
        rows.append(m)
        cur = jnp.where(cur == m, NEG, cur)
    return jnp.concatenate(rows, axis=0)


def _peer_score_kernel(x_ref, mod_ref, n2_ref, wqt_ref, sk_ref, ht_ref, s_ref, st_ref):
    mod = mod_ref[0]
    h2 = _rms_mod(x_ref[...], n2_ref[...], mod[:, 4 * D_MODEL:5 * D_MODEL], mod[:, 3 * D_MODEL:4 * D_MODEL])
    htb = h2.T.astype(BF16)
    ht_ref[...] = htb
    qt = jnp.dot(wqt_ref[...], htb, preferred_element_type=F32)
    row = lax.broadcasted_iota(jnp.int32, (PEER_TOPK, 1), 0)
    half = PEER_TOPK // 2
    for h in range(PEER_HEADS):
        tops = []
        for p in range(2):
            r0 = (2 * h + p) * (PEER_QDIM // 2)
            s = jnp.dot(sk_ref[p], qt[r0:r0 + PEER_QDIM // 2, :].astype(BF16), preferred_element_type=F32)
            s_ref[p, h] = s
            tops.append(_top_desc(s, PEER_TOPK))
        t1, t2 = tops
        cands = [t1[0:1] + t2, jnp.where(row >= 1, t1 + t2[0:1], NEG)]
        for a in range(1, half):
            nb = PEER_TOPK // (a + 1)
            cands.append(jnp.where((row[0:half] >= 1) & (row[0:half] < nb), t1[a:a + 1] + t2[0:half], NEG))
        best = _top_desc(jnp.concatenate(cands, axis=0), PEER_TOPK)
        z = jnp.sum(jnp.exp(best - best[0:1]), axis=0, keepdims=True)
        st_ref[4 * h:4 * h + 4, :] = jnp.concatenate(
            [best[PEER_TOPK - 1:PEER_TOPK], t1[0:1], t2[0:1], 1.0 / z], axis=0)


def _peer_dense_kernel(ht_ref, s_ref, st_ref, u_ref, v_ref, x_ref, mod_ref, o_ref, acc_ref, e1_ref, e2_ref):
    e = pl.program_id(1)

    @pl.when(e == 0)
    def _():
        acc_ref[...] = jnp.zeros_like(acc_ref)
        for h in range(PEER_HEADS):
            st = st_ref[4 * h:4 * h + 4, :]
            e1_ref[h] = jnp.exp(s_ref[0, h] - st[1:2]) * st[3:4]
            e2_ref[h] = jnp.exp(s_ref[1, h] - st[2:3])

    a = jnp.dot(u_ref[...], ht_ref[...], preferred_element_type=F32)
    ws = []
    for rr in range(PEER_ET // PEER_KEYS):
        r = e * (PEER_ET // PEER_KEYS) + rr
        g = None
        for h in range(PEER_HEADS):
            s1r = s_ref[0, h, pl.ds(r, 1), :]
            e1r = e1_ref[h, pl.ds(r, 1), :]
            sel = (s_ref[1, h] + s1r) >= st_ref[4 * h:4 * h + 1, :]
            term = jnp.where(sel, e2_ref[h] * e1r, 0.0)
            g = term if g is None else g + term
        ws.append(g * _gelu(a[rr * PEER_KEYS:(rr + 1) * PEER_KEYS, :]))
    w = jnp.concatenate(ws, axis=0)
    acc_ref[...] += jnp.dot(w.T.astype(BF16), v_ref[...], preferred_element_type=F32)

    @pl.when(e == pl.num_programs(1) - 1)
    def _():
        o_ref[...] = x_ref[...] + mod_ref[0][:, 5 * D_MODEL:6 * D_MODEL] * acc_ref[...]


def _peer(x, mod_l, lp, latent):
    n_tok = x.shape[0]
    nb = n_tok // TBP
    if latent:
        mod_spec = pl.BlockSpec((1, 1, 6 * D_MODEL), lambda i, *_: (1 + i // (DEC_SEQ // TBP), 0, 0))
    else:
        mod_spec = pl.BlockSpec((1, 1, 6 * D_MODEL), lambda i, *_: (0, 0, 0))
    n_keys2 = PEER_HEADS * PEER_QDIM
    ht, s, st = pl.pallas_call(
        _peer_score_kernel,
        grid=(nb,),
        in_specs=[pl.BlockSpec((TBP, D_MODEL), lambda i: (i, 0)), mod_spec, _const_spec((1, D_MODEL)),
                  _const_spec((n_keys2, D_MODEL)), _const_spec((2, PEER_KEYS, PEER_QDIM // 2))],
        out_specs=[pl.BlockSpec((D_MODEL, TBP), lambda i: (0, i)),
                   pl.BlockSpec((2, PEER_HEADS, PEER_KEYS, TBP), lambda i: (0, 0, 0, i)),
                   pl.BlockSpec((4 * PEER_HEADS, TBP), lambda i: (0, i))],
        out_shape=[jax.ShapeDtypeStruct((D_MODEL, n_tok), BF16),
                   jax.ShapeDtypeStruct((2, PEER_HEADS, PEER_KEYS, n_tok), F32),
                   jax.ShapeDtypeStruct((4 * PEER_HEADS, n_tok), F32)],
        compiler_params=_cparams("arbitrary"),
        name="peer_scores_lat" if latent else "peer_scores_ctx",
    )(x, mod_l, lp["norm2_g"], lp["peer_wqt"], lp["peer_subkeys"])
    n_et = PEER_KEYS * PEER_KEYS // PEER_ET
    return pl.pallas_call(
        _peer_dense_kernel,
        grid=(nb, n_et),
        in_specs=[pl.BlockSpec((D_MODEL, TBP), lambda i, e: (0, i)),
                  pl.BlockSpec((2, PEER_HEADS, PEER_KEYS, TBP), lambda i, e: (0, 0, 0, i)),
                  pl.BlockSpec((4 * PEER_HEADS, TBP), lambda i, e: (0, i)),
                  pl.BlockSpec((PEER_ET, D_MODEL), lambda i, e: (e, 0)),
                  pl.BlockSpec((PEER_ET, D_MODEL), lambda i, e: (e, 0)),
                  pl.BlockSpec((TBP, D_MODEL), lambda i, e: (i, 0)),
                  mod_spec],
        out_specs=pl.BlockSpec((TBP, D_MODEL), lambda i, e: (i, 0)),
        out_shape=jax.ShapeDtypeStruct((n_tok, D_MODEL), F32),
        scratch_shapes=[pltpu.VMEM((TBP, D_MODEL), F32),
                        pltpu.VMEM((PEER_HEADS, PEER_KEYS, TBP), F32),
                        pltpu.VMEM((PEER_HEADS, PEER_KEYS, TBP), F32)],
        compiler_params=_cparams("arbitrary", "arbitrary"),
        name="peer_dense_lat" if latent else "peer_dense_ctx",
    )(ht, s, st, lp["peer_u"], lp["peer_v"], x, mod_l)


def _layer_params(l, w):
    tile = lambda g, n: jnp.tile(g, n)
    gains = jnp.stack([
        tile(w["nat_qk_g"][l, 0], 4), tile(w["nat_qk_g"][l, 1], 4),
        tile(w["gqa_qk_g"][l, 0], 4), tile(w["gqa_qk_g"][l, 1], 4),
        tile(w["diff_qk_g"][l, 0], 8), tile(w["diff_qk_g"][l, 1], 8),
        jnp.zeros((256,), F32), jnp.zeros((256,), F32)])
    return dict(
        norm1_g=w["norm1_g"][l].reshape(1, D_MODEL),
        norm2_g=w["norm2_g"][l].reshape(1, D_MODEL),
        w_in=w["w_in"][l].astype(BF16),
        qk_gains=gains,
        bd64=_block_ones(256, 64), bd32=_block_ones(256, 32), bd256=_block_ones(256, 256),
        rope=w["rope"],
        sgu_norm_g=w["sgu_norm_g"][l].reshape(1, SGU_WIDTH),
        sgu_w=w["sgu_w"][l].astype(BF16),
        sgu_b=jnp.repeat(w["sgu_b"][l].T, SGU_WIDTH // SGU_GROUPS, axis=1),
        diff_lambda=w["diff_lambda"][l],
        diff_sub_g=tile(w["diff_sub_g"][l], 4).reshape(1, 256),
        nat_bias=_nat_bias_table(w["nat_rpb"][l]),
        w_branch=w["w_branch"][l].astype(BF16),
        w_gate=w["w_gate"][l].astype(BF16),
        b_gate=w["b_gate"][l].reshape(1, N_BRANCH * D_MODEL),
        w_out=w["w_out"][l].astype(BF16),
        peer_wqt=w["peer_wq"][l].T.astype(BF16),
        peer_subkeys=w["peer_subkeys"][l].astype(BF16),
        peer_u=w["peer_u"][l].astype(BF16),
        peer_v=w["peer_v"][l].astype(BF16),
    )


def kernel(x_prompt, x_sample, c, cache_nat_k, cache_nat_v, cache_gqa_k, cache_gqa_v, cache_diff_k, cache_diff_v, c_ctx, w_mod, b_mod, norm1_g, norm2_g, w_in, nat_qk_g, nat_rpb, gqa_qk_g, diff_qk_g, diff_lambda, diff_sub_g, sgu_norm_g, sgu_w, sgu_b, w_branch, w_gate, b_gate, w_out, peer_wq, peer_subkeys, peer_u, peer_v):
    w = dict(norm1_g=norm1_g, norm2_g=norm2_g, w_in=w_in, nat_qk_g=nat_qk_g, nat_rpb=nat_rpb,
             gqa_qk_g=gqa_qk_g, diff_qk_g=diff_qk_g, diff_lambda=diff_lambda, diff_sub_g=diff_sub_g,
             sgu_norm_g=sgu_norm_g, sgu_w=sgu_w, sgu_b=sgu_b, w_branch=w_branch, w_gate=w_gate,
             b_gate=b_gate, w_out=w_out, peer_wq=peer_wq, peer_subkeys=peer_subkeys, peer_u=peer_u,
             peer_v=peer_v, rope=_rope_tables())
    cvec = jnp.concatenate([c_ctx[None], c, jnp.zeros((MOD_ROWS - 1 - DEC_BATCH, D_MODEL), F32)], axis=0)
    mod = _modulation(cvec, w_mod, b_mod).reshape(DEPTH, MOD_ROWS, 1, 6 * D_MODEL)
    xp = x_prompt.reshape(N_CTX, D_MODEL)
    xs = x_sample.reshape(N_LAT, D_MODEL)
    new = []
    for l in range(DEPTH):
        lp = _layer_params(l, w)
        lam_init = 0.8 - 0.6 * math.exp(-0.3 * l)
        caches = (cache_nat_k, cache_nat_v, cache_gqa_k, cache_gqa_v, cache_diff_k, cache_diff_v)
        proj = _in_projection(xp, mod[l], lp, latent=False)
        new.append(proj)
        attn = _ctx_attention(proj, lp, lam_init)
        xp = _merge(xp, mod[l], lp, [(attn, 0), (attn, 1), (attn, 2), (proj, _COL["sgu"])], latent=False)
        xp = _peer(xp, mod[l], lp, latent=False)
        proj = _in_projection(xs, mod[l], lp, latent=True)
        o_nat, o_gqa, o_dif = _lat_attention(proj, caches, lp, l, lam_init)
        xs = _merge(xs, mod[l], lp, [(o_nat, 0), (o_gqa, 0), (o_dif, 0), (proj, _COL["sgu"])], latent=True)
        xs = _peer(xs, mod[l], lp, latent=True)

    def cache_out(lo, width, tail):
        a = jnp.stack([p[:, lo:lo + width].reshape(BATCH, SEQ, width) for p in new], axis=1)
        return a.reshape((BATCH, DEPTH, SEQ) + tail)

    return (xp.reshape(BATCH, SEQ, D_MODEL), xs.reshape(DEC_BATCH, DEC_SEQ, D_MODEL),
            cache_out(256, 256, (NAT_HEADS, HEAD_DIM)), cache_out(512, 256, (NAT_HEADS, HEAD_DIM)),
            cache_out(1024, 128, (GQA_KV_HEADS, HEAD_DIM)), cache_out(1152, 128, (GQA_KV_HEADS, HEAD_DIM)),
            cache_out(1536, 256, (DIFF_HEADS, 2, DIFF_DIM)), cache_out(1792, 256, (DIFF_HEADS, HEAD_DIM)))
```

```python
import functools
import math

import numpy as np
import jax
import jax.numpy as jnp
from jax import lax
from jax.experimental import pallas as pl
from jax.experimental.pallas import tpu as pltpu

F32 = jnp.float32
BF16 = jnp.bfloat16

D_MODEL = 1024
BATCH = 16
SEQ = 256
DEPTH = 2
DEC_BATCH = 2
DEC_SEQ = 4096
PAST_LEN = 512
GRID_W = 64
HEAD_DIM = 64
NAT_HEADS = 4
NAT_ROWS = 8
NAT_COLS = 16
GQA_HEADS = 4
GQA_KV_HEADS = 2
DIFF_HEADS = 4
DIFF_DIM = 32
SGU_WIDTH = 256
SGU_GROUPS = 4
CHUNK = 128
N_BRANCH = 4
BRANCH_W = 256
IN_W = 2560
PEER_HEADS = 8
PEER_KEYS = 128
PEER_QDIM = 256
PEER_TOPK = 16
ROPE_BASE = 10000.0
EPS = 1e-6

N_CTX = BATCH * SEQ
N_LAT = DEC_BATCH * DEC_SEQ
MOD_ROWS = 8
TB = 256
TQ = 256
PROJ_W = 2304
NAT_QROWS = 4
NAT_KROWS = NAT_QROWS + NAT_ROWS
TBP = 512
PEER_ET = 512
VMEM_LIMIT = 56 * 1024 * 1024
NEG = -1e30


def _cparams(*sem):
    return pltpu.CompilerParams(dimension_semantics=sem, vmem_limit_bytes=VMEM_LIMIT)


def _const_spec(shape):
    return pl.BlockSpec(shape, lambda *_: (0,) * len(shape))


def _gelu(x):
    return 0.5 * x * (1.0 + jnp.tanh(0.7978845608028654 * (x + 0.044715 * (x * x * x))))


def _sigmoid(x):
    return 1.0 / (1.0 + jnp.exp(-x))


def _dot(a, b):
    return jnp.dot(a.astype(BF16), b.astype(BF16), preferred_element_type=F32)


def _dot_nt(a, b):
    return lax.dot_general(a.astype(BF16), b.astype(BF16), (((1,), (1,)), ((), ())),
                           preferred_element_type=F32)


def _group_mean(y2, ones_bd):
    hi = y2.astype(BF16)
    lo = (y2 - hi.astype(F32)).astype(BF16)
    return (jnp.dot(hi, ones_bd, preferred_element_type=F32)
            + jnp.dot(lo, ones_bd, preferred_element_type=F32))


def _block_ones(width, group):
    idx = np.arange(width) // group
    return jnp.asarray((idx[:, None] == idx[None, :]).astype(np.float32) / group, dtype=BF16)


def _rms_mod(x, gain, scale, shift):
    xn = x * lax.rsqrt(jnp.mean(x * x, axis=-1, keepdims=True) + EPS) * gain
    return xn * (1.0 + scale) + shift


def _mod_spec(latent):
    if latent:
        return pl.BlockSpec((1, 1, 6 * D_MODEL), lambda i, *_: (1 + i // (DEC_SEQ // TB), 0, 0))
    return pl.BlockSpec((1, 1, 6 * D_MODEL), lambda i, *_: (0, 0, 0))


def _mod_kernel(c_ref, w_ref, b_ref, o_ref):
    c = c_ref[...]
    s = c * _sigmoid(c)
    o_ref[0] = _dot(s, w_ref[0]) + b_ref[0]


def _modulation(cvec, w_mod, b_mod):
    tn = 1536
    return pl.pallas_call(
        _mod_kernel,
        grid=(DEPTH, 6 * D_MODEL // tn),
        in_specs=[pl.BlockSpec((MOD_ROWS, D_MODEL), lambda l, j: (0, 0)),
                  pl.BlockSpec((1, D_MODEL, tn), lambda l, j: (l, 0, j)),
                  pl.BlockSpec((1, 1, tn), lambda l, j: (l, 0, j))],
        out_specs=pl.BlockSpec((1, MOD_ROWS, tn), lambda l, j: (l, 0, j)),
        out_shape=jax.ShapeDtypeStruct((DEPTH, MOD_ROWS, 6 * D_MODEL), F32),
        compiler_params=_cparams("arbitrary", "arbitrary"),
        name="modulation",
    )(cvec, w_mod, b_mod.reshape(DEPTH, 1, 6 * D_MODEL))


def _rope_tables():
    t = np.arange(DEC_SEQ)
    pos = (t // GRID_W, t % GRID_W)
    out = []
    for d in (HEAD_DIM, DIFF_DIM):
        half, quarter = d // 2, d // 4
        inv = ROPE_BASE ** (-np.arange(quarter, dtype=np.float32) * 2.0 / half)
        lane = np.arange(128) % d
        part, j = lane // half, lane % half
        ang = np.stack([pos[0][:, None] * inv[None, :], pos[1][:, None] * inv[None, :]], axis=1)
        a = ang[:, part, j % quarter].astype(np.float32)
        cos, sin = np.cos(a), np.sin(a)
        out += [jnp.asarray(cos, dtype=F32),
                jnp.asarray(np.where(j < quarter, -sin, 0.0), dtype=F32),
                jnp.asarray(np.where(j >= quarter, sin, 0.0), dtype=F32)]
    return out


def _rope(y, c, sa, sb, quarter):
    w = y.shape[-1]
    rep = w // 128
    if rep > 1:
        c, sa, sb = (jnp.concatenate([t] * rep, axis=-1) for t in (c, sa, sb))
    up = pltpu.roll(y, w - quarter, 1)
    dn = pltpu.roll(y, quarter, 1)
    return y * c + up * sa + dn * sb


def _inproj_kernel(*refs, latent):
    (x_ref, mod_ref, n1_ref, w_ref, g_ref, bd64_ref, bd32_ref, bd256_ref,
     sgn_ref, sgw_ref, sgb_ref) = refs[:11]
    o_ref = refs[-1]
    mod = mod_ref[0]
    h = _rms_mod(x_ref[...], n1_ref[...], mod[:, D_MODEL:2 * D_MODEL], mod[:, 0:D_MODEL])
    y = jnp.dot(h.astype(BF16), w_ref[...], preferred_element_type=F32)

    def qk_norm(lo, width, bd, gain_row):
        v = y[:, lo:lo + width]
        ms = _group_mean(v * v, bd)
        return v * lax.rsqrt(ms + EPS) * g_ref[gain_row:gain_row + 1, 0:width]

    bd64 = bd64_ref[...]
    bd32 = bd32_ref[...]
    if latent:
        c64, sa64, sb64, c32, sa32, sb32 = (r[...] for r in refs[11:17])
        rope64 = functools.partial(_rope, c=c64, sa=sa64, sb=sb64, quarter=HEAD_DIM // 4)
        rope32 = functools.partial(_rope, c=c32, sa=sa32, sb=sb32, quarter=DIFF_DIM // 4)
    else:
        rope64 = rope32 = lambda v: v
    o_ref[:, 0:256] = qk_norm(0, 256, bd64, 0)
    o_ref[:, 256:512] = qk_norm(256, 256, bd64, 1)
    o_ref[:, 512:768] = y[:, 512:768]
    o_ref[:, 768:1024] = rope64(qk_norm(768, 256, bd64, 2))
    o_ref[:, 1024:1152] = rope64(qk_norm(1024, 128, bd64[0:128, 0:128], 3))
    o_ref[:, 1152:1280] = y[:, 1152:1280]
    o_ref[:, 1280:1536] = rope32(qk_norm(1280, 256, bd32, 4))
    o_ref[:, 1536:1792] = rope32(qk_norm(1536, 256, bd32, 5))
    o_ref[:, 1792:2048] = y[:, 1792:2048]
    u = _gelu(y[:, 2048:2304])
    v = _gelu(y[:, 2304:2560])
    vn = v * lax.rsqrt(_group_mean(v * v, bd256_ref[...]) + EPS) * sgn_ref[...]
    vnb = vn.astype(BF16)
    lane_group = lax.broadcasted_iota(jnp.int32, (CHUNK, SGU_WIDTH), 1) // (SGU_WIDTH // SGU_GROUPS)
    for n in range(TB // CHUNK):
        vc = vnb[n * CHUNK:(n + 1) * CHUNK, :]
        s = sgb_ref[...]
        for g in range(SGU_GROUPS):
            sg = jnp.dot(sgw_ref[g], vc, preferred_element_type=F32)
            s = s + jnp.where(lane_group == g, sg, 0.0)
        o_ref[n * CHUNK:(n + 1) * CHUNK, 2048:2304] = u[n * CHUNK:(n + 1) * CHUNK, :] * s


def _in_projection(x, mod_l, lp, latent):
    n_tok = x.shape[0]
    in_specs = [pl.BlockSpec((TB, D_MODEL), lambda i: (i, 0)),
                _mod_spec(latent),
                _const_spec((1, D_MODEL)),
                _const_spec((D_MODEL, IN_W)),
                _const_spec((8, 256)),
                _const_spec((256, 256)), _const_spec((256, 256)), _const_spec((256, 256)),
                _const_spec((1, SGU_WIDTH)),
                _const_spec((SGU_GROUPS, CHUNK, CHUNK)),
                _const_spec((CHUNK, SGU_WIDTH))]
    args = [x, mod_l, lp["norm1_g"], lp["w_in"], lp["qk_gains"], lp["bd64"], lp["bd32"], lp["bd256"],
            lp["sgu_norm_g"], lp["sgu_w"], lp["sgu_b"]]
    if latent:
        in_specs += [pl.BlockSpec((TB, 128), lambda i: (i % (DEC_SEQ // TB), 0))] * 6
        args += lp["rope"]
    return pl.pallas_call(
        functools.partial(_inproj_kernel, latent=latent),
        grid=(n_tok // TB,),
        in_specs=in_specs,
        out_specs=pl.BlockSpec((TB, PROJ_W), lambda i: (i, 0)),
        out_shape=jax.ShapeDtypeStruct((n_tok, PROJ_W), F32),
        compiler_params=_cparams("arbitrary"),
        name="in_projection_lat" if latent else "in_projection_ctx",
    )(*args)


def _softmax_parts(scores):
    m = functools.reduce(jnp.maximum, [jnp.max(s, axis=-1, keepdims=True) for s in scores])
    ps = [jnp.exp(s - m) for s in scores]
    l = functools.reduce(jnp.add, [jnp.sum(p, axis=-1, keepdims=True) for p in ps])
    return ps, 1.0 / l


def _diff_lambda(lam_ref, lam_init):
    lv = lam_ref[...]
    a = jnp.sum(lv[0:1] * lv[1:2], axis=-1, keepdims=True)
    b = jnp.sum(lv[2:3] * lv[3:4], axis=-1, keepdims=True)
    return jnp.exp(a) - jnp.exp(b) + lam_init


def _head(ref_or_val, h, width=HEAD_DIM):
    return ref_or_val[:, h * width:(h + 1) * width]


def _mha(q, ks, vs, n_heads, kv_group, scale, biases=None):
    outs = []
    for h in range(n_heads):
        g = h // kv_group
        qh = _head(q, h) * scale
        scores = [_dot_nt(qh, _head(k, g)) for k in ks]
        if biases is not None:
            scores = [s if b is None else s + b[h] for s, b in zip(scores, biases)]
        ps, rl = _softmax_parts(scores)
        o = functools.reduce(jnp.add, [_dot(p, _head(v, g)) for p, v in zip(ps, vs)])
        outs.append(o * rl)
    return jnp.concatenate(outs, axis=-1)


def _diff_attn(q, ks, vs, lam, sub_gain, bd64, lam_init):
    scale = DIFF_DIM ** -0.5
    outs = []
    for h in range(DIFF_HEADS):
        pd = None
        for m in range(2):
            qm = _head(q, 2 * h + m, DIFF_DIM)
            scores = [_dot_nt(qm, _head(k, 2 * h + m, DIFF_DIM)) * scale for k in ks]
            ps, rl = _softmax_parts(scores)
            if m == 0:
                pd = [p * rl for p in ps]
            else:
                pd = [a - lam * (p * rl) for a, p in zip(pd, ps)]
        outs.append(functools.reduce(jnp.add, [_dot(p, _head(v, h)) for p, v in zip(pd, vs)]))
    o = jnp.concatenate(outs, axis=-1)
    ms = _group_mean(o * o, bd64)
    return o * lax.rsqrt(ms + EPS) * sub_gain * (1.0 - lam_init)


def _ctx_attn_kernel(nq, nk, nv, gq, gk, gv, dq, dk, dv, lam_ref, subg_ref, bd64_ref, o_ref, *, lam_init):
    scale = HEAD_DIM ** -0.5
    o_ref[:, 0:256] = _mha(nq[...], [nk[...]], [nv[...]], NAT_HEADS, 1, scale)
    o_ref[:, 256:512] = _mha(gq[...], [gk[...]], [gv[...]], GQA_HEADS, GQA_HEADS // GQA_KV_HEADS, scale)
    lam = _diff_lambda(lam_ref, lam_init)
    o_ref[:, 512:768] = _diff_attn(dq[...], [dk[...]], [dv[...]], lam, subg_ref[...], bd64_ref[...], lam_init)


_COL = dict(nq=0, nk=1, nv=2, gq=3, gk=8, gv=9, dq=5, dk=6, dv=7, sgu=8)


def _ctx_attention(proj, lp, lam_init):
    blk = lambda name, w: pl.BlockSpec((SEQ, w), lambda b: (b, _COL[name]))
    return pl.pallas_call(
        functools.partial(_ctx_attn_kernel, lam_init=lam_init),
        grid=(BATCH,),
        in_specs=[blk("nq", 256), blk("nk", 256), blk("nv", 256),
                  blk("gq", 256), blk("gk", 128), blk("gv", 128),
                  blk("dq", 256), blk("dk", 256), blk("dv", 256),
                  _const_spec((4, DIFF_DIM)), _const_spec((1, 256)), _const_spec((256, 256))],
        out_specs=pl.BlockSpec((SEQ, 768), lambda b: (b, 0)),
        out_shape=jax.ShapeDtypeStruct((N_CTX, 768), F32),
        compiler_params=_cparams("arbitrary"),
        name="ctx_attention",
    )(*([proj] * 9), lp["diff_lambda"], lp["diff_sub_g"], lp["bd64"])


def _gqa_lat_kernel(q_ref, k_ref, v_ref, ck_ref, cv_ref, o_ref):
    o_ref[...] = _mha(q_ref[...], [k_ref[...], ck_ref[...]], [v_ref[...], cv_ref[...]],
                      GQA_HEADS, GQA_HEADS // GQA_KV_HEADS, HEAD_DIM ** -0.5)


def _diff_lat_kernel(q_ref, k_ref, v_ref, ck_ref, cv_ref, lam_ref, subg_ref, bd64_ref, o_ref, *, lam_init):
    lam = _diff_lambda(lam_ref, lam_init)
    o_ref[...] = _diff_attn(q_ref[...], [k_ref[...], ck_ref[...]], [v_ref[...], cv_ref[...]],
                            lam, subg_ref[...], bd64_ref[...], lam_init)


def _nat_lat_kernel(q_ref, k_ref, v_ref, ck_ref, cv_ref, bias_ref, o_ref):
    i = pl.program_id(1)
    k_row0 = jnp.clip(NAT_QROWS * i - NAT_ROWS // 2, 0, GRID_W - NAT_KROWS)
    start = pl.multiple_of(k_row0 * GRID_W, GRID_W)
    kw = k_ref[pl.ds(start, NAT_KROWS * GRID_W), :]
    vw = v_ref[pl.ds(start, NAT_KROWS * GRID_W), :]
    o_ref[...] = _mha(q_ref[...], [kw, ck_ref[...]], [vw, cv_ref[...]], NAT_HEADS, 1, HEAD_DIM ** -0.5,
                      biases=[bias_ref[0], None])


def _nat_bias_table(rpb):
    rows = DEC_SEQ // GRID_W
    nblk = rows // NAT_QROWS
    pad = jnp.pad(rpb, ((0, 0), (0, 0), (GRID_W - NAT_COLS, GRID_W - NAT_COLS)))
    toep = jnp.stack([pad[:, :, GRID_W - 1 - c:2 * GRID_W - 1 - c] for c in range(GRID_W)], axis=2)
    col = np.arange(GRID_W)
    cs = np.clip(col - NAT_COLS // 2, 0, GRID_W - NAT_COLS)
    col_ok = (col[None, :] >= cs[:, None]) & (col[None, :] < cs[:, None] + NAT_COLS)
    toep = jnp.where(col_ok, toep, NEG)
    masked = jnp.full((NAT_HEADS, GRID_W, GRID_W), NEG, F32)
    cases = []
    for blk in (0, 1, nblk - 1):
        r0 = blk * NAT_QROWS
        k0 = int(np.clip(r0 - NAT_ROWS // 2, 0, rows - NAT_KROWS))
        q_rows = []
        for qr in range(r0, r0 + NAT_QROWS):
            rs = int(np.clip(qr - NAT_ROWS // 2, 0, rows - NAT_ROWS))
            q_rows.append(jnp.concatenate(
                [toep[:, kr - qr + NAT_ROWS - 1] if rs <= kr < rs + NAT_ROWS else masked
                 for kr in range(k0, k0 + NAT_KROWS)], axis=-1))
        cases.append(jnp.concatenate(q_rows, axis=-2))
    return jnp.stack(cases, axis=0)


def _lat_attention(proj, caches, lp, l, lam_init):
    cnk, cnv, cgk, cgv, cdk, cdv = caches
    nq_blocks = DEC_SEQ // TQ
    qspec = lambda name: pl.BlockSpec((TQ, 256), lambda b, i: (b * nq_blocks + i, _COL[name]))
    kvspec = lambda name, w: pl.BlockSpec((DEC_SEQ, w), lambda b, i: (b, _COL[name]))
    cspec = lambda w: pl.BlockSpec((None, None, PAST_LEN, w), lambda b, i: (b, l, 0, 0))
    ospec = pl.BlockSpec((TQ, 256), lambda b, i: (b * nq_blocks + i, 0))
    oshape = jax.ShapeDtypeStruct((N_LAT, 256), F32)
    o_gqa = pl.pallas_call(
        _gqa_lat_kernel,
        grid=(DEC_BATCH, nq_blocks),
        in_specs=[qspec("gq"), kvspec("gk", 128), kvspec("gv", 128), cspec(128), cspec(128)],
        out_specs=ospec, out_shape=oshape,
        compiler_params=_cparams("arbitrary", "arbitrary"),
        name="gqa_lat_attention",
    )(proj, proj, proj, cgk.reshape(DEC_BATCH, DEPTH, PAST_LEN, 128), cgv.reshape(DEC_BATCH, DEPTH, PAST_LEN, 128))
    o_dif = pl.pallas_call(
        functools.partial(_diff_lat_kernel, lam_init=lam_init),
        grid=(DEC_BATCH, nq_blocks),
        in_specs=[qspec("dq"), kvspec("dk", 256), kvspec("dv", 256), cspec(256), cspec(256),
                  _const_spec((4, DIFF_DIM)), _const_spec((1, 256)), _const_spec((256, 256))],
        out_specs=ospec, out_shape=oshape,
        compiler_params=_cparams("arbitrary", "arbitrary"),
        name="diff_lat_attention",
    )(proj, proj, proj, cdk.reshape(DEC_BATCH, DEPTH, PAST_LEN, 256), cdv.reshape(DEC_BATCH, DEPTH, PAST_LEN, 256),
      lp["diff_lambda"], lp["diff_sub_g"], lp["bd64"])
    nblk = DEC_SEQ // (NAT_QROWS * GRID_W)
    nat_q = NAT_QROWS * GRID_W
    o_nat = pl.pallas_call(
        _nat_lat_kernel,
        grid=(DEC_BATCH, nblk),
        in_specs=[pl.BlockSpec((nat_q, 256), lambda b, i: (b * nblk + i, _COL["nq"])),
                  kvspec("nk", 256), kvspec("nv", 256), cspec(256), cspec(256),
                  pl.BlockSpec((1, NAT_HEADS, nat_q, NAT_KROWS * GRID_W),
                               lambda b, i: (jnp.where(i == 0, 0, jnp.where(i == nblk - 1, 2, 1)), 0, 0, 0))],
        out_specs=pl.BlockSpec((nat_q, 256), lambda b, i: (b * nblk + i, 0)),
        out_shape=oshape,
        compiler_params=_cparams("arbitrary", "arbitrary"),
        name="nat_lat_attention",
    )(proj, proj, proj, cnk.reshape(DEC_BATCH, DEPTH, PAST_LEN, 256), cnv.reshape(DEC_BATCH, DEPTH, PAST_LEN, 256),
      lp["nat_bias"])
    return o_nat, o_gqa, o_dif


def _merge_kernel(x_ref, mod_ref, n1_ref, b0_ref, b1_ref, b2_ref, b3_ref, wb_ref, wg_ref, bg_ref, wo_ref, o_ref):
    x = x_ref[...]
    mod = mod_ref[0]
    hb = _rms_mod(x, n1_ref[...], mod[:, D_MODEL:2 * D_MODEL], mod[:, 0:D_MODEL]).astype(BF16)
    merged = None
    for n, b_ref in enumerate((b0_ref, b1_ref, b2_ref, b3_ref)):
        cols = slice(n * D_MODEL, (n + 1) * D_MODEL)
        gate = _sigmoid(jnp.dot(hb, wg_ref[:, cols], preferred_element_type=F32) + bg_ref[:, cols])
        term = gate * _dot(b_ref[...], wb_ref[n])
        merged = term if merged is None else merged + term
    out = _dot(merged, wo_ref[...])
    o_ref[...] = x + mod[:, 2 * D_MODEL:3 * D_MODEL] * out


def _merge(x, mod_l, lp, branches, latent):
    n_tok = x.shape[0]
    return pl.pallas_call(
        _merge_kernel,
        grid=(n_tok // TB,),
        in_specs=[pl.BlockSpec((TB, D_MODEL), lambda i: (i, 0)), _mod_spec(latent), _const_spec((1, D_MODEL))]
                 + [pl.BlockSpec((TB, BRANCH_W), functools.partial(lambda i, c: (i, c), c=col)) for _, col in branches]
                 + [_const_spec((N_BRANCH, BRANCH_W, D_MODEL)), _const_spec((D_MODEL, N_BRANCH * D_MODEL)),
                    _const_spec((1, N_BRANCH * D_MODEL)), _const_spec((D_MODEL, D_MODEL))],
        out_specs=pl.BlockSpec((TB, D_MODEL), lambda i: (i, 0)),
        out_shape=jax.ShapeDtypeStruct((n_tok, D_MODEL), F32),
        compiler_params=_cparams("arbitrary"),
        name="merge_lat" if latent else "merge_ctx",
    )(x, mod_l, lp["norm1_g"], *[a for a, _ in branches], lp["w_branch"], lp["w_gate"], lp["b_gate"], lp["w_out"])


def _top_desc(s, k):
    rows = []
    cur = s
    for _ in range(k):
        m = jnp.max(cur, axis=0, keepdims=True)
        rows.append(m)
        cur = jnp.where(cur == m, NEG, cur)
    return jnp.concatenate(rows, axis=0)


def _peer_score_kernel(x_ref, mod_ref, n2_ref, wqt_ref, sk_ref, ht_ref, s_ref, st_ref):
    mod = mod_ref[0]
    h2 = _rms_mod(x_ref[...], n2_ref[...], mod[:, 4 * D_MODEL:5 * D_MODEL], mod[:, 3 * D_MODEL:4 * D_MODEL])
    htb = h2.T.astype(BF16)
    ht_ref[...] = htb
    qt = jnp.dot(wqt_ref[...], htb, preferred_element_type=F32)
    row = lax.broadcasted_iota(jnp.int32, (PEER_TOPK, 1), 0)
    half = PEER_TOPK // 2
    for h in range(PEER_HEADS):
        tops = []
        for p in range(2):
            r0 = (2 * h + p) * (PEER_QDIM // 2)
            s = jnp.dot(sk_ref[p], qt[r0:r0 + PEER_QDIM // 2, :].astype(BF16), preferred_element_type=F32)
            s_ref[p, h] = s
            tops.append(_top_desc(s, PEER_TOPK))
        t1, t2 = tops
        cands = [t1[0:1] + t2, jnp.where(row >= 1, t1 + t2[0:1], NEG)]
        for a in range(1, half):
            nb = PEER_TOPK // (a + 1)
            cands.append(jnp.where((row[0:half] >= 1) & (row[0:half] < nb), t1[a:a + 1] + t2[0:half], NEG))
        best = _top_desc(jnp.concatenate(cands, axis=0), PEER_TOPK)
        z = jnp.sum(jnp.exp(best - best[0:1]), axis=0, keepdims=True)
        st_ref[4 * h:4 * h + 4, :] = jnp.concatenate(
            [best[PEER_TOPK - 1:PEER_TOPK], t1[0:1], t2[0:1], 1.0 / z], axis=0)


def _peer_dense_kernel(ht_ref, s_ref, st_ref, u_ref, v_ref, x_ref, mod_ref, o_ref, acc_ref, e1_ref, e2_ref):
    e = pl.program_id(1)

    @pl.when(e == 0)
    def _():
        acc_ref[...] = jnp.zeros_like(acc_ref)
        for h in range(PEER_HEADS):
            st = st_ref[4 * h:4 * h + 4, :]
            e1_ref[h] = jnp.exp(s_ref[0, h] - st[1:2]) * st[3:4]
            e2_ref[h] = jnp.exp(s_ref[1, h] - st[2:3])

    a = jnp.dot(u_ref[...], ht_ref[...], preferred_element_type=F32)
    ws = []
    for rr in range(PEER_ET // PEER_KEYS):
        r = e * (PEER_ET // PEER_KEYS) + rr
        g = None
        for h in range(PEER_HEADS):
            s1r = s_ref[0, h, pl.ds(r, 1), :]
            e1r = e1_ref[h, pl.ds(r, 1), :]
            sel = (s_ref[1, h] + s1r) >= st_ref[4 * h:4 * h + 1, :]
            term = jnp.where(sel, e2_ref[h] * e1r, 0.0)
            g = term if g is None else g + term
        ws.append(g * _gelu(a[rr * PEER_KEYS:(rr + 1) * PEER_KEYS, :]))
    w = jnp.concatenate(ws, axis=0)
    acc_ref[...] += jnp.dot(w.T.astype(BF16), v_ref[...], preferred_element_type=F32)

    @pl.when(e == pl.num_programs(1) - 1)
    def _():
        o_ref[...] = x_ref[...] + mod_ref[0][:, 5 * D_MODEL:6 * D_MODEL] * acc_ref[...]


def _peer(x, mod_l, lp, latent):
    n_tok = x.shape[0]
    nb = n_tok // TBP
    if latent:
        mod_spec = pl.BlockSpec((1, 1, 6 * D_MODEL), lambda i, *_: (1 + i // (DEC_SEQ // TBP), 0, 0))
    else:
        mod_spec = pl.BlockSpec((1, 1, 6 * D_MODEL), lambda i, *_: (0, 0, 0))
    n_keys2 = PEER_HEADS * PEER_QDIM
    ht, s, st = pl.pallas_call(
        _peer_score_kernel,
        grid=(nb,),
        in_specs=[pl.BlockSpec((TBP, D_MODEL), lambda i: (i, 0)), mod_spec, _const_spec((1, D_MODEL)),
                  _const_spec((n_keys2, D_MODEL)), _const_spec((2, PEER_KEYS, PEER_QDIM // 2))],
        out_specs=[pl.BlockSpec((D_MODEL, TBP), lambda i: (0, i)),
                   pl.BlockSpec((2, PEER_HEADS, PEER_KEYS, TBP), lambda i: (0, 0, 0, i)),
                   pl.BlockSpec((4 * PEER_HEADS, TBP), lambda i: (0, i))],
        out_shape=[jax.ShapeDtypeStruct((D_MODEL, n_tok), BF16),
                   jax.ShapeDtypeStruct((2, PEER_HEADS, PEER_KEYS, n_tok), F32),
                   jax.ShapeDtypeStruct((4 * PEER_HEADS, n_tok), F32)],
        compiler_params=_cparams("arbitrary"),
        name="peer_scores_lat" if latent else "peer_scores_ctx",
    )(x, mod_l, lp["norm2_g"], lp["peer_wqt"], lp["peer_subkeys"])
    n_et = PEER_KEYS * PEER_KEYS // PEER_ET
    return pl.pallas_call(
        _peer_dense_kernel,
        grid=(nb, n_et),
        in_specs=[pl.BlockSpec((D_MODEL, TBP), lambda i, e: (0, i)),
                  pl.BlockSpec((2, PEER_HEADS, PEER_KEYS, TBP), lambda i, e: (0, 0, 0, i)),
                  pl.BlockSpec((4 * PEER_HEADS, TBP), lambda i, e: (0, i)),
                  pl.BlockSpec((PEER_ET, D_MODEL), lambda i, e: (e, 0)),
                  pl.BlockSpec((PEER_ET, D_MODEL), lambda i, e: (e, 0)),
                  pl.BlockSpec((TBP, D_MODEL), lambda i, e: (i, 0)),
                  mod_spec],
        out_specs=pl.BlockSpec((TBP, D_MODEL), lambda i, e: (i, 0)),
        out_shape=jax.ShapeDtypeStruct((n_tok, D_MODEL), F32),
        scratch_shapes=[pltpu.VMEM((TBP, D_MODEL), F32),
                        pltpu.VMEM((PEER_HEADS, PEER_KEYS, TBP), F32),
                        pltpu.VMEM((PEER_HEADS, PEER_KEYS, TBP), F32)],
        compiler_params=_cparams("arbitrary", "arbitrary"),
        name="peer_dense_lat" if latent else "peer_dense_ctx",
    )(ht, s, st, lp["peer_u"], lp["peer_v"], x, mod_l)


def _layer_params(l, w):
    tile = lambda g, n: jnp.tile(g, n)
    gains = jnp.stack([
        tile(w["nat_qk_g"][l, 0], 4), tile(w["nat_qk_g"][l, 1], 4),
        tile(w["gqa_qk_g"][l, 0], 4), tile(w["gqa_qk_g"][l, 1], 4),
        tile(w["diff_qk_g"][l, 0], 8), tile(w["diff_qk_g"][l, 1], 8),
        jnp.zeros((256,), F32), jnp.zeros((256,), F32)])
    return dict(
        norm1_g=w["norm1_g"][l].reshape(1, D_MODEL),
        norm2_g=w["norm2_g"][l].reshape(1, D_MODEL),
        w_in=w["w_in"][l].astype(BF16),
        qk_gains=gains,
        bd64=_block_ones(256, 64), bd32=_block_ones(256, 32), bd256=_block_ones(256, 256),
        rope=w["rope"],
        sgu_norm_g=w["sgu_norm_g"][l].reshape(1, SGU_WIDTH),
        sgu_w=w["sgu_w"][l].astype(BF16),
        sgu_b=jnp.repeat(w["sgu_b"][l].T, SGU_WIDTH // SGU_GROUPS, axis=1),
        diff_lambda=w["diff_lambda"][l],
        diff_sub_g=tile(w["diff_sub_g"][l], 4).reshape(1, 256),
        nat_bias=_nat_bias_table(w["nat_rpb"][l]),
        w_branch=w["w_branch"][l].astype(BF16),
        w_gate=w["w_gate"][l].astype(BF16),
        b_gate=w["b_gate"][l].reshape(1, N_BRANCH * D_MODEL),
        w_out=w["w_out"][l].astype(BF16),
        peer_wqt=w["peer_wq"][l].T.astype(BF16),
        peer_subkeys=w["peer_subkeys"][l].astype(BF16),
        peer_u=w["peer_u"][l].astype(BF16),
        peer_v=w["peer_v"][l].astype(BF16),
    )


def kernel(x_prompt, x_sample, c, cache_nat_k, cache_nat_v, cache_gqa_k, cache_gqa_v, cache_diff_k, cache_diff_v, c_ctx, w_mod, b_mod, norm1_g, norm2_g, w_in, nat_qk_g, nat_rpb, gqa_qk_g, diff_qk_g, diff_lambda, diff_sub_g, sgu_norm_g, sgu_w, sgu_b, w_branch, w_gate, b_gate, w_out, peer_wq, peer_subkeys, peer_u, peer_v):
    w = dict(norm1_g=norm1_g, norm2_g=norm2_g, w_in=w_in, nat_qk_g=nat_qk_g, nat_rpb=nat_rpb,
             gqa_qk_g=gqa_qk_g, diff_qk_g=diff_qk_g, diff_lambda=diff_lambda, diff_sub_g=diff_sub_g,
             sgu_norm_g=sgu_norm_g, sgu_w=sgu_w, sgu_b=sgu_b, w_branch=w_branch, w_gate=w_gate,
             b_gate=b_gate, w_out=w_out, peer_wq=peer_wq, peer_subkeys=peer_subkeys, peer_u=peer_u,
             peer_v=peer_v, rope=_rope_tables())
    cvec = jnp.concatenate([c_ctx[None], c, jnp.zeros((MOD_ROWS - 1 - DEC_BATCH, D_MODEL), F32)], axis=0)
    mod = _modulation(cvec, w_mod, b_mod).reshape(DEPTH, MOD_ROWS, 1, 6 * D_MODEL)
    xp = x_prompt.reshape(N_CTX, D_MODEL)
    xs = x_sample.reshape(N_LAT, D_MODEL)
    new = []
    for l in range(DEPTH):
        lp = _layer_params(l, w)
        lam_init = 0.8 - 0.6 * math.exp(-0.3 * l)
        caches = (cache_nat_k, cache_nat_v, cache_gqa_k, cache_gqa_v, cache_diff_k, cache_diff_v)
        proj = _in_projection(xp, mod[l], lp, latent=False)
        new.append(proj)
        attn = _ctx_attention(proj, lp, lam_init)
        xp = _merge(xp, mod[l], lp, [(attn, 0), (attn, 1), (attn, 2), (proj, _COL["sgu"])], latent=False)
        xp = _peer(xp, mod[l], lp, latent=False)
        proj = _in_projection(xs, mod[l], lp, latent=True)
        o_nat, o_gqa, o_dif = _lat_attention(proj, caches, lp, l, lam_init)
        xs = _merge(xs, mod[l], lp, [(o_nat, 0), (o_gqa, 0), (o_dif, 0), (proj, _COL["sgu"])], latent=True)
        xs = _peer(xs, mod[l], lp, latent=True)

    def cache_out(lo, width, tail):
        a = jnp.stack([p[:, lo:lo + width].reshape(BATCH, SEQ, width) for p in new], axis=1)
        return a.reshape((BATCH, DEPTH, SEQ) + tail)

    return (xp.reshape(BATCH, SEQ, D_MODEL), xs.reshape(DEC_BATCH, DEC_SEQ, D_MODEL),
            cache_out(256, 256, (NAT_HEADS, HEAD_DIM)), cache_out(512, 256, (NAT_HEADS, HEAD_DIM)),
            cache_out(1024, 128, (GQA_KV_HEADS, HEAD_DIM)), cache_out(1152, 128, (GQA_KV_HEADS, HEAD_DIM)),
            cache_out(1536, 256, (DIFF_HEADS, 2, DIFF_DIM)), cache_out(1792, 256, (DIFF_HEADS, HEAD_DIM)))
```

```python
import functools
import math

import numpy as np
import jax
import jax.numpy as jnp
from jax import lax
from jax.experimental import pallas as pl
from jax.experimental.pallas import tpu as pltpu

F32 = jnp.float32
BF16 = jnp.bfloat16

D_MODEL = 1024
BATCH = 16
SEQ = 256
DEPTH = 2
DEC_BATCH = 2
DEC_SEQ = 4096
PAST_LEN = 512
GRID_W = 64
HEAD_DIM = 64
NAT_HEADS = 4
NAT_ROWS = 8
NAT_COLS = 16
GQA_HEADS = 4
GQA_KV_HEADS = 2
DIFF_HEADS = 4
DIFF_DIM = 32
SGU_WIDTH = 256
SGU_GROUPS = 4
CHUNK = 128
N_BRANCH = 4
BRANCH_W = 256
IN_W = 2560
PEER_HEADS = 8
PEER_KEYS = 128
PEER_QDIM = 256
PEER_TOPK = 16
ROPE_BASE = 10000.0
EPS = 1e-6

N_CTX = BATCH * SEQ
N_LAT = DEC_BATCH * DEC_SEQ
MOD_ROWS = 8
TB = 256
TQ = 256
PROJ_W = 2304
NAT_QROWS = 4
NAT_KROWS = NAT_QROWS + NAT_ROWS
TBP = 512
PEER_ET = 512
VMEM_LIMIT = 56 * 1024 * 1024
NEG = -1e30


def _cparams(*sem):
    return pltpu.CompilerParams(dimension_semantics=sem, vmem_limit_bytes=VMEM_LIMIT)


def _const_spec(shape):
    return pl.BlockSpec(shape, lambda *_: (0,) * len(shape))


def _gelu(x):
    return 0.5 * x * (1.0 + jnp.tanh(0.7978845608028654 * (x + 0.044715 * (x * x * x))))


def _sigmoid(x):
    return 1.0 / (1.0 + jnp.exp(-x))


def _dot(a, b):
    return jnp.dot(a.astype(BF16), b.astype(BF16), preferred_element_type=F32)


def _dot_nt(a, b):
    return lax.dot_general(a.astype(BF16), b.astype(BF16), (((1,), (1,)), ((), ())),
                           preferred_element_type=F32)


def _group_mean(y2, ones_bd):
    hi = y2.astype(BF16)
    lo = (y2 - hi.astype(F32)).astype(BF16)
    return (jnp.dot(hi, ones_bd, preferred_element_type=F32)
            + jnp.dot(lo, ones_bd, preferred_element_type=F32))


def _block_ones(width, group):
    idx = np.arange(width) // group
    return jnp.asarray((idx[:, None] == idx[None, :]).astype(np.float32) / group, dtype=BF16)


def _rms_mod(x, gain, scale, shift):
    xn = x * lax.rsqrt(jnp.mean(x * x, axis=-1, keepdims=True) + EPS) * gain
    return xn * (1.0 + scale) + shift


def _mod_spec(latent):
    if latent:
        return pl.BlockSpec((1, 1, 6 * D_MODEL), lambda i, *_: (1 + i // (DEC_SEQ // TB), 0, 0))
    return pl.BlockSpec((1, 1, 6 * D_MODEL), lambda i, *_: (0, 0, 0))


def _mod_kernel(c_ref, w_ref, b_ref, o_ref):
    c = c_ref[...]
    s = c * _sigmoid(c)
    o_ref[0] = _dot(s, w_ref[0]) + b_ref[0]


def _modulation(cvec, w_mod, b_mod):
    tn = 1536
    return pl.pallas_call(
        _mod_kernel,
        grid=(DEPTH, 6 * D_MODEL // tn),
        in_specs=[pl.BlockSpec((MOD_ROWS, D_MODEL), lambda l, j: (0, 0)),
                  pl.BlockSpec((1, D_MODEL, tn), lambda l, j: (l, 0, j)),
                  pl.BlockSpec((1, 1, tn), lambda l, j: (l, 0, j))],
        out_specs=pl.BlockSpec((1, MOD_ROWS, tn), lambda l, j: (l, 0, j)),
        out_shape=jax.ShapeDtypeStruct((DEPTH, MOD_ROWS, 6 * D_MODEL), F32),
        compiler_params=_cparams("arbitrary", "arbitrary"),
        name="modulation",
    )(cvec, w_mod, b_mod.reshape(DEPTH, 1, 6 * D_MODEL))


def _rope_tables():
    t = np.arange(DEC_SEQ)
    pos = (t // GRID_W, t % GRID_W)
    out = []
    for d in (HEAD_DIM, DIFF_DIM):
        half, quarter = d // 2, d // 4
        inv = ROPE_BASE ** (-np.arange(quarter, dtype=np.float32) * 2.0 / half)
        lane = np.arange(128) % d
        part, j = lane // half, lane % half
        ang = np.stack([pos[0][:, None] * inv[None, :], pos[1][:, None] * inv[None, :]], axis=1)
        a = ang[:, part, j % quarter].astype(np.float32)
        cos, sin = np.cos(a), np.sin(a)
        out += [jnp.asarray(cos, dtype=F32),
                jnp.asarray(np.where(j < quarter, -sin, 0.0), dtype=F32),
                jnp.asarray(np.where(j >= quarter, sin, 0.0), dtype=F32)]
    return out


def _rope(y, c, sa, sb, quarter):
    w = y.shape[-1]
    rep = w // 128
    if rep > 1:
        c, sa, sb = (jnp.concatenate([t] * rep, axis=-1) for t in (c, sa, sb))
    up = pltpu.roll(y, w - quarter, 1)
    dn = pltpu.roll(y, quarter, 1)
    return y * c + up * sa + dn * sb


def _inproj_kernel(*refs, latent):
    (x_ref, mod_ref, n1_ref, w_ref, g_ref, bd64_ref, bd32_ref, bd256_ref,
     sgn_ref, sgw_ref, sgb_ref) = refs[:11]
    o_ref = refs[-1]
    mod = mod_ref[0]
    h = _rms_mod(x_ref[...], n1_ref[...], mod[:, D_MODEL:2 * D_MODEL], mod[:, 0:D_MODEL])
    y = jnp.dot(h.astype(BF16), w_ref[...], preferred_element_type=F32)

    def qk_norm(lo, width, bd, gain_row):
        v = y[:, lo:lo + width]
        ms = _group_mean(v * v, bd)
        return v * lax.rsqrt(ms + EPS) * g_ref[gain_row:gain_row + 1, 0:width]

    bd64 = bd64_ref[...]
    bd32 = bd32_ref[...]
    if latent:
        c64, sa64, sb64, c32, sa32, sb32 = (r[...] for r in refs[11:17])
        rope64 = functools.partial(_rope, c=c64, sa=sa64, sb=sb64, quarter=HEAD_DIM // 4)
        rope32 = functools.partial(_rope, c=c32, sa=sa32, sb=sb32, quarter=DIFF_DIM // 4)
    else:
        rope64 = rope32 = lambda v: v
    o_ref[:, 0:256] = qk_norm(0, 256, bd64, 0)
    o_ref[:, 256:512] = qk_norm(256, 256, bd64, 1)
    o_ref[:, 512:768] = y[:, 512:768]
    o_ref[:, 768:1024] = rope64(qk_norm(768, 256, bd64, 2))
    o_ref[:, 1024:1152] = rope64(qk_norm(1024, 128, bd64[0:128, 0:128], 3))
    o_ref[:, 1152:1280] = y[:, 1152:1280]
    o_ref[:, 1280:1536] = rope32(qk_norm(1280, 256, bd32, 4))
    o_ref[:, 1536:1792] = rope32(qk_norm(1536, 256, bd32, 5))
    o_ref[:, 1792:2048] = y[:, 1792:2048]
    u = _gelu(y[:, 2048:2304])
    v = _gelu(y[:, 2304:2560])
    vn = v * lax.rsqrt(_group_mean(v * v, bd256_ref[...]) + EPS) * sgn_ref[...]
    vnb = vn.astype(BF16)
    lane_group = lax.broadcasted_iota(jnp.int32, (CHUNK, SGU_WIDTH), 1) // (SGU_WIDTH // SGU_GROUPS)
    for n in range(TB // CHUNK):
        vc = vnb[n * CHUNK:(n + 1) * CHUNK, :]
        s = sgb_ref[...]
        for g in range(SGU_GROUPS):
            sg = jnp.dot(sgw_ref[g], vc, preferred_element_type=F32)
            s = s + jnp.where(lane_group == g, sg, 0.0)
        o_ref[n * CHUNK:(n + 1) * CHUNK, 2048:2304] = u[n * CHUNK:(n + 1) * CHUNK, :] * s


def _in_projection(x, mod_l, lp, latent):
    n_tok = x.shape[0]
    in_specs = [pl.BlockSpec((TB, D_MODEL), lambda i: (i, 0)),
                _mod_spec(latent),
                _const_spec((1, D_MODEL)),
                _const_spec((D_MODEL, IN_W)),
                _const_spec((8, 256)),
                _const_spec((256, 256)), _const_spec((256, 256)), _const_spec((256, 256)),
                _const_spec((1, SGU_WIDTH)),
                _const_spec((SGU_GROUPS, CHUNK, CHUNK)),
                _const_spec((CHUNK, SGU_WIDTH))]
    args = [x, mod_l, lp["norm1_g"], lp["w_in"], lp["qk_gains"], lp["bd64"], lp["bd32"], lp["bd256"],
            lp["sgu_norm_g"], lp["sgu_w"], lp["sgu_b"]]
    if latent:
        in_specs += [pl.BlockSpec((TB, 128), lambda i: (i % (DEC_SEQ // TB), 0))] * 6
        args += lp["rope"]
    return pl.pallas_call(
        functools.partial(_inproj_kernel, latent=latent),
        grid=(n_tok // TB,),
        in_specs=in_specs,
        out_specs=pl.BlockSpec((TB, PROJ_W), lambda i: (i, 0)),
        out_shape=jax.ShapeDtypeStruct((n_tok, PROJ_W), F32),
        compiler_params=_cparams("arbitrary"),
        name="in_projection_lat" if latent else "in_projection_ctx",
    )(*args)


def _softmax_parts(scores):
    m = functools.reduce(jnp.maximum, [jnp.max(s, axis=-1, keepdims=True) for s in scores])
    ps = [jnp.exp(s - m) for s in scores]
    l = functools.reduce(jnp.add, [jnp.sum(p, axis=-1, keepdims=True) for p in ps])
    return ps, 1.0 / l


def _diff_lambda(lam_ref, lam_init):
    lv = lam_ref[...]
    a = jnp.sum(lv[0:1] * lv[1:2], axis=-1, keepdims=True)
    b = jnp.sum(lv[2:3] * lv[3:4], axis=-1, keepdims=True)
    return jnp.exp(a) - jnp.exp(b) + lam_init


def _head(ref_or_val, h, width=HEAD_DIM):
    return ref_or_val[:, h * width:(h + 1) * width]


def _mha(q, ks, vs, n_heads, kv_group, scale, biases=None):
    outs = []
    for h in range(n_heads):
        g = h // kv_group
        qh = _head(q, h) * scale
        scores = [_dot_nt(qh, _head(k, g)) for k in ks]
        if biases is not None:
            scores = [s if b is None else s + b[h] for s, b in zip(scores, biases)]
        ps, rl = _softmax_parts(scores)
        o = functools.reduce(jnp.add, [_dot(p, _head(v, g)) for p, v in zip(ps, vs)])
        outs.append(o * rl)
    return jnp.concatenate(outs, axis=-1)


def _diff_attn(q, ks, vs, lam, sub_gain, bd64, lam_init):
    scale = DIFF_DIM ** -0.5
    outs = []
    for h in range(DIFF_HEADS):
        pd = None
        for m in range(2):
            qm = _head(q, 2 * h + m, DIFF_DIM)
            scores = [_dot_nt(qm, _head(k, 2 * h + m, DIFF_DIM)) * scale for k in ks]
            ps, rl = _softmax_parts(scores)
            if m == 0:
                pd = [p * rl for p in ps]
            else:
                pd = [a - lam * (p * rl) for a, p in zip(pd, ps)]
        outs.append(functools.reduce(jnp.add, [_dot(p, _head(v, h)) for p, v in zip(pd, vs)]))
    o = jnp.concatenate(outs, axis=-1)
    ms = _group_mean(o * o, bd64)
    return o * lax.rsqrt(ms + EPS) * sub_gain * (1.0 - lam_init)


def _ctx_attn_kernel(nq, nk, nv, gq, gk, gv, dq, dk, dv, lam_ref, subg_ref, bd64_ref, o_ref, *, lam_init):
    scale = HEAD_DIM ** -0.5
    o_ref[:, 0:256] = _mha(nq[...], [nk[...]], [nv[...]], NAT_HEADS, 1, scale)
    o_ref[:, 256:512] = _mha(gq[...], [gk[...]], [gv[...]], GQA_HEADS, GQA_HEADS // GQA_KV_HEADS, scale)
    lam = _diff_lambda(lam_ref, lam_init)
    o_ref[:, 512:768] = _diff_attn(dq[...], [dk[...]], [dv[...]], lam, subg_ref[...], bd64_ref[...], lam_init)


_COL = dict(nq=0, nk=1, nv=2, gq=3, gk=8, gv=9, dq=5, dk=6, dv=7, sgu=8)


def _ctx_attention(proj, lp, lam_init):
    blk = lambda name, w: pl.BlockSpec((SEQ, w), lambda b: (b, _COL[name]))
    return pl.pallas_call(
        functools.partial(_ctx_attn_kernel, lam_init=lam_init),
        grid=(BATCH,),
        in_specs=[blk("nq", 256), blk("nk", 256), blk("nv", 256),
                  blk("gq", 256), blk("gk", 128), blk("gv", 128),
                  blk("dq", 256), blk("dk", 256), blk("dv", 256),
                  _const_spec((4, DIFF_DIM)), _const_spec((1, 256)), _const_spec((256, 256))],
        out_specs=pl.BlockSpec((SEQ, 768), lambda b: (b, 0)),
        out_shape=jax.ShapeDtypeStruct((N_CTX, 768), F32),
        compiler_params=_cparams("arbitrary"),
        name="ctx_attention",
    )(*([proj] * 9), lp["diff_lambda"], lp["diff_sub_g"], lp["bd64"])


def _gqa_lat_kernel(q_ref, k_ref, v_ref, ck_ref, cv_ref, o_ref):
    o_ref[...] = _mha(q_ref[...], [k_ref[...], ck_ref[...]], [v_ref[...], cv_ref[...]],
                      GQA_HEADS, GQA_HEADS // GQA_KV_HEADS, HEAD_DIM ** -0.5)


def _diff_lat_kernel(q_ref, k_ref, v_ref, ck_ref, cv_ref, lam_ref, subg_ref, bd64_ref, o_ref, *, lam_init):
    lam = _diff_lambda(lam_ref, lam_init)
    o_ref[...] = _diff_attn(q_ref[...], [k_ref[...], ck_ref[...]], [v_ref[...], cv_ref[...]],
                            lam, subg_ref[...], bd64_ref[...], lam_init)


def _nat_lat_kernel(q_ref, k_ref, v_ref, ck_ref, cv_ref, bias_ref, o_ref):
    i = pl.program_id(1)
    k_row0 = jnp.clip(NAT_QROWS * i - NAT_ROWS // 2, 0, GRID_W - NAT_KROWS)
    start = pl.multiple_of(k_row0 * GRID_W, GRID_W)
    kw = k_ref[pl.ds(start, NAT_KROWS * GRID_W), :]
    vw = v_ref[pl.ds(start, NAT_KROWS * GRID_W), :]
    o_ref[...] = _mha(q_ref[...], [kw, ck_ref[...]], [vw, cv_ref[...]], NAT_HEADS, 1, HEAD_DIM ** -0.5,
                      biases=[bias_ref[0], None])


def _nat_bias_table(rpb):
    rows = DEC_SEQ // GRID_W
    nblk = rows // NAT_QROWS
    pad = jnp.pad(rpb, ((0, 0), (0, 0), (GRID_W - NAT_COLS, GRID_W - NAT_COLS)))
    toep = jnp.stack([pad[:, :, GRID_W - 1 - c:2 * GRID_W - 1 - c] for c in range(GRID_W)], axis=2)
    col = np.arange(GRID_W)
    cs = np.clip(col - NAT_COLS // 2, 0, GRID_W - NAT_COLS)
    col_ok = (col[None, :] >= cs[:, None]) & (col[None, :] < cs[:, None] + NAT_COLS)
    toep = jnp.where(col_ok, toep, NEG)
    masked = jnp.full((NAT_HEADS, GRID_W, GRID_W), NEG, F32)
    cases = []
    for blk in (0, 1, nblk - 1):
        r0 = blk * NAT_QROWS
        k0 = int(np.clip(r0 - NAT_ROWS // 2, 0, rows - NAT_KROWS))
        q_rows = []
        for qr in range(r0, r0 + NAT_QROWS):
            rs = int(np.clip(qr - NAT_ROWS // 2, 0, rows - NAT_ROWS))
            q_rows.append(jnp.concatenate(
                [toep[:, kr - qr + NAT_ROWS - 1] if rs <= kr < rs + NAT_ROWS else masked
                 for kr in range(k0, k0 + NAT_KROWS)], axis=-1))
        cases.append(jnp.concatenate(q_rows, axis=-2))
    return jnp.stack(cases, axis=0)


def _lat_attention(proj, caches, lp, l, lam_init):
    cnk, cnv, cgk, cgv, cdk, cdv = caches
    nq_blocks = DEC_SEQ // TQ
    qspec = lambda name: pl.BlockSpec((TQ, 256), lambda b, i: (b * nq_blocks + i, _COL[name]))
    kvspec = lambda name, w: pl.BlockSpec((DEC_SEQ, w), lambda b, i: (b, _COL[name]))
    cspec = lambda w: pl.BlockSpec((None, None, PAST_LEN, w), lambda b, i: (b, l, 0, 0))
    ospec = pl.BlockSpec((TQ, 256), lambda b, i: (b * nq_blocks + i, 0))
    oshape = jax.ShapeDtypeStruct((N_LAT, 256), F32)
    o_gqa = pl.pallas_call(
        _gqa_lat_kernel,
        grid=(DEC_BATCH, nq_blocks),
        in_specs=[qspec("gq"), kvspec("gk", 128), kvspec("gv", 128), cspec(128), cspec(128)],
        out_specs=ospec, out_shape=oshape,
        compiler_params=_cparams("arbitrary", "arbitrary"),
        name="gqa_lat_attention",
    )(proj, proj, proj, cgk.reshape(DEC_BATCH, DEPTH, PAST_LEN, 128), cgv.reshape(DEC_BATCH, DEPTH, PAST_LEN, 128))
    o_dif = pl.pallas_call(
        functools.partial(_diff_lat_kernel, lam_init=lam_init),
        grid=(DEC_BATCH, nq_blocks),
        in_specs=[qspec("dq"), kvspec("dk", 256), kvspec("dv", 256), cspec(256), cspec(256),
                  _const_spec((4, DIFF_DIM)), _const_spec((1, 256)), _const_spec((256, 256))],
        out_specs=ospec, out_shape=oshape,
        compiler_params=_cparams("arbitrary", "arbitrary"),
        name="diff_lat_attention",
    )(proj, proj, proj, cdk.reshape(DEC_BATCH, DEPTH, PAST_LEN, 256), cdv.reshape(DEC_BATCH, DEPTH, PAST_LEN, 256),
      lp["diff_lambda"], lp["diff_sub_g"], lp["bd64"])
    nblk = DEC_SEQ // (NAT_QROWS * GRID_W)
    nat_q = NAT_QROWS * GRID_W
    o_nat = pl.pallas_call(
        _nat_lat_kernel,
        grid=(DEC_BATCH, nblk),
        in_specs=[pl.BlockSpec((nat_q, 256), lambda b, i: (b * nblk + i, _COL["nq"])),
                  kvspec("nk", 256), kvspec("nv", 256), cspec(256), cspec(256),
                  pl.BlockSpec((1, NAT_HEADS, nat_q, NAT_KROWS * GRID_W),
                               lambda b, i: (jnp.where(i == 0, 0, jnp.where(i == nblk - 1, 2, 1)), 0, 0, 0))],
        out_specs=pl.BlockSpec((nat_q, 256), lambda b, i: (b * nblk + i, 0)),
        out_shape=oshape,
        compiler_params=_cparams("arbitrary", "arbitrary"),
        name="nat_lat_attention",
    )(proj, proj, proj, cnk.reshape(DEC_BATCH, DEPTH, PAST_LEN, 256), cnv.reshape(DEC_BATCH, DEPTH, PAST_LEN, 256),
      lp["nat_bias"])
    return o_nat, o_gqa, o_dif


def _merge_kernel(x_ref, mod_ref, n1_ref, b0_ref, b1_ref, b2_ref, b3_ref, wb_ref, wg_ref, bg_ref, wo_ref, o_ref):
    x = x_ref[...]
    mod = mod_ref[0]
    hb = _rms_mod(x, n1_ref[...], mod[:, D_MODEL:2 * D_MODEL], mod[:, 0:D_MODEL]).astype(BF16)
    merged = None
    for n, b_ref in enumerate((b0_ref, b1_ref, b2_ref, b3_ref)):
        cols = slice(n * D_MODEL, (n + 1) * D_MODEL)
        gate = _sigmoid(jnp.dot(hb, wg_ref[:, cols], preferred_element_type=F32) + bg_ref[:, cols])
        term = gate * _dot(b_ref[...], wb_ref[n])
        merged = term if merged is None else merged + term
    out = _dot(merged, wo_ref[...])
    o_ref[...] = x + mod[:, 2 * D_MODEL:3 * D_MODEL] * out


def _merge(x, mod_l, lp, branches, latent):
    n_tok = x.shape[0]
    return pl.pallas_call(
        _merge_kernel,
        grid=(n_tok // TB,),
        in_specs=[pl.BlockSpec((TB, D_MODEL), lambda i: (i, 0)), _mod_spec(latent), _const_spec((1, D_MODEL))]
                 + [pl.BlockSpec((TB, BRANCH_W), functools.partial(lambda i, c: (i, c), c=col)) for _, col in branches]
                 + [_const_spec((N_BRANCH, BRANCH_W, D_MODEL)), _const_spec((D_MODEL, N_BRANCH * D_MODEL)),
                    _const_spec((1, N_BRANCH * D_MODEL)), _const_spec((D_MODEL, D_MODEL))],
        out_specs=pl.BlockSpec((TB, D_MODEL), lambda i: (i, 0)),
        out_shape=jax.ShapeDtypeStruct((n_tok, D_MODEL), F32),
        compiler_params=_cparams("arbitrary"),
        name="merge_lat" if latent else "merge_ctx",
    )(x, mod_l, lp["norm1_g"], *[a for a, _ in branches], lp["w_branch"], lp["w_gate"], lp["b_gate"], lp["w_out"])


def _top_desc(s, k):
    rows = []
    cur = s
    for _ in range(k):
        m = jnp.max(cur, axis=0, keepdims=True)
        rows.append(m)
        cur = jnp.where(cur == m, NEG, cur)
    return jnp.concatenate(rows, axis=0)


def _peer_score_kernel(x_ref, mod_ref, n2_ref, wqt_ref, sk_ref, ht_ref, s_ref, st_ref):
    mod = mod_ref[0]
    h2 = _rms_mod(x_ref[...], n2_ref[...], mod[:, 4 * D_MODEL:5 * D_MODEL], mod[:, 3 * D_MODEL:4 * D_MODEL])
    htb = h2.T.astype(BF16)
    ht_ref[...] = htb
    qt = jnp.dot(wqt_ref[...], htb, preferred_element_type=F32)
    k1 = PEER_TOPK + 1
    half = PEER_TOPK // 2
    row = lax.broadcasted_iota(jnp.int32, (k1, 1), 0)
    for h in range(PEER_HEADS):
        tops = []
        for p in range(2):
            r0 = (2 * h + p) * (PEER_QDIM // 2)
            s = jnp.dot(sk_ref[p], qt[r0:r0 + PEER_QDIM // 2, :].astype(BF16), preferred_element_type=F32)
            s_ref[p, h] = s
            tops.append(_top_desc(s, k1))
        t1, t2 = tops
        cands = [t1[0:1] + t2, jnp.where(row >= 1, t1 + t2[0:1], NEG)]
        for a in range(1, half):
            nb = k1 // (a + 1)
            cands.append(jnp.where((row[0:half] >= 1) & (row[0:half] < nb), t1[a:a + 1] + t2[0:half], NEG))
        best = _top_desc(jnp.concatenate(cands, axis=0), k1)
        z = jnp.sum(jnp.exp(best[0:PEER_TOPK] - best[0:1]), axis=0, keepdims=True)
        thr = 0.5 * (best[PEER_TOPK - 1:PEER_TOPK] + best[PEER_TOPK:k1])
        st_ref[4 * h:4 * h + 4, :] = jnp.concatenate([thr, t1[0:1], t2[0:1], 1.0 / z], axis=0)


def _peer_dense_kernel(ht_ref, s_ref, st_ref, u_ref, v_ref, x_ref, mod_ref, o_ref,
                       acc_ref, d_ref, e1_ref, e2_ref, a_ref, w_ref):
    e = pl.program_id(1)
    n_tok = acc_ref.shape[0]

    @pl.when(e == 0)
    def _():
        acc_ref[...] = jnp.zeros_like(acc_ref)
        for h in range(PEER_HEADS):
            st = st_ref[4 * h:4 * h + 4, :]
            s1 = s_ref[0, h]
            d_ref[h] = st[0:1] - s1
            e1_ref[h] = jnp.exp(s1 - st[1:2]) * (0.5 * st[3:4])
            e2_ref[h] = jnp.exp(s_ref[1, h] - st[2:3])

    a_ref[...] = jnp.dot(u_ref[...], ht_ref[...], preferred_element_type=F32)
    for rr in range(PEER_ET // PEER_KEYS):
        r = e * (PEER_ET // PEER_KEYS) + rr
        rows = slice(rr * PEER_KEYS, (rr + 1) * PEER_KEYS)
        d_rows = [d_ref[h, pl.ds(r, 1), :] for h in range(PEER_HEADS)]
        e1_rows = [e1_ref[h, pl.ds(r, 1), :] for h in range(PEER_HEADS)]
        for lt in range(n_tok // 128):
            lanes = slice(lt * 128, (lt + 1) * 128)
            g = None
            for h in range(PEER_HEADS):
                sel = s_ref[1, h, :, lanes] >= d_rows[h][:, lanes]
                term = jnp.where(sel, e2_ref[h, :, lanes] * e1_rows[h][:, lanes], 0.0)
                g = term if g is None else g + term
            a = a_ref[rows, lanes]
            t = jnp.tanh(a * (0.7978845608028654 + (0.7978845608028654 * 0.044715) * (a * a)))
            w_ref[rows, lanes] = g * (a + a * t)
    acc_ref[...] += jnp.dot(w_ref[...].T.astype(BF16), v_ref[...], preferred_element_type=F32)

    @pl.when(e == pl.num_programs(1) - 1)
    def _():
        o_ref[...] = x_ref[...] + mod_ref[0][:, 5 * D_MODEL:6 * D_MODEL] * acc_ref[...]


def _peer(x, mod_l, lp, latent):
    n_tok = x.shape[0]
    nb = n_tok // TBP
    if latent:
        mod_spec = pl.BlockSpec((1, 1, 6 * D_MODEL), lambda i, *_: (1 + i // (DEC_SEQ // TBP), 0, 0))
    else:
        mod_spec = pl.BlockSpec((1, 1, 6 * D_MODEL), lambda i, *_: (0, 0, 0))
    n_keys2 = PEER_HEADS * PEER_QDIM
    ht, s, st = pl.pallas_call(
        _peer_score_kernel,
        grid=(nb,),
        in_specs=[pl.BlockSpec((TBP, D_MODEL), lambda i: (i, 0)), mod_spec, _const_spec((1, D_MODEL)),
                  _const_spec((n_keys2, D_MODEL)), _const_spec((2, PEER_KEYS, PEER_QDIM // 2))],
        out_specs=[pl.BlockSpec((D_MODEL, TBP), lambda i: (0, i)),
                   pl.BlockSpec((2, PEER_HEADS, PEER_KEYS, TBP), lambda i: (0, 0, 0, i)),
                   pl.BlockSpec((4 * PEER_HEADS, TBP), lambda i: (0, i))],
        out_shape=[jax.ShapeDtypeStruct((D_MODEL, n_tok), BF16),
                   jax.ShapeDtypeStruct((2, PEER_HEADS, PEER_KEYS, n_tok), F32),
                   jax.ShapeDtypeStruct((4 * PEER_HEADS, n_tok), F32)],
        compiler_params=_cparams("arbitrary"),
        name="peer_scores_lat" if latent else "peer_scores_ctx",
    )(x, mod_l, lp["norm2_g"], lp["peer_wqt"], lp["peer_subkeys"])
    n_et = PEER_KEYS * PEER_KEYS // PEER_ET
    return pl.pallas_call(
        _peer_dense_kernel,
        grid=(nb, n_et),
        in_specs=[pl.BlockSpec((D_MODEL, TBP), lambda i, e: (0, i)),
                  pl.BlockSpec((2, PEER_HEADS, PEER_KEYS, TBP), lambda i, e: (0, 0, 0, i)),
                  pl.BlockSpec((4 * PEER_HEADS, TBP), lambda i, e: (0, i)),
                  pl.BlockSpec((PEER_ET, D_MODEL), lambda i, e: (e, 0)),
                  pl.BlockSpec((PEER_ET, D_MODEL), lambda i, e: (e, 0)),
                  pl.BlockSpec((TBP, D_MODEL), lambda i, e: (i, 0)),
                  mod_spec],
        out_specs=pl.BlockSpec((TBP, D_MODEL), lambda i, e: (i, 0)),
        out_shape=jax.ShapeDtypeStruct((n_tok, D_MODEL), F32),
        scratch_shapes=[pltpu.VMEM((TBP, D_MODEL), F32)]
                       + [pltpu.VMEM((PEER_HEADS, PEER_KEYS, TBP), F32)] * 3
                       + [pltpu.VMEM((PEER_ET, TBP), F32)] * 2,
        compiler_params=_cparams("arbitrary", "arbitrary"),
        name="peer_dense_lat" if latent else "peer_dense_ctx",
    )(ht, s, st, lp["peer_u"], lp["peer_v"], x, mod_l)


def _layer_params(l, w):
    tile = lambda g, n: jnp.tile(g, n)
    gains = jnp.stack([
        tile(w["nat_qk_g"][l, 0], 4), tile(w["nat_qk_g"][l, 1], 4),
        tile(w["gqa_qk_g"][l, 0], 4), tile(w["gqa_qk_g"][l, 1], 4),
        tile(w["diff_qk_g"][l, 0], 8), tile(w["diff_qk_g"][l, 1], 8),
        jnp.zeros((256,), F32), jnp.zeros((256,), F32)])
    return dict(
        norm1_g=w["norm1_g"][l].reshape(1, D_MODEL),
        norm2_g=w["norm2_g"][l].reshape(1, D_MODEL),
        w_in=w["w_in"][l].astype(BF16),
        qk_gains=gains,
        bd64=_block_ones(256, 64), bd32=_block_ones(256, 32), bd256=_block_ones(256, 256),
        rope=w["rope"],
        sgu_norm_g=w["sgu_norm_g"][l].reshape(1, SGU_WIDTH),
        sgu_w=w["sgu_w"][l].astype(BF16),
        sgu_b=jnp.repeat(w["sgu_b"][l].T, SGU_WIDTH // SGU_GROUPS, axis=1),
        diff_lambda=w["diff_lambda"][l],
        diff_sub_g=tile(w["diff_sub_g"][l], 4).reshape(1, 256),
        nat_bias=_nat_bias_table(w["nat_rpb"][l]),
        w_branch=w["w_branch"][l].astype(BF16),
        w_gate=w["w_gate"][l].astype(BF16),
        b_gate=w["b_gate"][l].reshape(1, N_BRANCH * D_MODEL),
        w_out=w["w_out"][l].astype(BF16),
        peer_wqt=w["peer_wq"][l].T.astype(BF16),
        peer_subkeys=w["peer_subkeys"][l].astype(BF16),
        peer_u=w["peer_u"][l].astype(BF16),
        peer_v=w["peer_v"][l].astype(BF16),
    )


def kernel(x_prompt, x_sample, c, cache_nat_k, cache_nat_v, cache_gqa_k, cache_gqa_v, cache_diff_k, cache_diff_v, c_ctx, w_mod, b_mod, norm1_g, norm2_g, w_in, nat_qk_g, nat_rpb, gqa_qk_g, diff_qk_g, diff_lambda, diff_sub_g, sgu_norm_g, sgu_w, sgu_b, w_branch, w_gate, b_gate, w_out, peer_wq, peer_subkeys, peer_u, peer_v):
    w = dict(norm1_g=norm1_g, norm2_g=norm2_g, w_in=w_in, nat_qk_g=nat_qk_g, nat_rpb=nat_rpb,
             gqa_qk_g=gqa_qk_g, diff_qk_g=diff_qk_g, diff_lambda=diff_lambda, diff_sub_g=diff_sub_g,
             sgu_norm_g=sgu_norm_g, sgu_w=sgu_w, sgu_b=sgu_b, w_branch=w_branch, w_gate=w_gate,
             b_gate=b_gate, w_out=w_out, peer_wq=peer_wq, peer_subkeys=peer_subkeys, peer_u=peer_u,
             peer_v=peer_v, rope=_rope_tables())
    cvec = jnp.concatenate([c_ctx[None], c, jnp.zeros((MOD_ROWS - 1 - DEC_BATCH, D_MODEL), F32)], axis=0)
    mod = _modulation(cvec, w_mod, b_mod).reshape(DEPTH, MOD_ROWS, 1, 6 * D_MODEL)
    xp = x_prompt.reshape(N_CTX, D_MODEL)
    xs = x_sample.reshape(N_LAT, D_MODEL)
    new = []
    for l in range(DEPTH):
        lp = _layer_params(l, w)
        lam_init = 0.8 - 0.6 * math.exp(-0.3 * l)
        caches = (cache_nat_k, cache_nat_v, cache_gqa_k, cache_gqa_v, cache_diff_k, cache_diff_v)
        proj = _in_projection(xp, mod[l], lp, latent=False)
        new.append(proj)
        attn = _ctx_attention(proj, lp, lam_init)
        xp = _merge(xp, mod[l], lp, [(attn, 0), (attn, 1), (attn, 2), (proj, _COL["sgu"])], latent=False)
        xp = _peer(xp, mod[l], lp, latent=False)
        proj = _in_projection(xs, mod[l], lp, latent=True)
        o_nat, o_gqa, o_dif = _lat_attention(proj, caches, lp, l, lam_init)
        xs = _merge(xs, mod[l], lp, [(o_nat, 0), (o_gqa, 0), (o_dif, 0), (proj, _COL["sgu"])], latent=True)
        xs = _peer(xs, mod[l], lp, latent=True)

    def cache_out(lo, width, tail):
        a = jnp.stack([p[:, lo:lo + width].reshape(BATCH, SEQ, width) for p in new], axis=1)
        return a.reshape((BATCH, DEPTH, SEQ) + tail)

    return (xp.reshape(BATCH, SEQ, D_MODEL), xs.reshape(DEC_BATCH, DEC_SEQ, D_MODEL),
            cache_out(256, 256, (NAT_HEADS, HEAD_DIM)), cache_out(512, 256, (NAT_HEADS, HEAD_DIM)),
            cache_out(1024, 128, (GQA_KV_HEADS, HEAD_DIM)), cache_out(1152, 128, (GQA_KV_HEADS, HEAD_DIM)),
            cache_out(1536, 256, (DIFF_HEADS, 2, DIFF_DIM)), cache_out(1792, 256, (DIFF_HEADS, HEAD_DIM)))
```

```python
import functools
import math

import numpy as np
import jax
import jax.numpy as jnp
from jax import lax
from jax.experimental import pallas as pl
from jax.experimental.pallas import tpu as pltpu

F32 = jnp.float32
BF16 = jnp.bfloat16

D_MODEL = 1024
BATCH = 16
SEQ = 256
DEPTH = 2
DEC_BATCH = 2
DEC_SEQ = 4096
PAST_LEN = 512
GRID_W = 64
HEAD_DIM = 64
NAT_HEADS = 4
NAT_ROWS = 8
NAT_COLS = 16
GQA_HEADS = 4
GQA_KV_HEADS = 2
DIFF_HEADS = 4
DIFF_DIM = 32
SGU_WIDTH = 256
SGU_GROUPS = 4
CHUNK = 128
N_BRANCH = 4
BRANCH_W = 256
IN_W = 2560
PEER_HEADS = 8
PEER_KEYS = 128
PEER_QDIM = 256
PEER_TOPK = 16
ROPE_BASE = 10000.0
EPS = 1e-6

N_CTX = BATCH * SEQ
N_LAT = DEC_BATCH * DEC_SEQ
MOD_ROWS = 8
TB = 256
TQ = 256
PROJ_W = 2304
NAT_QROWS = 4
NAT_KROWS = NAT_QROWS + NAT_ROWS
TBP = 512
PEER_ET = 512
VMEM_LIMIT = 56 * 1024 * 1024
NEG = -1e30


def _cparams(*sem):
    return pltpu.CompilerParams(dimension_semantics=sem, vmem_limit_bytes=VMEM_LIMIT)


def _const_spec(shape):
    return pl.BlockSpec(shape, lambda *_: (0,) * len(shape))


def _gelu(x):
    return 0.5 * x * (1.0 + jnp.tanh(0.7978845608028654 * (x + 0.044715 * (x * x * x))))


def _sigmoid(x):
    return 1.0 / (1.0 + jnp.exp(-x))


def _dot(a, b):
    return jnp.dot(a.astype(BF16), b.astype(BF16), preferred_element_type=F32)


def _dot_nt(a, b):
    return lax.dot_general(a.astype(BF16), b.astype(BF16), (((1,), (1,)), ((), ())),
                           preferred_element_type=F32)


def _group_mean(y2, ones_bd):
    hi = y2.astype(BF16)
    lo = (y2 - hi.astype(F32)).astype(BF16)
    return (jnp.dot(hi, ones_bd, preferred_element_type=F32)
            + jnp.dot(lo, ones_bd, preferred_element_type=F32))


def _block_ones(width, group):
    idx = np.arange(width) // group
    return jnp.asarray((idx[:, None] == idx[None, :]).astype(np.float32) / group, dtype=BF16)


def _rms_mod(x, gain, scale, shift):
    xn = x * lax.rsqrt(jnp.mean(x * x, axis=-1, keepdims=True) + EPS) * gain
    return xn * (1.0 + scale) + shift


def _mod_spec(latent):
    if latent:
        return pl.BlockSpec((1, 1, 6 * D_MODEL), lambda i, *_: (1 + i // (DEC_SEQ // TB), 0, 0))
    return pl.BlockSpec((1, 1, 6 * D_MODEL), lambda i, *_: (0, 0, 0))


def _mod_kernel(c_ref, w_ref, b_ref, o_ref):
    c = c_ref[...]
    s = c * _sigmoid(c)
    o_ref[0] = _dot(s, w_ref[0]) + b_ref[0]


def _modulation(cvec, w_mod, b_mod):
    tn = 1536
    return pl.pallas_call(
        _mod_kernel,
        grid=(DEPTH, 6 * D_MODEL // tn),
        in_specs=[pl.BlockSpec((MOD_ROWS, D_MODEL), lambda l, j: (0, 0)),
                  pl.BlockSpec((1, D_MODEL, tn), lambda l, j: (l, 0, j)),
                  pl.BlockSpec((1, 1, tn), lambda l, j: (l, 0, j))],
        out_specs=pl.BlockSpec((1, MOD_ROWS, tn), lambda l, j: (l, 0, j)),
        out_shape=jax.ShapeDtypeStruct((DEPTH, MOD_ROWS, 6 * D_MODEL), F32),
        compiler_params=_cparams("arbitrary", "arbitrary"),
        name="modulation",
    )(cvec, w_mod, b_mod.reshape(DEPTH, 1, 6 * D_MODEL))


def _rope_tables():
    t = np.arange(DEC_SEQ)
    pos = (t // GRID_W, t % GRID_W)
    out = []
    for d in (HEAD_DIM, DIFF_DIM):
        half, quarter = d // 2, d // 4
        inv = ROPE_BASE ** (-np.arange(quarter, dtype=np.float32) * 2.0 / half)
        lane = np.arange(128) % d
        part, j = lane // half, lane % half
        ang = np.stack([pos[0][:, None] * inv[None, :], pos[1][:, None] * inv[None, :]], axis=1)
        a = ang[:, part, j % quarter].astype(np.float32)
        cos, sin = np.cos(a), np.sin(a)
        out += [jnp.asarray(cos, dtype=F32),
                jnp.asarray(np.where(j < quarter, -sin, 0.0), dtype=F32),
                jnp.asarray(np.where(j >= quarter, sin, 0.0), dtype=F32)]
    return out


def _rope(y, c, sa, sb, quarter):
    w = y.shape[-1]
    rep = w // 128
    if rep > 1:
        c, sa, sb = (jnp.concatenate([t] * rep, axis=-1) for t in (c, sa, sb))
    up = pltpu.roll(y, w - quarter, 1)
    dn = pltpu.roll(y, quarter, 1)
    return y * c + up * sa + dn * sb


def _inproj_kernel(*refs, latent):
    (x_ref, mod_ref, n1_ref, w_ref, g_ref, bd64_ref, bd32_ref, bd256_ref,
     sgn_ref, sgw_ref, sgb_ref) = refs[:11]
    o_ref = refs[-1]
    mod = mod_ref[0]
    h = _rms_mod(x_ref[...], n1_ref[...], mod[:, D_MODEL:2 * D_MODEL], mod[:, 0:D_MODEL])
    y = jnp.dot(h.astype(BF16), w_ref[...], preferred_element_type=F32)

    def qk_norm(lo, width, bd, gain_row):
        v = y[:, lo:lo + width]
        ms = _group_mean(v * v, bd)
        return v * lax.rsqrt(ms + EPS) * g_ref[gain_row:gain_row + 1, 0:width]

    bd64 = bd64_ref[...]
    bd32 = bd32_ref[...]
    if latent:
        c64, sa64, sb64, c32, sa32, sb32 = (r[...] for r in refs[11:17])
        rope64 = functools.partial(_rope, c=c64, sa=sa64, sb=sb64, quarter=HEAD_DIM // 4)
        rope32 = functools.partial(_rope, c=c32, sa=sa32, sb=sb32, quarter=DIFF_DIM // 4)
    else:
        rope64 = rope32 = lambda v: v
    o_ref[:, 0:256] = qk_norm(0, 256, bd64, 0)
    o_ref[:, 256:512] = qk_norm(256, 256, bd64, 1)
    o_ref[:, 512:768] = y[:, 512:768]
    o_ref[:, 768:1024] = rope64(qk_norm(768, 256, bd64, 2))
    o_ref[:, 1024:1152] = rope64(qk_norm(1024, 128, bd64[0:128, 0:128], 3))
    o_ref[:, 1152:1280] = y[:, 1152:1280]
    o_ref[:, 1280:1536] = rope32(qk_norm(1280, 256, bd32, 4))
    o_ref[:, 1536:1792] = rope32(qk_norm(1536, 256, bd32, 5))
    o_ref[:, 1792:2048] = y[:, 1792:2048]
    u = _gelu(y[:, 2048:2304])
    v = _gelu(y[:, 2304:2560])
    vn = v * lax.rsqrt(_group_mean(v * v, bd256_ref[...]) + EPS) * sgn_ref[...]
    vnb = vn.astype(BF16)
    lane_group = lax.broadcasted_iota(jnp.int32, (CHUNK, SGU_WIDTH), 1) // (SGU_WIDTH // SGU_GROUPS)
    for n in range(TB // CHUNK):
        vc = vnb[n * CHUNK:(n + 1) * CHUNK, :]
        s = sgb_ref[...]
        for g in range(SGU_GROUPS):
            sg = jnp.dot(sgw_ref[g], vc, preferred_element_type=F32)
            s = s + jnp.where(lane_group == g, sg, 0.0)
        o_ref[n * CHUNK:(n + 1) * CHUNK, 2048:2304] = u[n * CHUNK:(n + 1) * CHUNK, :] * s


def _in_projection(x, mod_l, lp, latent):
    n_tok = x.shape[0]
    in_specs = [pl.BlockSpec((TB, D_MODEL), lambda i: (i, 0)),
                _mod_spec(latent),
                _const_spec((1, D_MODEL)),
                _const_spec((D_MODEL, IN_W)),
                _const_spec((8, 256)),
                _const_spec((256, 256)), _const_spec((256, 256)), _const_spec((256, 256)),
                _const_spec((1, SGU_WIDTH)),
                _const_spec((SGU_GROUPS, CHUNK, CHUNK)),
                _const_spec((CHUNK, SGU_WIDTH))]
    args = [x, mod_l, lp["norm1_g"], lp["w_in"], lp["qk_gains"], lp["bd64"], lp["bd32"], lp["bd256"],
            lp["sgu_norm_g"], lp["sgu_w"], lp["sgu_b"]]
    if latent:
        in_specs += [pl.BlockSpec((TB, 128), lambda i: (i % (DEC_SEQ // TB), 0))] * 6
        args += lp["rope"]
    return pl.pallas_call(
        functools.partial(_inproj_kernel, latent=latent),
        grid=(n_tok // TB,),
        in_specs=in_specs,
        out_specs=pl.BlockSpec((TB, PROJ_W), lambda i: (i, 0)),
        out_shape=jax.ShapeDtypeStruct((n_tok, PROJ_W), F32),
        compiler_params=_cparams("arbitrary"),
        name="in_projection_lat" if latent else "in_projection_ctx",
    )(*args)


def _softmax_parts(scores):
    m = functools.reduce(jnp.maximum, [jnp.max(s, axis=-1, keepdims=True) for s in scores])
    ps = [jnp.exp(s - m) for s in scores]
    l = functools.reduce(jnp.add, [jnp.sum(p, axis=-1, keepdims=True) for p in ps])
    return ps, 1.0 / l


def _diff_lambda(lam_ref, lam_init):
    lv = lam_ref[...]
    a = jnp.sum(lv[0:1] * lv[1:2], axis=-1, keepdims=True)
    b = jnp.sum(lv[2:3] * lv[3:4], axis=-1, keepdims=True)
    return jnp.exp(a) - jnp.exp(b) + lam_init


def _head(ref_or_val, h, width=HEAD_DIM):
    return ref_or_val[:, h * width:(h + 1) * width]


def _mha(q, ks, vs, n_heads, kv_group, scale, biases=None):
    outs = []
    for h in range(n_heads):
        g = h // kv_group
        qh = _head(q, h) * scale
        scores = [_dot_nt(qh, _head(k, g)) for k in ks]
        if biases is not None:
            scores = [s if b is None else s + b[h] for s, b in zip(scores, biases)]
        ps, rl = _softmax_parts(scores)
        o = functools.reduce(jnp.add, [_dot(p, _head(v, g)) for p, v in zip(ps, vs)])
        outs.append(o * rl)
    return jnp.concatenate(outs, axis=-1)


def _diff_attn(q, ks, vs, lam, sub_gain, bd64, lam_init):
    scale = DIFF_DIM ** -0.5
    outs = []
    for h in range(DIFF_HEADS):
        pd = None
        for m in range(2):
            qm = _head(q, 2 * h + m, DIFF_DIM)
            scores = [_dot_nt(qm, _head(k, 2 * h + m, DIFF_DIM)) * scale for k in ks]
            ps, rl = _softmax_parts(scores)
            if m == 0:
                pd = [p * rl for p in ps]
            else:
                pd = [a - lam * (p * rl) for a, p in zip(pd, ps)]
        outs.append(functools.reduce(jnp.add, [_dot(p, _head(v, h)) for p, v in zip(pd, vs)]))
    o = jnp.concatenate(outs, axis=-1)
    ms = _group_mean(o * o, bd64)
    return o * lax.rsqrt(ms + EPS) * sub_gain * (1.0 - lam_init)


def _ctx_attn_kernel(nq, nk, nv, gq, gk, gv, dq, dk, dv, lam_ref, subg_ref, bd64_ref, o_ref, *, lam_init):
    scale = HEAD_DIM ** -0.5
    o_ref[:, 0:256] = _mha(nq[...], [nk[...]], [nv[...]], NAT_HEADS, 1, scale)
    o_ref[:, 256:512] = _mha(gq[...], [gk[...]], [gv[...]], GQA_HEADS, GQA_HEADS // GQA_KV_HEADS, scale)
    lam = _diff_lambda(lam_ref, lam_init)
    o_ref[:, 512:768] = _diff_attn(dq[...], [dk[...]], [dv[...]], lam, subg_ref[...], bd64_ref[...], lam_init)


_COL = dict(nq=0, nk=1, nv=2, gq=3, gk=8, gv=9, dq=5, dk=6, dv=7, sgu=8)


def _ctx_attention(proj, lp, lam_init):
    blk = lambda name, w: pl.BlockSpec((SEQ, w), lambda b: (b, _COL[name]))
    return pl.pallas_call(
        functools.partial(_ctx_attn_kernel, lam_init=lam_init),
        grid=(BATCH,),
        in_specs=[blk("nq", 256), blk("nk", 256), blk("nv", 256),
                  blk("gq", 256), blk("gk", 128), blk("gv", 128),
                  blk("dq", 256), blk("dk", 256), blk("dv", 256),
                  _const_spec((4, DIFF_DIM)), _const_spec((1, 256)), _const_spec((256, 256))],
        out_specs=pl.BlockSpec((SEQ, 768), lambda b: (b, 0)),
        out_shape=jax.ShapeDtypeStruct((N_CTX, 768), F32),
        compiler_params=_cparams("arbitrary"),
        name="ctx_attention",
    )(*([proj] * 9), lp["diff_lambda"], lp["diff_sub_g"], lp["bd64"])


def _gqa_lat_kernel(q_ref, k_ref, v_ref, ck_ref, cv_ref, o_ref):
    o_ref[...] = _mha(q_ref[...], [k_ref[...], ck_ref[...]], [v_ref[...], cv_ref[...]],
                      GQA_HEADS, GQA_HEADS // GQA_KV_HEADS, HEAD_DIM ** -0.5)


def _diff_lat_kernel(q_ref, k_ref, v_ref, ck_ref, cv_ref, lam_ref, subg_ref, bd64_ref, o_ref, *, lam_init):
    lam = _diff_lambda(lam_ref, lam_init)
    o_ref[...] = _diff_attn(q_ref[...], [k_ref[...], ck_ref[...]], [v_ref[...], cv_ref[...]],
                            lam, subg_ref[...], bd64_ref[...], lam_init)


def _nat_lat_kernel(q_ref, k_ref, v_ref, ck_ref, cv_ref, bias_ref, o_ref):
    i = pl.program_id(1)
    k_row0 = jnp.clip(NAT_QROWS * i - NAT_ROWS // 2, 0, GRID_W - NAT_KROWS)
    start = pl.multiple_of(k_row0 * GRID_W, GRID_W)
    kw = k_ref[pl.ds(start, NAT_KROWS * GRID_W), :]
    vw = v_ref[pl.ds(start, NAT_KROWS * GRID_W), :]
    o_ref[...] = _mha(q_ref[...], [kw, ck_ref[...]], [vw, cv_ref[...]], NAT_HEADS, 1, HEAD_DIM ** -0.5,
                      biases=[bias_ref[0], None])


def _nat_bias_table(rpb):
    rows = DEC_SEQ // GRID_W
    nblk = rows // NAT_QROWS
    pad = jnp.pad(rpb, ((0, 0), (0, 0), (GRID_W - NAT_COLS, GRID_W - NAT_COLS)))
    toep = jnp.stack([pad[:, :, GRID_W - 1 - c:2 * GRID_W - 1 - c] for c in range(GRID_W)], axis=2)
    col = np.arange(GRID_W)
    cs = np.clip(col - NAT_COLS // 2, 0, GRID_W - NAT_COLS)
    col_ok = (col[None, :] >= cs[:, None]) & (col[None, :] < cs[:, None] + NAT_COLS)
    toep = jnp.where(col_ok, toep, NEG)
    masked = jnp.full((NAT_HEADS, GRID_W, GRID_W), NEG, F32)
    cases = []
    for blk in (0, 1, nblk - 1):
        r0 = blk * NAT_QROWS
        k0 = int(np.clip(r0 - NAT_ROWS // 2, 0, rows - NAT_KROWS))
        q_rows = []
        for qr in range(r0, r0 + NAT_QROWS):
            rs = int(np.clip(qr - NAT_ROWS // 2, 0, rows - NAT_ROWS))
            q_rows.append(jnp.concatenate(
                [toep[:, kr - qr + NAT_ROWS - 1] if rs <= kr < rs + NAT_ROWS else masked
                 for kr in range(k0, k0 + NAT_KROWS)], axis=-1))
        cases.append(jnp.concatenate(q_rows, axis=-2))
    return jnp.stack(cases, axis=0)


def _lat_attention(proj, caches, lp, l, lam_init):
    cnk, cnv, cgk, cgv, cdk, cdv = caches
    nq_blocks = DEC_SEQ // TQ
    qspec = lambda name: pl.BlockSpec((TQ, 256), lambda b, i: (b * nq_blocks + i, _COL[name]))
    kvspec = lambda name, w: pl.BlockSpec((DEC_SEQ, w), lambda b, i: (b, _COL[name]))
    cspec = lambda w: pl.BlockSpec((None, None, PAST_LEN, w), lambda b, i: (b, l, 0, 0))
    ospec = pl.BlockSpec((TQ, 256), lambda b, i: (b * nq_blocks + i, 0))
    oshape = jax.ShapeDtypeStruct((N_LAT, 256), F32)
    o_gqa = pl.pallas_call(
        _gqa_lat_kernel,
        grid=(DEC_BATCH, nq_blocks),
        in_specs=[qspec("gq"), kvspec("gk", 128), kvspec("gv", 128), cspec(128), cspec(128)],
        out_specs=ospec, out_shape=oshape,
        compiler_params=_cparams("arbitrary", "arbitrary"),
        name="gqa_lat_attention",
    )(proj, proj, proj, cgk.reshape(DEC_BATCH, DEPTH, PAST_LEN, 128), cgv.reshape(DEC_BATCH, DEPTH, PAST_LEN, 128))
    o_dif = pl.pallas_call(
        functools.partial(_diff_lat_kernel, lam_init=lam_init),
        grid=(DEC_BATCH, nq_blocks),
        in_specs=[qspec("dq"), kvspec("dk", 256), kvspec("dv", 256), cspec(256), cspec(256),
                  _const_spec((4, DIFF_DIM)), _const_spec((1, 256)), _const_spec((256, 256))],
        out_specs=ospec, out_shape=oshape,
        compiler_params=_cparams("arbitrary", "arbitrary"),
        name="diff_lat_attention",
    )(proj, proj, proj, cdk.reshape(DEC_BATCH, DEPTH, PAST_LEN, 256), cdv.reshape(DEC_BATCH, DEPTH, PAST_LEN, 256),
      lp["diff_lambda"], lp["diff_sub_g"], lp["bd64"])
    nblk = DEC_SEQ // (NAT_QROWS * GRID_W)
    nat_q = NAT_QROWS * GRID_W
    o_nat = pl.pallas_call(
        _nat_lat_kernel,
        grid=(DEC_BATCH, nblk),
        in_specs=[pl.BlockSpec((nat_q, 256), lambda b, i: (b * nblk + i, _COL["nq"])),
                  kvspec("nk", 256), kvspec("nv", 256), cspec(256), cspec(256),
                  pl.BlockSpec((1, NAT_HEADS, nat_q, NAT_KROWS * GRID_W),
                               lambda b, i: (jnp.where(i == 0, 0, jnp.where(i == nblk - 1, 2, 1)), 0, 0, 0))],
        out_specs=pl.BlockSpec((nat_q, 256), lambda b, i: (b * nblk + i, 0)),
        out_shape=oshape,
        compiler_params=_cparams("arbitrary", "arbitrary"),
        name="nat_lat_attention",
    )(proj, proj, proj, cnk.reshape(DEC_BATCH, DEPTH, PAST_LEN, 256), cnv.reshape(DEC_BATCH, DEPTH, PAST_LEN, 256),
      lp["nat_bias"])
    return o_nat, o_gqa, o_dif


def _merge_kernel(x_ref, mod_ref, n1_ref, b0_ref, b1_ref, b2_ref, b3_ref, wb_ref, wg_ref, bg_ref, wo_ref, o_ref):
    x = x_ref[...]
    mod = mod_ref[0]
    hb = _rms_mod(x, n1_ref[...], mod[:, D_MODEL:2 * D_MODEL], mod[:, 0:D_MODEL]).astype(BF16)
    merged = None
    for n, b_ref in enumerate((b0_ref, b1_ref, b2_ref, b3_ref)):
        cols = slice(n * D_MODEL, (n + 1) * D_MODEL)
        gate = _sigmoid(jnp.dot(hb, wg_ref[:, cols], preferred_element_type=F32) + bg_ref[:, cols])
        term = gate * _dot(b_ref[...], wb_ref[n])
        merged = term if merged is None else merged + term
    out = _dot(merged, wo_ref[...])
    o_ref[...] = x + mod[:, 2 * D_MODEL:3 * D_MODEL] * out


def _merge(x, mod_l, lp, branches, latent):
    n_tok = x.shape[0]
    return pl.pallas_call(
        _merge_kernel,
        grid=(n_tok // TB,),
        in_specs=[pl.BlockSpec((TB, D_MODEL), lambda i: (i, 0)), _mod_spec(latent), _const_spec((1, D_MODEL))]
                 + [pl.BlockSpec((TB, BRANCH_W), functools.partial(lambda i, c: (i, c), c=col)) for _, col in branches]
                 + [_const_spec((N_BRANCH, BRANCH_W, D_MODEL)), _const_spec((D_MODEL, N_BRANCH * D_MODEL)),
                    _const_spec((1, N_BRANCH * D_MODEL)), _const_spec((D_MODEL, D_MODEL))],
        out_specs=pl.BlockSpec((TB, D_MODEL), lambda i: (i, 0)),
        out_shape=jax.ShapeDtypeStruct((n_tok, D_MODEL), F32),
        compiler_params=_cparams("arbitrary"),
        name="merge_lat" if latent else "merge_ctx",
    )(x, mod_l, lp["norm1_g"], *[a for a, _ in branches], lp["w_branch"], lp["w_gate"], lp["b_gate"], lp["w_out"])


def _top_desc(s, k):
    rows = []
    cur = s
    for _ in range(k):
        m = jnp.max(cur, axis=0, keepdims=True)
        rows.append(m)
        cur = jnp.where(cur == m, NEG, cur)
    return jnp.concatenate(rows, axis=0)


def _peer_score_kernel(x_ref, mod_ref, n2_ref, wqt_ref, sk_ref, ht_ref, s_ref, st_ref):
    mod = mod_ref[0]
    h2 = _rms_mod(x_ref[...], n2_ref[...], mod[:, 4 * D_MODEL:5 * D_MODEL], mod[:, 3 * D_MODEL:4 * D_MODEL])
    htb = h2.T.astype(BF16)
    ht_ref[...] = htb
    qt = jnp.dot(wqt_ref[...], htb, preferred_element_type=F32)
    k1 = PEER_TOPK + 1
    half = PEER_TOPK // 2
    row = lax.broadcasted_iota(jnp.int32, (k1, 1), 0)
    for h in range(PEER_HEADS):
        tops = []
        for p in range(2):
            r0 = (2 * h + p) * (PEER_QDIM // 2)
            s = jnp.dot(sk_ref[p], qt[r0:r0 + PEER_QDIM // 2, :].astype(BF16), preferred_element_type=F32)
            s_ref[p, h] = s
            tops.append(_top_desc(s, k1))
        t1, t2 = tops
        cands = [t1[0:1] + t2, jnp.where(row >= 1, t1 + t2[0:1], NEG)]
        for a in range(1, half):
            nb = k1 // (a + 1)
            cands.append(jnp.where((row[0:half] >= 1) & (row[0:half] < nb), t1[a:a + 1] + t2[0:half], NEG))
        best = _top_desc(jnp.concatenate(cands, axis=0), k1)
        z = jnp.sum(jnp.exp(best[0:PEER_TOPK] - best[0:1]), axis=0, keepdims=True)
        thr = 0.5 * (best[PEER_TOPK - 1:PEER_TOPK] + best[PEER_TOPK:k1])
        st_ref[4 * h:4 * h + 4, :] = jnp.concatenate([thr, t1[0:1], t2[0:1], 1.0 / z], axis=0)


def _peer_dense_kernel(ht_ref, s_ref, st_ref, u_ref, v_ref, vlast_ref, x_ref, mod_ref, o_ref,
                       acc_ref, d_ref, e1_ref, e2_ref, a_ref, g_ref, w_ref, wt_ref):
    e = pl.program_id(1)
    n_tok = acc_ref.shape[0]
    cur = e % 2
    prev = 1 - cur

    @pl.when(e == 0)
    def _():
        acc_ref[...] = jnp.zeros_like(acc_ref)
        w_ref[1] = jnp.zeros((PEER_ET, n_tok), F32)
        for h in range(PEER_HEADS):
            st = st_ref[4 * h:4 * h + 4, :]
            s1 = s_ref[0, h]
            d_ref[h] = st[0:1] - s1
            e1_ref[h] = jnp.exp(s1 - st[1:2]) * (0.5 * st[3:4])
            e2_ref[h] = jnp.exp(s_ref[1, h] - st[2:3])

    wt_ref[...] = w_ref[prev].T.astype(BF16)
    mw = 256

    def act_piece(p):
        mh, nh = divmod(p, n_tok // mw)
        a_ref[mh * mw:(mh + 1) * mw, nh * mw:(nh + 1) * mw] = jnp.dot(
            u_ref[mh * mw:(mh + 1) * mw, :], ht_ref[:, nh * mw:(nh + 1) * mw], preferred_element_type=F32)

    def val_piece(q):
        cols = slice(q * mw, (q + 1) * mw)
        acc_ref[:, cols] += jnp.dot(wt_ref[...], v_ref[:, cols], preferred_element_type=F32)

    chunks = [(rr, lt) for rr in range(PEER_ET // PEER_KEYS) for lt in range(n_tok // 128)]
    rows_of = lambda rr: slice(rr * PEER_KEYS, (rr + 1) * PEER_KEYS)
    lanes_of = lambda lt: slice(lt * 128, (lt + 1) * 128)

    def gate_chunk(rr, lt):
        r = e * (PEER_ET // PEER_KEYS) + rr
        g = None
        for h in range(PEER_HEADS):
            sel = s_ref[1, h, :, lanes_of(lt)] >= d_ref[h, pl.ds(r, 1), :][:, lanes_of(lt)]
            term = jnp.where(sel, e2_ref[h, :, lanes_of(lt)] * e1_ref[h, pl.ds(r, 1), :][:, lanes_of(lt)], 0.0)
            g = term if g is None else g + term
        g_ref[rows_of(rr), lanes_of(lt)] = g

    def weight_chunk(rr, lt):
        a = a_ref[rows_of(rr), lanes_of(lt)]
        t = jnp.tanh(a * (0.7978845608028654 + (0.7978845608028654 * 0.044715) * (a * a)))
        w_ref[cur, rows_of(rr), lanes_of(lt)] = g_ref[rows_of(rr), lanes_of(lt)] * (a + a * t)

    def alternate(jobs, chunk_fn):
        done = 0
        for i, c in enumerate(chunks):
            while done * len(chunks) <= i * len(jobs) and done < len(jobs):
                jobs[done]()
                done += 1
            chunk_fn(*c)

    n_act = (PEER_ET // mw) * (n_tok // mw)
    n_val = D_MODEL // mw
    act_jobs = [functools.partial(act_piece, p) for p in range(n_act)]
    val_jobs = [functools.partial(val_piece, q) for q in range(n_val)]
    alternate(act_jobs[:n_act // 2] + val_jobs[:1] + act_jobs[n_act // 2:] + val_jobs[1:n_val // 2], gate_chunk)
    alternate(val_jobs[n_val // 2:], weight_chunk)

    @pl.when(e == pl.num_programs(1) - 1)
    def _():
        acc = acc_ref[...] + jnp.dot(w_ref[cur].T.astype(BF16), vlast_ref[...], preferred_element_type=F32)
        o_ref[...] = x_ref[...] + mod_ref[0][:, 5 * D_MODEL:6 * D_MODEL] * acc


def _peer(x, mod_l, lp, latent):
    n_tok = x.shape[0]
    nb = n_tok // TBP
    if latent:
        mod_spec = pl.BlockSpec((1, 1, 6 * D_MODEL), lambda i, *_: (1 + i // (DEC_SEQ // TBP), 0, 0))
    else:
        mod_spec = pl.BlockSpec((1, 1, 6 * D_MODEL), lambda i, *_: (0, 0, 0))
    n_keys2 = PEER_HEADS * PEER_QDIM
    ht, s, st = pl.pallas_call(
        _peer_score_kernel,
        grid=(nb,),
        in_specs=[pl.BlockSpec((TBP, D_MODEL), lambda i: (i, 0)), mod_spec, _const_spec((1, D_MODEL)),
                  _const_spec((n_keys2, D_MODEL)), _const_spec((2, PEER_KEYS, PEER_QDIM // 2))],
        out_specs=[pl.BlockSpec((D_MODEL, TBP), lambda i: (0, i)),
                   pl.BlockSpec((2, PEER_HEADS, PEER_KEYS, TBP), lambda i: (0, 0, 0, i)),
                   pl.BlockSpec((4 * PEER_HEADS, TBP), lambda i: (0, i))],
        out_shape=[jax.ShapeDtypeStruct((D_MODEL, n_tok), BF16),
                   jax.ShapeDtypeStruct((2, PEER_HEADS, PEER_KEYS, n_tok), F32),
                   jax.ShapeDtypeStruct((4 * PEER_HEADS, n_tok), F32)],
        compiler_params=_cparams("arbitrary"),
        name="peer_scores_lat" if latent else "peer_scores_ctx",
    )(x, mod_l, lp["norm2_g"], lp["peer_wqt"], lp["peer_subkeys"])
    n_et = PEER_KEYS * PEER_KEYS // PEER_ET
    return pl.pallas_call(
        _peer_dense_kernel,
        grid=(nb, n_et),
        in_specs=[pl.BlockSpec((D_MODEL, TBP), lambda i, e: (0, i)),
                  pl.BlockSpec((2, PEER_HEADS, PEER_KEYS, TBP), lambda i, e: (0, 0, 0, i)),
                  pl.BlockSpec((4 * PEER_HEADS, TBP), lambda i, e: (0, i)),
                  pl.BlockSpec((PEER_ET, D_MODEL), lambda i, e: (e, 0)),
                  pl.BlockSpec((PEER_ET, D_MODEL), lambda i, e: (jnp.maximum(e - 1, 0), 0)),
                  pl.BlockSpec((PEER_ET, D_MODEL), lambda i, e: (jnp.where(e == n_et - 1, n_et - 1, 0), 0)),
                  pl.BlockSpec((TBP, D_MODEL), lambda i, e: (i, 0)),
                  mod_spec],
        out_specs=pl.BlockSpec((TBP, D_MODEL), lambda i, e: (i, 0)),
        out_shape=jax.ShapeDtypeStruct((n_tok, D_MODEL), F32),
        scratch_shapes=[pltpu.VMEM((TBP, D_MODEL), F32)]
                       + [pltpu.VMEM((PEER_HEADS, PEER_KEYS, TBP), F32)] * 3
                       + [pltpu.VMEM((PEER_ET, TBP), F32)] * 2
                       + [pltpu.VMEM((2, PEER_ET, TBP), F32), pltpu.VMEM((TBP, PEER_ET), BF16)],
        compiler_params=_cparams("arbitrary", "arbitrary"),
        name="peer_dense_lat" if latent else "peer_dense_ctx",
    )(ht, s, st, lp["peer_u"], lp["peer_v"], lp["peer_v"], x, mod_l)


def _layer_params(l, w):
    tile = lambda g, n: jnp.tile(g, n)
    gains = jnp.stack([
        tile(w["nat_qk_g"][l, 0], 4), tile(w["nat_qk_g"][l, 1], 4),
        tile(w["gqa_qk_g"][l, 0], 4), tile(w["gqa_qk_g"][l, 1], 4),
        tile(w["diff_qk_g"][l, 0], 8), tile(w["diff_qk_g"][l, 1], 8),
        jnp.zeros((256,), F32), jnp.zeros((256,), F32)])
    return dict(
        norm1_g=w["norm1_g"][l].reshape(1, D_MODEL),
        norm2_g=w["norm2_g"][l].reshape(1, D_MODEL),
        w_in=w["w_in"][l].astype(BF16),
        qk_gains=gains,
        bd64=_block_ones(256, 64), bd32=_block_ones(256, 32), bd256=_block_ones(256, 256),
        rope=w["rope"],
        sgu_norm_g=w["sgu_norm_g"][l].reshape(1, SGU_WIDTH),
        sgu_w=w["sgu_w"][l].astype(BF16),
        sgu_b=jnp.repeat(w["sgu_b"][l].T, SGU_WIDTH // SGU_GROUPS, axis=1),
        diff_lambda=w["diff_lambda"][l],
        diff_sub_g=tile(w["diff_sub_g"][l], 4).reshape(1, 256),
        nat_bias=_nat_bias_table(w["nat_rpb"][l]),
        w_branch=w["w_branch"][l].astype(BF16),
        w_gate=w["w_gate"][l].astype(BF16),
        b_gate=w["b_gate"][l].reshape(1, N_BRANCH * D_MODEL),
        w_out=w["w_out"][l].astype(BF16),
        peer_wqt=w["peer_wq"][l].T.astype(BF16),
        peer_subkeys=w["peer_subkeys"][l].astype(BF16),
        peer_u=w["peer_u"][l].astype(BF16),
        peer_v=w["peer_v"][l].astype(BF16),
    )


def kernel(x_prompt, x_sample, c, cache_nat_k, cache_nat_v, cache_gqa_k, cache_gqa_v, cache_diff_k, cache_diff_v, c_ctx, w_mod, b_mod, norm1_g, norm2_g, w_in, nat_qk_g, nat_rpb, gqa_qk_g, diff_qk_g, diff_lambda, diff_sub_g, sgu_norm_g, sgu_w, sgu_b, w_branch, w_gate, b_gate, w_out, peer_wq, peer_subkeys, peer_u, peer_v):
    w = dict(norm1_g=norm1_g, norm2_g=norm2_g, w_in=w_in, nat_qk_g=nat_qk_g, nat_rpb=nat_rpb,
             gqa_qk_g=gqa_qk_g, diff_qk_g=diff_qk_g, diff_lambda=diff_lambda, diff_sub_g=diff_sub_g,
             sgu_norm_g=sgu_norm_g, sgu_w=sgu_w, sgu_b=sgu_b, w_branch=w_branch, w_gate=w_gate,
             b_gate=b_gate, w_out=w_out, peer_wq=peer_wq, peer_subkeys=peer_subkeys, peer_u=peer_u,
             peer_v=peer_v, rope=_rope_tables())
    cvec = jnp.concatenate([c_ctx[None], c, jnp.zeros((MOD_ROWS - 1 - DEC_BATCH, D_MODEL), F32)], axis=0)
    mod = _modulation(cvec, w_mod, b_mod).reshape(DEPTH, MOD_ROWS, 1, 6 * D_MODEL)
    xp = x_prompt.reshape(N_CTX, D_MODEL)
    xs = x_sample.reshape(N_LAT, D_MODEL)
    new = []
    for l in range(DEPTH):
        lp = _layer_params(l, w)
        lam_init = 0.8 - 0.6 * math.exp(-0.3 * l)
        caches = (cache_nat_k, cache_nat_v, cache_gqa_k, cache_gqa_v, cache_diff_k, cache_diff_v)
        proj = _in_projection(xp, mod[l], lp, latent=False)
        new.append(proj)
        attn = _ctx_attention(proj, lp, lam_init)
        xp = _merge(xp, mod[l], lp, [(attn, 0), (attn, 1), (attn, 2), (proj, _COL["sgu"])], latent=False)
        xp = _peer(xp, mod[l], lp, latent=False)
        proj = _in_projection(xs, mod[l], lp, latent=True)
        o_nat, o_gqa, o_dif = _lat_attention(proj, caches, lp, l, lam_init)
        xs = _merge(xs, mod[l], lp, [(o_nat, 0), (o_gqa, 0), (o_dif, 0), (proj, _COL["sgu"])], latent=True)
        xs = _peer(xs, mod[l], lp, latent=True)

    def cache_out(lo, width, tail):
        a = jnp.stack([p[:, lo:lo + width].reshape(BATCH, SEQ, width) for p in new], axis=1)
        return a.reshape((BATCH, DEPTH, SEQ) + tail)

    return (xp.reshape(BATCH, SEQ, D_MODEL), xs.reshape(DEC_BATCH, DEC_SEQ, D_MODEL),
            cache_out(256, 256, (NAT_HEADS, HEAD_DIM)), cache_out(512, 256, (NAT_HEADS, HEAD_DIM)),
            cache_out(1024, 128, (GQA_KV_HEADS, HEAD_DIM)), cache_out(1152, 128, (GQA_KV_HEADS, HEAD_DIM)),
            cache_out(1536, 256, (DIFF_HEADS, 2, DIFF_DIM)), cache_out(1792, 256, (DIFF_HEADS, HEAD_DIM)))
```

```python
import functools
import math

import numpy as np
import jax
import jax.numpy as jnp
from jax import lax
from jax.experimental import pallas as pl
from jax.experimental.pallas import tpu as pltpu

F32 = jnp.float32
BF16 = jnp.bfloat16

D_MODEL = 1024
BATCH = 16
SEQ = 256
DEPTH = 2
DEC_BATCH = 2
DEC_SEQ = 4096
PAST_LEN = 512
GRID_W = 64
HEAD_DIM = 64
NAT_HEADS = 4
NAT_ROWS = 8
NAT_COLS = 16
GQA_HEADS = 4
GQA_KV_HEADS = 2
DIFF_HEADS = 4
DIFF_DIM = 32
SGU_WIDTH = 256
SGU_GROUPS = 4
CHUNK = 128
N_BRANCH = 4
BRANCH_W = 256
IN_W = 2560
PEER_HEADS = 8
PEER_KEYS = 128
PEER_QDIM = 256
PEER_TOPK = 16
ROPE_BASE = 10000.0
EPS = 1e-6

N_CTX = BATCH * SEQ
N_LAT = DEC_BATCH * DEC_SEQ
MOD_ROWS = 8
TB = 256
TQ = 256
PROJ_W = 2304
NAT_QROWS = 4
NAT_KROWS = NAT_QROWS + NAT_ROWS
TBP = 512
PEER_ET = 512
BF16_ROWS = 16
VMEM_LIMIT = 56 * 1024 * 1024
NEG = -1e30


def _cparams(*sem):
    return pltpu.CompilerParams(dimension_semantics=sem, vmem_limit_bytes=VMEM_LIMIT)


def _const_spec(shape):
    return pl.BlockSpec(shape, lambda *_: (0,) * len(shape))


def _gelu(x):
    return 0.5 * x * (1.0 + jnp.tanh(0.7978845608028654 * (x + 0.044715 * (x * x * x))))


def _sigmoid(x):
    return 1.0 / (1.0 + jnp.exp(-x))


def _dot(a, b):
    return jnp.dot(a.astype(BF16), b.astype(BF16), preferred_element_type=F32)


def _dot_nt(a, b):
    return lax.dot_general(a.astype(BF16), b.astype(BF16), (((1,), (1,)), ((), ())),
                           preferred_element_type=F32)


def _group_mean(y2, ones_bd):
    hi = y2.astype(BF16)
    lo = (y2 - hi.astype(F32)).astype(BF16)
    return (jnp.dot(hi, ones_bd, preferred_element_type=F32)
            + jnp.dot(lo, ones_bd, preferred_element_type=F32))


def _block_ones(width, group):
    idx = np.arange(width) // group
    return jnp.asarray((idx[:, None] == idx[None, :]).astype(np.float32) / group, dtype=BF16)


def _rms_mod(x, gain, scale, shift):
    xn = x * lax.rsqrt(jnp.mean(x * x, axis=-1, keepdims=True) + EPS) * gain
    return xn * (1.0 + scale) + shift


def _mod_spec(latent):
    if latent:
        return pl.BlockSpec((1, 1, 6 * D_MODEL), lambda i, *_: (1 + i // (DEC_SEQ // TB), 0, 0))
    return pl.BlockSpec((1, 1, 6 * D_MODEL), lambda i, *_: (0, 0, 0))


def _mod_kernel(c_ref, w_ref, b_ref, o_ref):
    c = c_ref[...]
    s = c * _sigmoid(c)
    o_ref[0] = _dot(s, w_ref[0]) + b_ref[0]


def _modulation(cvec, w_mod, b_mod):
    tn = 1536
    return pl.pallas_call(
        _mod_kernel,
        grid=(DEPTH, 6 * D_MODEL // tn),
        in_specs=[pl.BlockSpec((MOD_ROWS, D_MODEL), lambda l, j: (0, 0)),
                  pl.BlockSpec((1, D_MODEL, tn), lambda l, j: (l, 0, j)),
                  pl.BlockSpec((1, 1, tn), lambda l, j: (l, 0, j))],
        out_specs=pl.BlockSpec((1, MOD_ROWS, tn), lambda l, j: (l, 0, j)),
        out_shape=jax.ShapeDtypeStruct((DEPTH, MOD_ROWS, 6 * D_MODEL), F32),
        compiler_params=_cparams("arbitrary", "arbitrary"),
        name="modulation",
    )(cvec, w_mod, b_mod.reshape(DEPTH, 1, 6 * D_MODEL))


def _rope_tables():
    t = np.arange(DEC_SEQ)
    pos = (t // GRID_W, t % GRID_W)
    out = []
    for d in (HEAD_DIM, DIFF_DIM):
        half, quarter = d // 2, d // 4
        inv = ROPE_BASE ** (-np.arange(quarter, dtype=np.float32) * 2.0 / half)
        lane = np.arange(128) % d
        part, j = lane // half, lane % half
        ang = np.stack([pos[0][:, None] * inv[None, :], pos[1][:, None] * inv[None, :]], axis=1)
        a = ang[:, part, j % quarter].astype(np.float32)
        cos, sin = np.cos(a), np.sin(a)
        out += [jnp.asarray(cos, dtype=F32),
                jnp.asarray(np.where(j < quarter, -sin, 0.0), dtype=F32),
                jnp.asarray(np.where(j >= quarter, sin, 0.0), dtype=F32)]
    return out


def _rope(y, c, sa, sb, quarter):
    w = y.shape[-1]
    rep = w // 128
    if rep > 1:
        c, sa, sb = (jnp.concatenate([t] * rep, axis=-1) for t in (c, sa, sb))
    up = pltpu.roll(y, w - quarter, 1)
    dn = pltpu.roll(y, quarter, 1)
    return y * c + up * sa + dn * sb


def _inproj_kernel(*refs, latent):
    (x_ref, mod_ref, n1_ref, w_ref, g_ref, bd64_ref, bd32_ref, bd256_ref,
     sgn_ref, sgw_ref, sgb_ref) = refs[:11]
    o_ref = refs[-1]
    mod = mod_ref[0]
    h = _rms_mod(x_ref[...], n1_ref[...], mod[:, D_MODEL:2 * D_MODEL], mod[:, 0:D_MODEL])
    y = jnp.dot(h.astype(BF16), w_ref[...], preferred_element_type=F32)

    def qk_norm(lo, width, bd, gain_row):
        v = y[:, lo:lo + width]
        ms = _group_mean(v * v, bd)
        return v * lax.rsqrt(ms + EPS) * g_ref[gain_row:gain_row + 1, 0:width]

    bd64 = bd64_ref[...]
    bd32 = bd32_ref[...]
    if latent:
        c64, sa64, sb64, c32, sa32, sb32 = (r[...] for r in refs[11:17])
        rope64 = functools.partial(_rope, c=c64, sa=sa64, sb=sb64, quarter=HEAD_DIM // 4)
        rope32 = functools.partial(_rope, c=c32, sa=sa32, sb=sb32, quarter=DIFF_DIM // 4)
    else:
        rope64 = rope32 = lambda v: v
    o_ref[:, 0:256] = qk_norm(0, 256, bd64, 0)
    o_ref[:, 256:512] = qk_norm(256, 256, bd64, 1)
    o_ref[:, 512:768] = y[:, 512:768]
    o_ref[:, 768:1024] = rope64(qk_norm(768, 256, bd64, 2))
    o_ref[:, 1024:1152] = rope64(qk_norm(1024, 128, bd64[0:128, 0:128], 3))
    o_ref[:, 1152:1280] = y[:, 1152:1280]
    o_ref[:, 1280:1536] = rope32(qk_norm(1280, 256, bd32, 4))
    o_ref[:, 1536:1792] = rope32(qk_norm(1536, 256, bd32, 5))
    o_ref[:, 1792:2048] = y[:, 1792:2048]
    u = _gelu(y[:, 2048:2304])
    v = _gelu(y[:, 2304:2560])
    vn = v * lax.rsqrt(_group_mean(v * v, bd256_ref[...]) + EPS) * sgn_ref[...]
    vnb = vn.astype(BF16)
    lane_group = lax.broadcasted_iota(jnp.int32, (CHUNK, SGU_WIDTH), 1) // (SGU_WIDTH // SGU_GROUPS)
    for n in range(TB // CHUNK):
        vc = vnb[n * CHUNK:(n + 1) * CHUNK, :]
        s = sgb_ref[...]
        for g in range(SGU_GROUPS):
            sg = jnp.dot(sgw_ref[g], vc, preferred_element_type=F32)
            s = s + jnp.where(lane_group == g, sg, 0.0)
        o_ref[n * CHUNK:(n + 1) * CHUNK, 2048:2304] = u[n * CHUNK:(n + 1) * CHUNK, :] * s


def _in_projection(x, mod_l, lp, latent):
    n_tok = x.shape[0]
    in_specs = [pl.BlockSpec((TB, D_MODEL), lambda i: (i, 0)),
                _mod_spec(latent),
                _const_spec((1, D_MODEL)),
                _const_spec((D_MODEL, IN_W)),
                _const_spec((8, 256)),
                _const_spec((256, 256)), _const_spec((256, 256)), _const_spec((256, 256)),
                _const_spec((1, SGU_WIDTH)),
                _const_spec((SGU_GROUPS, CHUNK, CHUNK)),
                _const_spec((CHUNK, SGU_WIDTH))]
    args = [x, mod_l, lp["norm1_g"], lp["w_in"], lp["qk_gains"], lp["bd64"], lp["bd32"], lp["bd256"],
            lp["sgu_norm_g"], lp["sgu_w"], lp["sgu_b"]]
    if latent:
        in_specs += [pl.BlockSpec((TB, 128), lambda i: (i % (DEC_SEQ // TB), 0))] * 6
        args += lp["rope"]
    return pl.pallas_call(
        functools.partial(_inproj_kernel, latent=latent),
        grid=(n_tok // TB,),
        in_specs=in_specs,
        out_specs=pl.BlockSpec((TB, PROJ_W), lambda i: (i, 0)),
        out_shape=jax.ShapeDtypeStruct((n_tok, PROJ_W), F32),
        compiler_params=_cparams("arbitrary"),
        name="in_projection_lat" if latent else "in_projection_ctx",
    )(*args)


def _softmax_parts(scores):
    m = functools.reduce(jnp.maximum, [jnp.max(s, axis=-1, keepdims=True) for s in scores])
    ps = [jnp.exp(s - m) for s in scores]
    l = functools.reduce(jnp.add, [jnp.sum(p, axis=-1, keepdims=True) for p in ps])
    return ps, 1.0 / l


def _diff_lambda(lam_ref, lam_init):
    lv = lam_ref[...]
    a = jnp.sum(lv[0:1] * lv[1:2], axis=-1, keepdims=True)
    b = jnp.sum(lv[2:3] * lv[3:4], axis=-1, keepdims=True)
    return jnp.exp(a) - jnp.exp(b) + lam_init


def _head(ref_or_val, h, width=HEAD_DIM):
    return ref_or_val[:, h * width:(h + 1) * width]


def _mha(q, ks, vs, n_heads, kv_group, scale, biases=None):
    outs = []
    for h in range(n_heads):
        g = h // kv_group
        qh = _head(q, h) * scale
        scores = [_dot_nt(qh, _head(k, g)) for k in ks]
        if biases is not None:
            scores = [s if b is None else s + b[h] for s, b in zip(scores, biases)]
        ps, rl = _softmax_parts(scores)
        o = functools.reduce(jnp.add, [_dot(p, _head(v, g)) for p, v in zip(ps, vs)])
        outs.append(o * rl)
    return jnp.concatenate(outs, axis=-1)


def _diff_attn(q, ks, vs, lam, sub_gain, bd64, lam_init):
    scale = DIFF_DIM ** -0.5
    outs = []
    for h in range(DIFF_HEADS):
        pd = None
        for m in range(2):
            qm = _head(q, 2 * h + m, DIFF_DIM)
            scores = [_dot_nt(qm, _head(k, 2 * h + m, DIFF_DIM)) * scale for k in ks]
            ps, rl = _softmax_parts(scores)
            if m == 0:
                pd = [p * rl for p in ps]
            else:
                pd = [a - lam * (p * rl) for a, p in zip(pd, ps)]
        outs.append(functools.reduce(jnp.add, [_dot(p, _head(v, h)) for p, v in zip(pd, vs)]))
    o = jnp.concatenate(outs, axis=-1)
    ms = _group_mean(o * o, bd64)
    return o * lax.rsqrt(ms + EPS) * sub_gain * (1.0 - lam_init)


def _ctx_attn_kernel(nq, nk, nv, gq, gk, gv, dq, dk, dv, lam_ref, subg_ref, bd64_ref, o_ref, *, lam_init):
    scale = HEAD_DIM ** -0.5
    o_ref[:, 0:256] = _mha(nq[...], [nk[...]], [nv[...]], NAT_HEADS, 1, scale)
    o_ref[:, 256:512] = _mha(gq[...], [gk[...]], [gv[...]], GQA_HEADS, GQA_HEADS // GQA_KV_HEADS, scale)
    lam = _diff_lambda(lam_ref, lam_init)
    o_ref[:, 512:768] = _diff_attn(dq[...], [dk[...]], [dv[...]], lam, subg_ref[...], bd64_ref[...], lam_init)


_COL = dict(nq=0, nk=1, nv=2, gq=3, gk=8, gv=9, dq=5, dk=6, dv=7, sgu=8)


def _ctx_attention(proj, lp, lam_init):
    blk = lambda name, w: pl.BlockSpec((SEQ, w), lambda b: (b, _COL[name]))
    return pl.pallas_call(
        functools.partial(_ctx_attn_kernel, lam_init=lam_init),
        grid=(BATCH,),
        in_specs=[blk("nq", 256), blk("nk", 256), blk("nv", 256),
                  blk("gq", 256), blk("gk", 128), blk("gv", 128),
                  blk("dq", 256), blk("dk", 256), blk("dv", 256),
                  _const_spec((4, DIFF_DIM)), _const_spec((1, 256)), _const_spec((256, 256))],
        out_specs=pl.BlockSpec((SEQ, 768), lambda b: (b, 0)),
        out_shape=jax.ShapeDtypeStruct((N_CTX, 768), F32),
        compiler_params=_cparams("arbitrary"),
        name="ctx_attention",
    )(*([proj] * 9), lp["diff_lambda"], lp["diff_sub_g"], lp["bd64"])


def _gqa_lat_kernel(q_ref, k_ref, v_ref, ck_ref, cv_ref, o_ref):
    o_ref[...] = _mha(q_ref[...], [k_ref[...], ck_ref[...]], [v_ref[...], cv_ref[...]],
                      GQA_HEADS, GQA_HEADS // GQA_KV_HEADS, HEAD_DIM ** -0.5)


def _diff_lat_kernel(q_ref, k_ref, v_ref, ck_ref, cv_ref, lam_ref, subg_ref, bd64_ref, o_ref, *, lam_init):
    lam = _diff_lambda(lam_ref, lam_init)
    o_ref[...] = _diff_attn(q_ref[...], [k_ref[...], ck_ref[...]], [v_ref[...], cv_ref[...]],
                            lam, subg_ref[...], bd64_ref[...], lam_init)


def _nat_lat_kernel(q_ref, k_ref, v_ref, ck_ref, cv_ref, bias_ref, o_ref):
    i = pl.program_id(1)
    k_row0 = jnp.clip(NAT_QROWS * i - NAT_ROWS // 2, 0, GRID_W - NAT_KROWS)
    start = pl.multiple_of(k_row0 * GRID_W, GRID_W)
    kw = k_ref[pl.ds(start, NAT_KROWS * GRID_W), :]
    vw = v_ref[pl.ds(start, NAT_KROWS * GRID_W), :]
    o_ref[...] = _mha(q_ref[...], [kw, ck_ref[...]], [vw, cv_ref[...]], NAT_HEADS, 1, HEAD_DIM ** -0.5,
                      biases=[bias_ref[0], None])


def _nat_bias_table(rpb):
    rows = DEC_SEQ // GRID_W
    nblk = rows // NAT_QROWS
    pad = jnp.pad(rpb, ((0, 0), (0, 0), (GRID_W - NAT_COLS, GRID_W - NAT_COLS)))
    toep = jnp.stack([pad[:, :, GRID_W - 1 - c:2 * GRID_W - 1 - c] for c in range(GRID_W)], axis=2)
    col = np.arange(GRID_W)
    cs = np.clip(col - NAT_COLS // 2, 0, GRID_W - NAT_COLS)
    col_ok = (col[None, :] >= cs[:, None]) & (col[None, :] < cs[:, None] + NAT_COLS)
    toep = jnp.where(col_ok, toep, NEG)
    masked = jnp.full((NAT_HEADS, GRID_W, GRID_W), NEG, F32)
    cases = []
    for blk in (0, 1, nblk - 1):
        r0 = blk * NAT_QROWS
        k0 = int(np.clip(r0 - NAT_ROWS // 2, 0, rows - NAT_KROWS))
        q_rows = []
        for qr in range(r0, r0 + NAT_QROWS):
            rs = int(np.clip(qr - NAT_ROWS // 2, 0, rows - NAT_ROWS))
            q_rows.append(jnp.concatenate(
                [toep[:, kr - qr + NAT_ROWS - 1] if rs <= kr < rs + NAT_ROWS else masked
                 for kr in range(k0, k0 + NAT_KROWS)], axis=-1))
        cases.append(jnp.concatenate(q_rows, axis=-2))
    return jnp.stack(cases, axis=0)


def _lat_attention(proj, caches, lp, l, lam_init):
    cnk, cnv, cgk, cgv, cdk, cdv = caches
    nq_blocks = DEC_SEQ // TQ
    qspec = lambda name: pl.BlockSpec((TQ, 256), lambda b, i: (b * nq_blocks + i, _COL[name]))
    kvspec = lambda name, w: pl.BlockSpec((DEC_SEQ, w), lambda b, i: (b, _COL[name]))
    cspec = lambda w: pl.BlockSpec((None, None, PAST_LEN, w), lambda b, i: (b, l, 0, 0))
    ospec = pl.BlockSpec((TQ, 256), lambda b, i: (b * nq_blocks + i, 0))
    oshape = jax.ShapeDtypeStruct((N_LAT, 256), F32)
    o_gqa = pl.pallas_call(
        _gqa_lat_kernel,
        grid=(DEC_BATCH, nq_blocks),
        in_specs=[qspec("gq"), kvspec("gk", 128), kvspec("gv", 128), cspec(128), cspec(128)],
        out_specs=ospec, out_shape=oshape,
        compiler_params=_cparams("arbitrary", "arbitrary"),
        name="gqa_lat_attention",
    )(proj, proj, proj, cgk.reshape(DEC_BATCH, DEPTH, PAST_LEN, 128), cgv.reshape(DEC_BATCH, DEPTH, PAST_LEN, 128))
    o_dif = pl.pallas_call(
        functools.partial(_diff_lat_kernel, lam_init=lam_init),
        grid=(DEC_BATCH, nq_blocks),
        in_specs=[qspec("dq"), kvspec("dk", 256), kvspec("dv", 256), cspec(256), cspec(256),
                  _const_spec((4, DIFF_DIM)), _const_spec((1, 256)), _const_spec((256, 256))],
        out_specs=ospec, out_shape=oshape,
        compiler_params=_cparams("arbitrary", "arbitrary"),
        name="diff_lat_attention",
    )(proj, proj, proj, cdk.reshape(DEC_BATCH, DEPTH, PAST_LEN, 256), cdv.reshape(DEC_BATCH, DEPTH, PAST_LEN, 256),
      lp["diff_lambda"], lp["diff_sub_g"], lp["bd64"])
    nblk = DEC_SEQ // (NAT_QROWS * GRID_W)
    nat_q = NAT_QROWS * GRID_W
    o_nat = pl.pallas_call(
        _nat_lat_kernel,
        grid=(DEC_BATCH, nblk),
        in_specs=[pl.BlockSpec((nat_q, 256), lambda b, i: (b * nblk + i, _COL["nq"])),
                  kvspec("nk", 256), kvspec("nv", 256), cspec(256), cspec(256),
                  pl.BlockSpec((1, NAT_HEADS, nat_q, NAT_KROWS * GRID_W),
                               lambda b, i: (jnp.where(i == 0, 0, jnp.where(i == nblk - 1, 2, 1)), 0, 0, 0))],
        out_specs=pl.BlockSpec((nat_q, 256), lambda b, i: (b * nblk + i, 0)),
        out_shape=oshape,
        compiler_params=_cparams("arbitrary", "arbitrary"),
        name="nat_lat_attention",
    )(proj, proj, proj, cnk.reshape(DEC_BATCH, DEPTH, PAST_LEN, 256), cnv.reshape(DEC_BATCH, DEPTH, PAST_LEN, 256),
      lp["nat_bias"])
    return o_nat, o_gqa, o_dif


def _merge_kernel(x_ref, mod_ref, n1_ref, b0_ref, b1_ref, b2_ref, b3_ref, wb_ref, wg_ref, bg_ref, wo_ref, o_ref):
    x = x_ref[...]
    mod = mod_ref[0]
    hb = _rms_mod(x, n1_ref[...], mod[:, D_MODEL:2 * D_MODEL], mod[:, 0:D_MODEL]).astype(BF16)
    merged = None
    for n, b_ref in enumerate((b0_ref, b1_ref, b2_ref, b3_ref)):
        cols = slice(n * D_MODEL, (n + 1) * D_MODEL)
        gate = _sigmoid(jnp.dot(hb, wg_ref[:, cols], preferred_element_type=F32) + bg_ref[:, cols])
        term = gate * _dot(b_ref[...], wb_ref[n])
        merged = term if merged is None else merged + term
    out = _dot(merged, wo_ref[...])
    o_ref[...] = x + mod[:, 2 * D_MODEL:3 * D_MODEL] * out


def _merge(x, mod_l, lp, branches, latent):
    n_tok = x.shape[0]
    return pl.pallas_call(
        _merge_kernel,
        grid=(n_tok // TB,),
        in_specs=[pl.BlockSpec((TB, D_MODEL), lambda i: (i, 0)), _mod_spec(latent), _const_spec((1, D_MODEL))]
                 + [pl.BlockSpec((TB, BRANCH_W), functools.partial(lambda i, c: (i, c), c=col)) for _, col in branches]
                 + [_const_spec((N_BRANCH, BRANCH_W, D_MODEL)), _const_spec((D_MODEL, N_BRANCH * D_MODEL)),
                    _const_spec((1, N_BRANCH * D_MODEL)), _const_spec((D_MODEL, D_MODEL))],
        out_specs=pl.BlockSpec((TB, D_MODEL), lambda i: (i, 0)),
        out_shape=jax.ShapeDtypeStruct((n_tok, D_MODEL), F32),
        compiler_params=_cparams("arbitrary"),
        name="merge_lat" if latent else "merge_ctx",
    )(x, mod_l, lp["norm1_g"], *[a for a, _ in branches], lp["w_branch"], lp["w_gate"], lp["b_gate"], lp["w_out"])


def _top_desc(s, k, with_rank=False):
    rows = []
    cur = s
    rank = jnp.full(s.shape, float(k), F32) if with_rank else None
    for i in range(k):
        m = jnp.max(cur, axis=0, keepdims=True)
        rows.append(m)
        hit = cur == m
        if with_rank:
            rank = jnp.where(hit, float(i), rank)
        cur = jnp.where(hit, NEG, cur)
    top = jnp.concatenate(rows, axis=0)
    return (top, rank) if with_rank else top


def _peer_score_kernel(x_ref, mod_ref, n2_ref, wqt_ref, sk_ref, ht_ref, e1_ref, cnt_ref, rk_ref, e2_ref):
    mod = mod_ref[0]
    h2 = _rms_mod(x_ref[...], n2_ref[...], mod[:, 4 * D_MODEL:5 * D_MODEL], mod[:, 3 * D_MODEL:4 * D_MODEL])
    htb = h2.T.astype(BF16)
    ht_ref[...] = htb
    qt = jnp.dot(wqt_ref[...], htb, preferred_element_type=F32)
    k1 = PEER_TOPK + 1
    half = PEER_TOPK // 2
    row = lax.broadcasted_iota(jnp.int32, (k1, 1), 0)
    for h in range(PEER_HEADS):
        s1, s2 = (jnp.dot(sk_ref[p], qt[(2 * h + p) * PEER_KEYS:(2 * h + p + 1) * PEER_KEYS, :].astype(BF16),
                          preferred_element_type=F32) for p in range(2))
        t1 = _top_desc(s1, k1)
        t2, rk2 = _top_desc(s2, k1, with_rank=True)
        cands = [t1[0:1] + t2, jnp.where(row >= 1, t1 + t2[0:1], NEG)]
        for a in range(1, half):
            nb = k1 // (a + 1)
            cands.append(jnp.where((row[0:half] >= 1) & (row[0:half] < nb), t1[a:a + 1] + t2[0:half], NEG))
        best = _top_desc(jnp.concatenate(cands, axis=0), k1)
        z = jnp.sum(jnp.exp(best[0:PEER_TOPK] - best[0:1]), axis=0, keepdims=True)
        thr = 0.5 * (best[PEER_TOPK - 1:PEER_TOPK] + best[PEER_TOPK:k1])
        cnt = None
        for b in range(PEER_TOPK):
            ok = jnp.where((s1 + t2[b:b + 1]) >= thr, 1.0, 0.0)
            cnt = ok if cnt is None else cnt + ok
        e1_ref[h] = jnp.exp(s1 - t1[0:1]) * (0.5 / z)
        cnt_ref[h] = cnt
        rk_ref[h] = pltpu.bitcast(rk2.astype(BF16), jnp.int32)
        e2_ref[h] = pltpu.bitcast(jnp.exp(s2 - t2[0:1]).astype(BF16), jnp.int32)


def _peer_dense_kernel(ht_ref, e1_ref, cnt_ref, rk_ref, e2_ref, u_ref, v_ref, x_ref, mod_ref, o_ref,
                       acc_ref, a_ref, w_ref):
    e = pl.program_id(1)
    n_tok = acc_ref.shape[0]

    @pl.when(e == 0)
    def _():
        acc_ref[...] = jnp.zeros_like(acc_ref)

    a_ref[...] = jnp.dot(u_ref[...], ht_ref[...], preferred_element_type=F32)
    zero = jnp.zeros((), BF16)
    for rr in range(PEER_ET // PEER_KEYS):
        r = e * (PEER_ET // PEER_KEYS) + rr
        rows = slice(rr * PEER_KEYS, (rr + 1) * PEER_KEYS)
        def tile_row(ref, h):
            t16 = jnp.broadcast_to(ref[h, pl.ds(r, 1), :], (BF16_ROWS, n_tok)).astype(BF16)
            return jnp.concatenate([t16] * (PEER_KEYS // BF16_ROWS), axis=0)
        cnt_rows = [tile_row(cnt_ref, h) for h in range(PEER_HEADS)]
        e1_rows = [tile_row(e1_ref, h) for h in range(PEER_HEADS)]
        for lt in range(n_tok // 128):
            lanes = slice(lt * 128, (lt + 1) * 128)
            g = None
            for h in range(PEER_HEADS):
                sel = pltpu.bitcast(rk_ref[h, :, lanes], BF16) < cnt_rows[h][:, lanes]
                term = jnp.where(sel, pltpu.bitcast(e2_ref[h, :, lanes], BF16) * e1_rows[h][:, lanes], zero)
                g = term if g is None else g + term
            a = a_ref[rows, lanes]
            t = jnp.tanh(a * (0.7978845608028654 + (0.7978845608028654 * 0.044715) * (a * a)))
            w_ref[rr * (PEER_KEYS // 2):(rr + 1) * (PEER_KEYS // 2), lanes] = pltpu.bitcast(
                g * (a + a * t).astype(BF16), jnp.int32)
    acc_ref[...] += jnp.dot(pltpu.bitcast(w_ref[...], BF16).T, v_ref[...], preferred_element_type=F32)

    @pl.when(e == pl.num_programs(1) - 1)
    def _():
        o_ref[...] = x_ref[...] + mod_ref[0][:, 5 * D_MODEL:6 * D_MODEL] * acc_ref[...]


def _peer(x, mod_l, lp, latent):
    n_tok = x.shape[0]
    nb = n_tok // TBP
    if latent:
        mod_spec = pl.BlockSpec((1, 1, 6 * D_MODEL), lambda i, *_: (1 + i // (DEC_SEQ // TBP), 0, 0))
    else:
        mod_spec = pl.BlockSpec((1, 1, 6 * D_MODEL), lambda i, *_: (0, 0, 0))
    n_keys2 = PEER_HEADS * PEER_QDIM
    f32_tab = lambda n: (PEER_HEADS, PEER_KEYS, n)
    bf16_tab = lambda n: (PEER_HEADS, PEER_KEYS // 2, n)
    tab_specs = lambda ix: ([pl.BlockSpec(f32_tab(TBP), ix)] * 2 + [pl.BlockSpec(bf16_tab(TBP), ix)] * 2)
    ht, *tabs = pl.pallas_call(
        _peer_score_kernel,
        grid=(nb,),
        in_specs=[pl.BlockSpec((TBP, D_MODEL), lambda i: (i, 0)), mod_spec, _const_spec((1, D_MODEL)),
                  _const_spec((n_keys2, D_MODEL)), _const_spec((2, PEER_KEYS, PEER_QDIM // 2))],
        out_specs=[pl.BlockSpec((D_MODEL, TBP), lambda i: (0, i))]
                  + tab_specs(lambda i: (0, 0, i)),
        out_shape=[jax.ShapeDtypeStruct((D_MODEL, n_tok), BF16),
                   jax.ShapeDtypeStruct(f32_tab(n_tok), F32), jax.ShapeDtypeStruct(f32_tab(n_tok), F32),
                   jax.ShapeDtypeStruct(bf16_tab(n_tok), jnp.int32), jax.ShapeDtypeStruct(bf16_tab(n_tok), jnp.int32)],
        compiler_params=_cparams("arbitrary"),
        name="peer_scores_lat" if latent else "peer_scores_ctx",
    )(x, mod_l, lp["norm2_g"], lp["peer_wqt"], lp["peer_subkeys"])
    n_et = PEER_KEYS * PEER_KEYS // PEER_ET
    return pl.pallas_call(
        _peer_dense_kernel,
        grid=(nb, n_et),
        in_specs=[pl.BlockSpec((D_MODEL, TBP), lambda i, e: (0, i))]
                 + tab_specs(lambda i, e: (0, 0, i))
                 + [pl.BlockSpec((PEER_ET, D_MODEL), lambda i, e: (e, 0)),
                    pl.BlockSpec((PEER_ET, D_MODEL), lambda i, e: (e, 0)),
                    pl.BlockSpec((TBP, D_MODEL), lambda i, e: (i, 0)),
                    mod_spec],
        out_specs=pl.BlockSpec((TBP, D_MODEL), lambda i, e: (i, 0)),
        out_shape=jax.ShapeDtypeStruct((n_tok, D_MODEL), F32),
        scratch_shapes=[pltpu.VMEM((TBP, D_MODEL), F32), pltpu.VMEM((PEER_ET, TBP), F32),
                        pltpu.VMEM((PEER_ET // 2, TBP), jnp.int32)],
        compiler_params=_cparams("arbitrary", "arbitrary"),
        name="peer_dense_lat" if latent else "peer_dense_ctx",
    )(ht, *tabs, lp["peer_u"], lp["peer_v"], x, mod_l)


def _layer_params(l, w):
    tile = lambda g, n: jnp.tile(g, n)
    gains = jnp.stack([
        tile(w["nat_qk_g"][l, 0], 4), tile(w["nat_qk_g"][l, 1], 4),
        tile(w["gqa_qk_g"][l, 0], 4), tile(w["gqa_qk_g"][l, 1], 4),
        tile(w["diff_qk_g"][l, 0], 8), tile(w["diff_qk_g"][l, 1], 8),
        jnp.zeros((256,), F32), jnp.zeros((256,), F32)])
    return dict(
        norm1_g=w["norm1_g"][l].reshape(1, D_MODEL),
        norm2_g=w["norm2_g"][l].reshape(1, D_MODEL),
        w_in=w["w_in"][l].astype(BF16),
        qk_gains=gains,
        bd64=_block_ones(256, 64), bd32=_block_ones(256, 32), bd256=_block_ones(256, 256),
        rope=w["rope"],
        sgu_norm_g=w["sgu_norm_g"][l].reshape(1, SGU_WIDTH),
        sgu_w=w["sgu_w"][l].astype(BF16),
        sgu_b=jnp.repeat(w["sgu_b"][l].T, SGU_WIDTH // SGU_GROUPS, axis=1),
        diff_lambda=w["diff_lambda"][l],
        diff_sub_g=tile(w["diff_sub_g"][l], 4).reshape(1, 256),
        nat_bias=_nat_bias_table(w["nat_rpb"][l]),
        w_branch=w["w_branch"][l].astype(BF16),
        w_gate=w["w_gate"][l].astype(BF16),
        b_gate=w["b_gate"][l].reshape(1, N_BRANCH * D_MODEL),
        w_out=w["w_out"][l].astype(BF16),
        peer_wqt=w["peer_wq"][l].T.astype(BF16),
        peer_subkeys=w["peer_subkeys"][l].astype(BF16),
        peer_u=w["peer_u"][l].astype(BF16),
        peer_v=w["peer_v"][l].astype(BF16),
    )


def kernel(x_prompt, x_sample, c, cache_nat_k, cache_nat_v, cache_gqa_k, cache_gqa_v, cache_diff_k, cache_diff_v, c_ctx, w_mod, b_mod, norm1_g, norm2_g, w_in, nat_qk_g, nat_rpb, gqa_qk_g, diff_qk_g, diff_lambda, diff_sub_g, sgu_norm_g, sgu_w, sgu_b, w_branch, w_gate, b_gate, w_out, peer_wq, peer_subkeys, peer_u, peer_v):
    w = dict(norm1_g=norm1_g, norm2_g=norm2_g, w_in=w_in, nat_qk_g=nat_qk_g, nat_rpb=nat_rpb,
             gqa_qk_g=gqa_qk_g, diff_qk_g=diff_qk_g, diff_lambda=diff_lambda, diff_sub_g=diff_sub_g,
             sgu_norm_g=sgu_norm_g, sgu_w=sgu_w, sgu_b=sgu_b, w_branch=w_branch, w_gate=w_gate,
             b_gate=b_gate, w_out=w_out, peer_wq=peer_wq, peer_subkeys=peer_subkeys, peer_u=peer_u,
             peer_v=peer_v, rope=_rope_tables())
    cvec = jnp.concatenate([c_ctx[None], c, jnp.zeros((MOD_ROWS - 1 - DEC_BATCH, D_MODEL), F32)], axis=0)
    mod = _modulation(cvec, w_mod, b_mod).reshape(DEPTH, MOD_ROWS, 1, 6 * D_MODEL)
    xp = x_prompt.reshape(N_CTX, D_MODEL)
    xs = x_sample.reshape(N_LAT, D_MODEL)
    new = []
    for l in range(DEPTH):
        lp = _layer_params(l, w)
        lam_init = 0.8 - 0.6 * math.exp(-0.3 * l)
        caches = (cache_nat_k, cache_nat_v, cache_gqa_k, cache_gqa_v, cache_diff_k, cache_diff_v)
        proj = _in_projection(xp, mod[l], lp, latent=False)
        new.append(proj)
        attn = _ctx_attention(proj, lp, lam_init)
        xp = _merge(xp, mod[l], lp, [(attn, 0), (attn, 1), (attn, 2), (proj, _COL["sgu"])], latent=False)
        xp = _peer(xp, mod[l], lp, latent=False)
        proj = _in_projection(xs, mod[l], lp, latent=True)
        o_nat, o_gqa, o_dif = _lat_attention(proj, caches, lp, l, lam_init)
        xs = _merge(xs, mod[l], lp, [(o_nat, 0), (o_gqa, 0), (o_dif, 0), (proj, _COL["sgu"])], latent=True)
        xs = _peer(xs, mod[l], lp, latent=True)

    def cache_out(lo, width, tail):
        a = jnp.stack([p[:, lo:lo + width].reshape(BATCH, SEQ, width) for p in new], axis=1)
        return a.reshape((BATCH, DEPTH, SEQ) + tail)

    return (xp.reshape(BATCH, SEQ, D_MODEL), xs.reshape(DEC_BATCH, DEC_SEQ, D_MODEL),
            cache_out(256, 256, (NAT_HEADS, HEAD_DIM)), cache_out(512, 256, (NAT_HEADS, HEAD_DIM)),
            cache_out(1024, 128, (GQA_KV_HEADS, HEAD_DIM)), cache_out(1152, 128, (GQA_KV_HEADS, HEAD_DIM)),
            cache_out(1536, 256, (DIFF_HEADS, 2, DIFF_DIM)), cache_out(1792, 256, (DIFF_HEADS, HEAD_DIM)))
```

```python
import functools
import math

import numpy as np
import jax
import jax.numpy as jnp
from jax import lax
from jax.experimental import pallas as pl
from jax.experimental.pallas import tpu as pltpu

F32 = jnp.float32
BF16 = jnp.bfloat16

D_MODEL = 1024
BATCH = 16
SEQ = 256
DEPTH = 2
DEC_BATCH = 2
DEC_SEQ = 4096
PAST_LEN = 512
GRID_W = 64
HEAD_DIM = 64
NAT_HEADS = 4
NAT_ROWS = 8
NAT_COLS = 16
GQA_HEADS = 4
GQA_KV_HEADS = 2
DIFF_HEADS = 4
DIFF_DIM = 32
SGU_WIDTH = 256
SGU_GROUPS = 4
CHUNK = 128
N_BRANCH = 4
BRANCH_W = 256
IN_W = 2560
PEER_HEADS = 8
PEER_KEYS = 128
PEER_QDIM = 256
PEER_TOPK = 16
ROPE_BASE = 10000.0
EPS = 1e-6

N_CTX = BATCH * SEQ
N_LAT = DEC_BATCH * DEC_SEQ
MOD_ROWS = 8
TB = 256
TQ = 256
PROJ_W = 2304
NAT_QROWS = 4
NAT_KROWS = NAT_QROWS + NAT_ROWS
TBS = 512
TBP = 1024
PEER_ET = 512
PEER_ROWS = PEER_ET // PEER_KEYS
SUBLANES = 8
BF16_ROWS = 16
VMEM_LIMIT = 56 * 1024 * 1024
NEG = -1e30


def _cparams(*sem):
    return pltpu.CompilerParams(dimension_semantics=sem, vmem_limit_bytes=VMEM_LIMIT)


def _const_spec(shape):
    return pl.BlockSpec(shape, lambda *_: (0,) * len(shape))


def _gelu(x):
    return 0.5 * x * (1.0 + jnp.tanh(0.7978845608028654 * (x + 0.044715 * (x * x * x))))


def _sigmoid(x):
    return 1.0 / (1.0 + jnp.exp(-x))


def _dot(a, b):
    return jnp.dot(a.astype(BF16), b.astype(BF16), preferred_element_type=F32)


def _dot_nt(a, b):
    return lax.dot_general(a.astype(BF16), b.astype(BF16), (((1,), (1,)), ((), ())),
                           preferred_element_type=F32)


def _group_mean(y2, ones_bd):
    hi = y2.astype(BF16)
    lo = (y2 - hi.astype(F32)).astype(BF16)
    return (jnp.dot(hi, ones_bd, preferred_element_type=F32)
            + jnp.dot(lo, ones_bd, preferred_element_type=F32))


def _block_ones(width, group):
    idx = np.arange(width) // group
    return jnp.asarray((idx[:, None] == idx[None, :]).astype(np.float32) / group, dtype=BF16)


def _rms_mod(x, gain, scale, shift):
    xn = x * lax.rsqrt(jnp.mean(x * x, axis=-1, keepdims=True) + EPS) * gain
    return xn * (1.0 + scale) + shift


def _mod_spec(latent):
    if latent:
        return pl.BlockSpec((1, 1, 6 * D_MODEL), lambda i, *_: (1 + i // (DEC_SEQ // TB), 0, 0))
    return pl.BlockSpec((1, 1, 6 * D_MODEL), lambda i, *_: (0, 0, 0))


def _mod_kernel(c_ref, w_ref, b_ref, o_ref):
    c = c_ref[...]
    s = c * _sigmoid(c)
    o_ref[0] = _dot(s, w_ref[0]) + b_ref[0]


def _modulation(cvec, w_mod, b_mod):
    tn = 1536
    return pl.pallas_call(
        _mod_kernel,
        grid=(DEPTH, 6 * D_MODEL // tn),
        in_specs=[pl.BlockSpec((MOD_ROWS, D_MODEL), lambda l, j: (0, 0)),
                  pl.BlockSpec((1, D_MODEL, tn), lambda l, j: (l, 0, j)),
                  pl.BlockSpec((1, 1, tn), lambda l, j: (l, 0, j))],
        out_specs=pl.BlockSpec((1, MOD_ROWS, tn), lambda l, j: (l, 0, j)),
        out_shape=jax.ShapeDtypeStruct((DEPTH, MOD_ROWS, 6 * D_MODEL), F32),
        compiler_params=_cparams("arbitrary", "arbitrary"),
        name="modulation",
    )(cvec, w_mod, b_mod.reshape(DEPTH, 1, 6 * D_MODEL))


def _rope_tables():
    t = np.arange(DEC_SEQ)
    pos = (t // GRID_W, t % GRID_W)
    out = []
    for d in (HEAD_DIM, DIFF_DIM):
        half, quarter = d // 2, d // 4
        inv = ROPE_BASE ** (-np.arange(quarter, dtype=np.float32) * 2.0 / half)
        lane = np.arange(128) % d
        part, j = lane // half, lane % half
        ang = np.stack([pos[0][:, None] * inv[None, :], pos[1][:, None] * inv[None, :]], axis=1)
        a = ang[:, part, j % quarter].astype(np.float32)
        cos, sin = np.cos(a), np.sin(a)
        out += [jnp.asarray(cos, dtype=F32),
                jnp.asarray(np.where(j < quarter, -sin, 0.0), dtype=F32),
                jnp.asarray(np.where(j >= quarter, sin, 0.0), dtype=F32)]
    return out


def _rope(y, c, sa, sb, quarter):
    w = y.shape[-1]
    rep = w // 128
    if rep > 1:
        c, sa, sb = (jnp.concatenate([t] * rep, axis=-1) for t in (c, sa, sb))
    up = pltpu.roll(y, w - quarter, 1)
    dn = pltpu.roll(y, quarter, 1)
    return y * c + up * sa + dn * sb


def _inproj_kernel(*refs, latent):
    (x_ref, mod_ref, n1_ref, w_ref, g_ref, bd64_ref, bd32_ref, bd256_ref,
     sgn_ref, sgw_ref, sgb_ref) = refs[:11]
    o_ref = refs[-1]
    mod = mod_ref[0]
    h = _rms_mod(x_ref[...], n1_ref[...], mod[:, D_MODEL:2 * D_MODEL], mod[:, 0:D_MODEL])
    y = jnp.dot(h.astype(BF16), w_ref[...], preferred_element_type=F32)

    def qk_norm(lo, width, bd, gain_row):
        v = y[:, lo:lo + width]
        ms = _group_mean(v * v, bd)
        return v * lax.rsqrt(ms + EPS) * g_ref[gain_row:gain_row + 1, 0:width]

    bd64 = bd64_ref[...]
    bd32 = bd32_ref[...]
    if latent:
        c64, sa64, sb64, c32, sa32, sb32 = (r[...] for r in refs[11:17])
        rope64 = functools.partial(_rope, c=c64, sa=sa64, sb=sb64, quarter=HEAD_DIM // 4)
        rope32 = functools.partial(_rope, c=c32, sa=sa32, sb=sb32, quarter=DIFF_DIM // 4)
    else:
        rope64 = rope32 = lambda v: v
    o_ref[:, 0:256] = qk_norm(0, 256, bd64, 0)
    o_ref[:, 256:512] = qk_norm(256, 256, bd64, 1)
    o_ref[:, 512:768] = y[:, 512:768]
    o_ref[:, 768:1024] = rope64(qk_norm(768, 256, bd64, 2))
    o_ref[:, 1024:1152] = rope64(qk_norm(1024, 128, bd64[0:128, 0:128], 3))
    o_ref[:, 1152:1280] = y[:, 1152:1280]
    o_ref[:, 1280:1536] = rope32(qk_norm(1280, 256, bd32, 4))
    o_ref[:, 1536:1792] = rope32(qk_norm(1536, 256, bd32, 5))
    o_ref[:, 1792:2048] = y[:, 1792:2048]
    u = _gelu(y[:, 2048:2304])
    v = _gelu(y[:, 2304:2560])
    vn = v * lax.rsqrt(_group_mean(v * v, bd256_ref[...]) + EPS) * sgn_ref[...]
    vnb = vn.astype(BF16)
    lane_group = lax.broadcasted_iota(jnp.int32, (CHUNK, SGU_WIDTH), 1) // (SGU_WIDTH // SGU_GROUPS)
    for n in range(TB // CHUNK):
        vc = vnb[n * CHUNK:(n + 1) * CHUNK, :]
        s = sgb_ref[...]
        for g in range(SGU_GROUPS):
            sg = jnp.dot(sgw_ref[g], vc, preferred_element_type=F32)
            s = s + jnp.where(lane_group == g, sg, 0.0)
        o_ref[n * CHUNK:(n + 1) * CHUNK, 2048:2304] = u[n * CHUNK:(n + 1) * CHUNK, :] * s


def _in_projection(x, mod_l, lp, latent):
    n_tok = x.shape[0]
    in_specs = [pl.BlockSpec((TB, D_MODEL), lambda i: (i, 0)),
                _mod_spec(latent),
                _const_spec((1, D_MODEL)),
                _const_spec((D_MODEL, IN_W)),
                _const_spec((8, 256)),
                _const_spec((256, 256)), _const_spec((256, 256)), _const_spec((256, 256)),
                _const_spec((1, SGU_WIDTH)),
                _const_spec((SGU_GROUPS, CHUNK, CHUNK)),
                _const_spec((CHUNK, SGU_WIDTH))]
    args = [x, mod_l, lp["norm1_g"], lp["w_in"], lp["qk_gains"], lp["bd64"], lp["bd32"], lp["bd256"],
            lp["sgu_norm_g"], lp["sgu_w"], lp["sgu_b"]]
    if latent:
        in_specs += [pl.BlockSpec((TB, 128), lambda i: (i % (DEC_SEQ // TB), 0))] * 6
        args += lp["rope"]
    return pl.pallas_call(
        functools.partial(_inproj_kernel, latent=latent),
        grid=(n_tok // TB,),
        in_specs=in_specs,
        out_specs=pl.BlockSpec((TB, PROJ_W), lambda i: (i, 0)),
        out_shape=jax.ShapeDtypeStruct((n_tok, PROJ_W), F32),
        compiler_params=_cparams("arbitrary"),
        name="in_projection_lat" if latent else "in_projection_ctx",
    )(*args)


def _softmax_parts(scores):
    m = functools.reduce(jnp.maximum, [jnp.max(s, axis=-1, keepdims=True) for s in scores])
    ps = [jnp.exp(s - m) for s in scores]
    l = functools.reduce(jnp.add, [jnp.sum(p, axis=-1, keepdims=True) for p in ps])
    return ps, 1.0 / l


def _diff_lambda(lam_ref, lam_init):
    lv = lam_ref[...]
    a = jnp.sum(lv[0:1] * lv[1:2], axis=-1, keepdims=True)
    b = jnp.sum(lv[2:3] * lv[3:4], axis=-1, keepdims=True)
    return jnp.exp(a) - jnp.exp(b) + lam_init


def _head(ref_or_val, h, width=HEAD_DIM):
    return ref_or_val[:, h * width:(h + 1) * width]


def _mha(q, ks, vs, n_heads, kv_group, scale, biases=None):
    outs = []
    for h in range(n_heads):
        g = h // kv_group
        qh = _head(q, h) * scale
        scores = [_dot_nt(qh, _head(k, g)) for k in ks]
        if biases is not None:
            scores = [s if b is None else s + b[h] for s, b in zip(scores, biases)]
        ps, rl = _softmax_parts(scores)
        o = functools.reduce(jnp.add, [_dot(p, _head(v, g)) for p, v in zip(ps, vs)])
        outs.append(o * rl)
    return jnp.concatenate(outs, axis=-1)


def _diff_attn(q, ks, vs, lam, sub_gain, bd64, lam_init):
    scale = DIFF_DIM ** -0.5
    outs = []
    for h in range(DIFF_HEADS):
        pd = None
        for m in range(2):
            qm = _head(q, 2 * h + m, DIFF_DIM)
            scores = [_dot_nt(qm, _head(k, 2 * h + m, DIFF_DIM)) * scale for k in ks]
            ps, rl = _softmax_parts(scores)
            if m == 0:
                pd = [p * rl for p in ps]
            else:
                pd = [a - lam * (p * rl) for a, p in zip(pd, ps)]
        outs.append(functools.reduce(jnp.add, [_dot(p, _head(v, h)) for p, v in zip(pd, vs)]))
    o = jnp.concatenate(outs, axis=-1)
    ms = _group_mean(o * o, bd64)
    return o * lax.rsqrt(ms + EPS) * sub_gain * (1.0 - lam_init)


def _ctx_attn_kernel(nq, nk, nv, gq, gk, gv, dq, dk, dv, lam_ref, subg_ref, bd64_ref, o_ref, *, lam_init):
    scale = HEAD_DIM ** -0.5
    o_ref[:, 0:256] = _mha(nq[...], [nk[...]], [nv[...]], NAT_HEADS, 1, scale)
    o_ref[:, 256:512] = _mha(gq[...], [gk[...]], [gv[...]], GQA_HEADS, GQA_HEADS // GQA_KV_HEADS, scale)
    lam = _diff_lambda(lam_ref, lam_init)
    o_ref[:, 512:768] = _diff_attn(dq[...], [dk[...]], [dv[...]], lam, subg_ref[...], bd64_ref[...], lam_init)


_COL = dict(nq=0, nk=1, nv=2, gq=3, gk=8, gv=9, dq=5, dk=6, dv=7, sgu=8)


def _ctx_attention(proj, lp, lam_init):
    blk = lambda name, w: pl.BlockSpec((SEQ, w), lambda b: (b, _COL[name]))
    return pl.pallas_call(
        functools.partial(_ctx_attn_kernel, lam_init=lam_init),
        grid=(BATCH,),
        in_specs=[blk("nq", 256), blk("nk", 256), blk("nv", 256),
                  blk("gq", 256), blk("gk", 128), blk("gv", 128),
                  blk("dq", 256), blk("dk", 256), blk("dv", 256),
                  _const_spec((4, DIFF_DIM)), _const_spec((1, 256)), _const_spec((256, 256))],
        out_specs=pl.BlockSpec((SEQ, 768), lambda b: (b, 0)),
        out_shape=jax.ShapeDtypeStruct((N_CTX, 768), F32),
        compiler_params=_cparams("arbitrary"),
        name="ctx_attention",
    )(*([proj] * 9), lp["diff_lambda"], lp["diff_sub_g"], lp["bd64"])


def _gqa_lat_kernel(q_ref, k_ref, v_ref, ck_ref, cv_ref, o_ref):
    o_ref[...] = _mha(q_ref[...], [k_ref[...], ck_ref[...]], [v_ref[...], cv_ref[...]],
                      GQA_HEADS, GQA_HEADS // GQA_KV_HEADS, HEAD_DIM ** -0.5)


def _diff_lat_kernel(q_ref, k_ref, v_ref, ck_ref, cv_ref, lam_ref, subg_ref, bd64_ref, o_ref, *, lam_init):
    lam = _diff_lambda(lam_ref, lam_init)
    o_ref[...] = _diff_attn(q_ref[...], [k_ref[...], ck_ref[...]], [v_ref[...], cv_ref[...]],
                            lam, subg_ref[...], bd64_ref[...], lam_init)


def _nat_lat_kernel(q_ref, k_ref, v_ref, ck_ref, cv_ref, bias_ref, o_ref):
    i = pl.program_id(1)
    k_row0 = jnp.clip(NAT_QROWS * i - NAT_ROWS // 2, 0, GRID_W - NAT_KROWS)
    start = pl.multiple_of(k_row0 * GRID_W, GRID_W)
    kw = k_ref[pl.ds(start, NAT_KROWS * GRID_W), :]
    vw = v_ref[pl.ds(start, NAT_KROWS * GRID_W), :]
    o_ref[...] = _mha(q_ref[...], [kw, ck_ref[...]], [vw, cv_ref[...]], NAT_HEADS, 1, HEAD_DIM ** -0.5,
                      biases=[bias_ref[0], None])


def _nat_bias_table(rpb):
    rows = DEC_SEQ // GRID_W
    nblk = rows // NAT_QROWS
    pad = jnp.pad(rpb, ((0, 0), (0, 0), (GRID_W - NAT_COLS, GRID_W - NAT_COLS)))
    toep = jnp.stack([pad[:, :, GRID_W - 1 - c:2 * GRID_W - 1 - c] for c in range(GRID_W)], axis=2)
    col = np.arange(GRID_W)
    cs = np.clip(col - NAT_COLS // 2, 0, GRID_W - NAT_COLS)
    col_ok = (col[None, :] >= cs[:, None]) & (col[None, :] < cs[:, None] + NAT_COLS)
    toep = jnp.where(col_ok, toep, NEG)
    masked = jnp.full((NAT_HEADS, GRID_W, GRID_W), NEG, F32)
    cases = []
    for blk in (0, 1, nblk - 1):
        r0 = blk * NAT_QROWS
        k0 = int(np.clip(r0 - NAT_ROWS // 2, 0, rows - NAT_KROWS))
        q_rows = []
        for qr in range(r0, r0 + NAT_QROWS):
            rs = int(np.clip(qr - NAT_ROWS // 2, 0, rows - NAT_ROWS))
            q_rows.append(jnp.concatenate(
                [toep[:, kr - qr + NAT_ROWS - 1] if rs <= kr < rs + NAT_ROWS else masked
                 for kr in range(k0, k0 + NAT_KROWS)], axis=-1))
        cases.append(jnp.concatenate(q_rows, axis=-2))
    return jnp.stack(cases, axis=0)


def _lat_attention(proj, caches, lp, l, lam_init):
    cnk, cnv, cgk, cgv, cdk, cdv = caches
    nq_blocks = DEC_SEQ // TQ
    qspec = lambda name: pl.BlockSpec((TQ, 256), lambda b, i: (b * nq_blocks + i, _COL[name]))
    kvspec = lambda name, w: pl.BlockSpec((DEC_SEQ, w), lambda b, i: (b, _COL[name]))
    cspec = lambda w: pl.BlockSpec((None, None, PAST_LEN, w), lambda b, i: (b, l, 0, 0))
    ospec = pl.BlockSpec((TQ, 256), lambda b, i: (b * nq_blocks + i, 0))
    oshape = jax.ShapeDtypeStruct((N_LAT, 256), F32)
    o_gqa = pl.pallas_call(
        _gqa_lat_kernel,
        grid=(DEC_BATCH, nq_blocks),
        in_specs=[qspec("gq"), kvspec("gk", 128), kvspec("gv", 128), cspec(128), cspec(128)],
        out_specs=ospec, out_shape=oshape,
        compiler_params=_cparams("arbitrary", "arbitrary"),
        name="gqa_lat_attention",
    )(proj, proj, proj, cgk.reshape(DEC_BATCH, DEPTH, PAST_LEN, 128), cgv.reshape(DEC_BATCH, DEPTH, PAST_LEN, 128))
    o_dif = pl.pallas_call(
        functools.partial(_diff_lat_kernel, lam_init=lam_init),
        grid=(DEC_BATCH, nq_blocks),
        in_specs=[qspec("dq"), kvspec("dk", 256), kvspec("dv", 256), cspec(256), cspec(256),
                  _const_spec((4, DIFF_DIM)), _const_spec((1, 256)), _const_spec((256, 256))],
        out_specs=ospec, out_shape=oshape,
        compiler_params=_cparams("arbitrary", "arbitrary"),
        name="diff_lat_attention",
    )(proj, proj, proj, cdk.reshape(DEC_BATCH, DEPTH, PAST_LEN, 256), cdv.reshape(DEC_BATCH, DEPTH, PAST_LEN, 256),
      lp["diff_lambda"], lp["diff_sub_g"], lp["bd64"])
    nblk = DEC_SEQ // (NAT_QROWS * GRID_W)
    nat_q = NAT_QROWS * GRID_W
    o_nat = pl.pallas_call(
        _nat_lat_kernel,
        grid=(DEC_BATCH, nblk),
        in_specs=[pl.BlockSpec((nat_q, 256), lambda b, i: (b * nblk + i, _COL["nq"])),
                  kvspec("nk", 256), kvspec("nv", 256), cspec(256), cspec(256),
                  pl.BlockSpec((1, NAT_HEADS, nat_q, NAT_KROWS * GRID_W),
                               lambda b, i: (jnp.where(i == 0, 0, jnp.where(i == nblk - 1, 2, 1)), 0, 0, 0))],
        out_specs=pl.BlockSpec((nat_q, 256), lambda b, i: (b * nblk + i, 0)),
        out_shape=oshape,
        compiler_params=_cparams("arbitrary", "arbitrary"),
        name="nat_lat_attention",
    )(proj, proj, proj, cnk.reshape(DEC_BATCH, DEPTH, PAST_LEN, 256), cnv.reshape(DEC_BATCH, DEPTH, PAST_LEN, 256),
      lp["nat_bias"])
    return o_nat, o_gqa, o_dif


def _merge_kernel(x_ref, mod_ref, n1_ref, b0_ref, b1_ref, b2_ref, b3_ref, wb_ref, wg_ref, bg_ref, wo_ref, o_ref):
    x = x_ref[...]
    mod = mod_ref[0]
    hb = _rms_mod(x, n1_ref[...], mod[:, D_MODEL:2 * D_MODEL], mod[:, 0:D_MODEL]).astype(BF16)
    merged = None
    for n, b_ref in enumerate((b0_ref, b1_ref, b2_ref, b3_ref)):
        cols = slice(n * D_MODEL, (n + 1) * D_MODEL)
        gate = _sigmoid(jnp.dot(hb, wg_ref[:, cols], preferred_element_type=F32) + bg_ref[:, cols])
        term = gate * _dot(b_ref[...], wb_ref[n])
        merged = term if merged is None else merged + term
    out = _dot(merged, wo_ref[...])
    o_ref[...] = x + mod[:, 2 * D_MODEL:3 * D_MODEL] * out


def _merge(x, mod_l, lp, branches, latent):
    n_tok = x.shape[0]
    return pl.pallas_call(
        _merge_kernel,
        grid=(n_tok // TB,),
        in_specs=[pl.BlockSpec((TB, D_MODEL), lambda i: (i, 0)), _mod_spec(latent), _const_spec((1, D_MODEL))]
                 + [pl.BlockSpec((TB, BRANCH_W), functools.partial(lambda i, c: (i, c), c=col)) for _, col in branches]
                 + [_const_spec((N_BRANCH, BRANCH_W, D_MODEL)), _const_spec((D_MODEL, N_BRANCH * D_MODEL)),
                    _const_spec((1, N_BRANCH * D_MODEL)), _const_spec((D_MODEL, D_MODEL))],
        out_specs=pl.BlockSpec((TB, D_MODEL), lambda i: (i, 0)),
        out_shape=jax.ShapeDtypeStruct((n_tok, D_MODEL), F32),
        compiler_params=_cparams("arbitrary"),
        name="merge_lat" if latent else "merge_ctx",
    )(x, mod_l, lp["norm1_g"], *[a for a, _ in branches], lp["w_branch"], lp["w_gate"], lp["b_gate"], lp["w_out"])


def _top_desc(s, k, with_rank=False):
    rows = []
    cur = s
    rank = jnp.full(s.shape, float(k), F32) if with_rank else None
    for i in range(k):
        m = jnp.max(cur, axis=0, keepdims=True)
        rows.append(m)
        hit = cur == m
        if with_rank:
            rank = jnp.where(hit, float(i), rank)
        cur = jnp.where(hit, NEG, cur)
    top = jnp.concatenate(rows, axis=0)
    return (top, rank) if with_rank else top


def _peer_score_kernel(x_ref, mod_ref, n2_ref, wqt_ref, sk_ref, ht_ref, e1_ref, cnt_ref, rk_ref, e2_ref):
    mod = mod_ref[0]
    h2 = _rms_mod(x_ref[...], n2_ref[...], mod[:, 4 * D_MODEL:5 * D_MODEL], mod[:, 3 * D_MODEL:4 * D_MODEL])
    htb = h2.T.astype(BF16)
    ht_ref[...] = htb
    qt = jnp.dot(wqt_ref[...], htb, preferred_element_type=F32)
    k1 = PEER_TOPK + 1
    half = PEER_TOPK // 2
    row = lax.broadcasted_iota(jnp.int32, (k1, 1), 0)
    for h in range(PEER_HEADS):
        s1, s2 = (jnp.dot(sk_ref[p], qt[(2 * h + p) * PEER_KEYS:(2 * h + p + 1) * PEER_KEYS, :].astype(BF16),
                          preferred_element_type=F32) for p in range(2))
        t1 = _top_desc(s1, k1)
        t2, rk2 = _top_desc(s2, k1, with_rank=True)
        cands = [t1[0:1] + t2, jnp.where(row >= 1, t1 + t2[0:1], NEG)]
        for a in range(1, half):
            nb = k1 // (a + 1)
            cands.append(jnp.where((row[0:half] >= 1) & (row[0:half] < nb), t1[a:a + 1] + t2[0:half], NEG))
        best = _top_desc(jnp.concatenate(cands, axis=0), k1)
        z = jnp.sum(jnp.exp(best[0:PEER_TOPK] - best[0:1]), axis=0, keepdims=True)
        thr = 0.5 * (best[PEER_TOPK - 1:PEER_TOPK] + best[PEER_TOPK:k1])
        need = thr - s1
        cnt = None
        for b in range(PEER_TOPK):
            ok = jnp.where(t2[b:b + 1] >= need, 1.0, 0.0)
            cnt = ok if cnt is None else cnt + ok
        e1_ref[h] = jnp.exp(s1 - t1[0:1]) * (0.5 / z)
        cnt_ref[h] = cnt
        rk_ref[h] = pltpu.bitcast(rk2.astype(BF16), jnp.int32)
        e2_ref[h] = pltpu.bitcast(jnp.exp(s2 - t2[0:1]).astype(BF16), jnp.int32)


def _peer_dense_kernel(ht_ref, e1_ref, cnt_ref, rk_ref, e2_ref, u_ref, v_ref, x_ref, mod_ref, o_ref,
                       acc_ref, a_ref, w_ref):
    e = pl.program_id(1)
    n_tok = acc_ref.shape[0]

    @pl.when(e == 0)
    def _():
        acc_ref[...] = jnp.zeros_like(acc_ref)

    a_ref[...] = jnp.dot(u_ref[...], ht_ref[...], preferred_element_type=F32)
    zero = jnp.zeros((), BF16)
    for rr in range(PEER_ROWS):
        r = (e % (SUBLANES // PEER_ROWS)) * PEER_ROWS + rr
        rows = slice(rr * PEER_KEYS, (rr + 1) * PEER_KEYS)
        def tile_row(ref, h):
            t16 = jnp.broadcast_to(ref[h, pl.ds(r, 1), :], (BF16_ROWS, n_tok)).astype(BF16)
            return jnp.concatenate([t16] * (PEER_KEYS // BF16_ROWS), axis=0)
        cnt_rows = [tile_row(cnt_ref, h) for h in range(PEER_HEADS)]
        e1_rows = [tile_row(e1_ref, h) for h in range(PEER_HEADS)]
        for lt in range(n_tok // 128):
            lanes = slice(lt * 128, (lt + 1) * 128)
            g = None
            for h in range(PEER_HEADS):
                sel = pltpu.bitcast(rk_ref[h, :, lanes], BF16) < cnt_rows[h][:, lanes]
                term = jnp.where(sel, pltpu.bitcast(e2_ref[h, :, lanes], BF16) * e1_rows[h][:, lanes], zero)
                g = term if g is None else g + term
            a = a_ref[rows, lanes]
            t = jnp.tanh(a * (0.7978845608028654 + (0.7978845608028654 * 0.044715) * (a * a)))
            w_ref[rr * (PEER_KEYS // 2):(rr + 1) * (PEER_KEYS // 2), lanes] = pltpu.bitcast(
                g * (a + a * t).astype(BF16), jnp.int32)
    acc_ref[...] += jnp.dot(pltpu.bitcast(w_ref[...], BF16).T, v_ref[...], preferred_element_type=F32)

    @pl.when(e == pl.num_programs(1) - 1)
    def _():
        o_ref[...] = x_ref[...] + mod_ref[0][:, 5 * D_MODEL:6 * D_MODEL] * acc_ref[...]


def _peer(x, mod_l, lp, latent):
    n_tok = x.shape[0]

    def mod_spec(tb):
        if latent:
            return pl.BlockSpec((1, 1, 6 * D_MODEL), lambda i, *_: (1 + i // (DEC_SEQ // tb), 0, 0))
        return pl.BlockSpec((1, 1, 6 * D_MODEL), lambda i, *_: (0, 0, 0))

    n_keys2 = PEER_HEADS * PEER_QDIM
    f32_tab = lambda n: (PEER_HEADS, PEER_KEYS, n)
    bf16_tab = lambda n: (PEER_HEADS, PEER_KEYS // 2, n)
    ht, *tabs = pl.pallas_call(
        _peer_score_kernel,
        grid=(n_tok // TBS,),
        in_specs=[pl.BlockSpec((TBS, D_MODEL), lambda i: (i, 0)), mod_spec(TBS), _const_spec((1, D_MODEL)),
                  _const_spec((n_keys2, D_MODEL)), _const_spec((2, PEER_KEYS, PEER_QDIM // 2))],
        out_specs=[pl.BlockSpec((D_MODEL, TBS), lambda i: (0, i))]
                  + [pl.BlockSpec(f32_tab(TBS), lambda i: (0, 0, i))] * 2
                  + [pl.BlockSpec(bf16_tab(TBS), lambda i: (0, 0, i))] * 2,
        out_shape=[jax.ShapeDtypeStruct((D_MODEL, n_tok), BF16),
                   jax.ShapeDtypeStruct(f32_tab(n_tok), F32), jax.ShapeDtypeStruct(f32_tab(n_tok), F32),
                   jax.ShapeDtypeStruct(bf16_tab(n_tok), jnp.int32), jax.ShapeDtypeStruct(bf16_tab(n_tok), jnp.int32)],
        compiler_params=_cparams("arbitrary"),
        name="peer_scores_lat" if latent else "peer_scores_ctx",
    )(x, mod_l, lp["norm2_g"], lp["peer_wqt"], lp["peer_subkeys"])
    nb = n_tok // TBP
    n_et = PEER_KEYS * PEER_KEYS // PEER_ET
    row_spec = pl.BlockSpec((PEER_HEADS, SUBLANES, TBP), lambda i, e: (0, e // (SUBLANES // PEER_ROWS), i))
    return pl.pallas_call(
        _peer_dense_kernel,
        grid=(nb, n_et),
        in_specs=[pl.BlockSpec((D_MODEL, TBP), lambda i, e: (0, i)), row_spec, row_spec]
                 + [pl.BlockSpec(bf16_tab(TBP), lambda i, e: (0, 0, i))] * 2
                 + [pl.BlockSpec((PEER_ET, D_MODEL), lambda i, e: (e, 0)),
                    pl.BlockSpec((PEER_ET, D_MODEL), lambda i, e: (e, 0)),
                    pl.BlockSpec((TBP, D_MODEL), lambda i, e: (i, 0)),
                    mod_spec(TBP)],
        out_specs=pl.BlockSpec((TBP, D_MODEL), lambda i, e: (i, 0)),
        out_shape=jax.ShapeDtypeStruct((n_tok, D_MODEL), F32),
        scratch_shapes=[pltpu.VMEM((TBP, D_MODEL), F32), pltpu.VMEM((PEER_ET, TBP), F32),
                        pltpu.VMEM((PEER_ET // 2, TBP), jnp.int32)],
        compiler_params=_cparams("arbitrary", "arbitrary"),
        name="peer_dense_lat" if latent else "peer_dense_ctx",
    )(ht, *tabs, lp["peer_u"], lp["peer_v"], x, mod_l)


def _layer_params(l, w):
    tile = lambda g, n: jnp.tile(g, n)
    gains = jnp.stack([
        tile(w["nat_qk_g"][l, 0], 4), tile(w["nat_qk_g"][l, 1], 4),
        tile(w["gqa_qk_g"][l, 0], 4), tile(w["gqa_qk_g"][l, 1], 4),
        tile(w["diff_qk_g"][l, 0], 8), tile(w["diff_qk_g"][l, 1], 8),
        jnp.zeros((256,), F32), jnp.zeros((256,), F32)])
    return dict(
        norm1_g=w["norm1_g"][l].reshape(1, D_MODEL),
        norm2_g=w["norm2_g"][l].reshape(1, D_MODEL),
        w_in=w["w_in"][l].astype(BF16),
        qk_gains=gains,
        bd64=_block_ones(256, 64), bd32=_block_ones(256, 32), bd256=_block_ones(256, 256),
        rope=w["rope"],
        sgu_norm_g=w["sgu_norm_g"][l].reshape(1, SGU_WIDTH),
        sgu_w=w["sgu_w"][l].astype(BF16),
        sgu_b=jnp.repeat(w["sgu_b"][l].T, SGU_WIDTH // SGU_GROUPS, axis=1),
        diff_lambda=w["diff_lambda"][l],
        diff_sub_g=tile(w["diff_sub_g"][l], 4).reshape(1, 256),
        nat_bias=_nat_bias_table(w["nat_rpb"][l]),
        w_branch=w["w_branch"][l].astype(BF16),
        w_gate=w["w_gate"][l].astype(BF16),
        b_gate=w["b_gate"][l].reshape(1, N_BRANCH * D_MODEL),
        w_out=w["w_out"][l].astype(BF16),
        peer_wqt=w["peer_wq"][l].T.astype(BF16),
        peer_subkeys=w["peer_subkeys"][l].astype(BF16),
        peer_u=w["peer_u"][l].astype(BF16),
        peer_v=w["peer_v"][l].astype(BF16),
    )


def kernel(x_prompt, x_sample, c, cache_nat_k, cache_nat_v, cache_gqa_k, cache_gqa_v, cache_diff_k, cache_diff_v, c_ctx, w_mod, b_mod, norm1_g, norm2_g, w_in, nat_qk_g, nat_rpb, gqa_qk_g, diff_qk_g, diff_lambda, diff_sub_g, sgu_norm_g, sgu_w, sgu_b, w_branch, w_gate, b_gate, w_out, peer_wq, peer_subkeys, peer_u, peer_v):
    w = dict(norm1_g=norm1_g, norm2_g=norm2_g, w_in=w_in, nat_qk_g=nat_qk_g, nat_rpb=nat_rpb,
             gqa_qk_g=gqa_qk_g, diff_qk_g=diff_qk_g, diff_lambda=diff_lambda, diff_sub_g=diff_sub_g,
             sgu_norm_g=sgu_norm_g, sgu_w=sgu_w, sgu_b=sgu_b, w_branch=w_branch, w_gate=w_gate,
             b_gate=b_gate, w_out=w_out, peer_wq=peer_wq, peer_subkeys=peer_subkeys, peer_u=peer_u,
             peer_v=peer_v, rope=_rope_tables())
    cvec = jnp.concatenate([c_ctx[None], c, jnp.zeros((MOD_ROWS - 1 - DEC_BATCH, D_MODEL), F32)], axis=0)
    mod = _modulation(cvec, w_mod, b_mod).reshape(DEPTH, MOD_ROWS, 1, 6 * D_MODEL)
    xp = x_prompt.reshape(N_CTX, D_MODEL)
    xs = x_sample.reshape(N_LAT, D_MODEL)
    new = []
    for l in range(DEPTH):
        lp = _layer_params(l, w)
        lam_init = 0.8 - 0.6 * math.exp(-0.3 * l)
        caches = (cache_nat_k, cache_nat_v, cache_gqa_k, cache_gqa_v, cache_diff_k, cache_diff_v)
        proj = _in_projection(xp, mod[l], lp, latent=False)
        new.append(proj)
        attn = _ctx_attention(proj, lp, lam_init)
        xp = _merge(xp, mod[l], lp, [(attn, 0), (attn, 1), (attn, 2), (proj, _COL["sgu"])], latent=False)
        xp = _peer(xp, mod[l], lp, latent=False)
        proj = _in_projection(xs, mod[l], lp, latent=True)
        o_nat, o_gqa, o_dif = _lat_attention(proj, caches, lp, l, lam_init)
        xs = _merge(xs, mod[l], lp, [(o_nat, 0), (o_gqa, 0), (o_dif, 0), (proj, _COL["sgu"])], latent=True)
        xs = _peer(xs, mod[l], lp, latent=True)

    def cache_out(lo, width, tail):
        a = jnp.stack([p[:, lo:lo + width].reshape(BATCH, SEQ, width) for p in new], axis=1)
        return a.reshape((BATCH, DEPTH, SEQ) + tail)

    return (xp.reshape(BATCH, SEQ, D_MODEL), xs.reshape(DEC_BATCH, DEC_SEQ, D_MODEL),
            cache_out(256, 256, (NAT_HEADS, HEAD_DIM)), cache_out(512, 256, (NAT_HEADS, HEAD_DIM)),
            cache_out(1024, 128, (GQA_KV_HEADS, HEAD_DIM)), cache_out(1152, 128, (GQA_KV_HEADS, HEAD_DIM)),
            cache_out(1536, 256, (DIFF_HEADS, 2, DIFF_DIM)), cache_out(1792, 256, (DIFF_HEADS, HEAD_DIM)))
```

```python
import functools
import math

import numpy as np
import jax
import jax.numpy as jnp
from jax import lax
from jax.experimental import pallas as pl
from jax.experimental.pallas import tpu as pltpu

F32 = jnp.float32
BF16 = jnp.bfloat16

D_MODEL = 1024
BATCH = 16
SEQ = 256
DEPTH = 2
DEC_BATCH = 2
DEC_SEQ = 4096
PAST_LEN = 512
GRID_W = 64
HEAD_DIM = 64
NAT_HEADS = 4
NAT_ROWS = 8
NAT_COLS = 16
GQA_HEADS = 4
GQA_KV_HEADS = 2
DIFF_HEADS = 4
DIFF_DIM = 32
SGU_WIDTH = 256
SGU_GROUPS = 4
CHUNK = 128
N_BRANCH = 4
BRANCH_W = 256
IN_W = 2560
PEER_HEADS = 8
PEER_KEYS = 128
PEER_QDIM = 256
PEER_TOPK = 16
ROPE_BASE = 10000.0
EPS = 1e-6

N_CTX = BATCH * SEQ
N_LAT = DEC_BATCH * DEC_SEQ
MOD_ROWS = 8
TB = 256
TQ = 256
PROJ_W = 2304
NAT_QROWS = 4
NAT_KROWS = NAT_QROWS + NAT_ROWS
TBS = 512
TBP = 1024
PEER_ET = 512
PEER_ROWS = PEER_ET // PEER_KEYS
SUBLANES = 8
BF16_ROWS = 16
VMEM_LIMIT = 56 * 1024 * 1024
NEG = -1e30


def _cparams(*sem):
    return pltpu.CompilerParams(dimension_semantics=sem, vmem_limit_bytes=VMEM_LIMIT)


def _const_spec(shape):
    return pl.BlockSpec(shape, lambda *_: (0,) * len(shape))


def _gelu(x):
    return 0.5 * x * (1.0 + jnp.tanh(0.7978845608028654 * (x + 0.044715 * (x * x * x))))


def _sigmoid(x):
    return 1.0 / (1.0 + jnp.exp(-x))


def _dot(a, b):
    return jnp.dot(a.astype(BF16), b.astype(BF16), preferred_element_type=F32)


def _dot_nt(a, b):
    return lax.dot_general(a.astype(BF16), b.astype(BF16), (((1,), (1,)), ((), ())),
                           preferred_element_type=F32)


def _group_mean(y2, ones_bd):
    hi = y2.astype(BF16)
    lo = (y2 - hi.astype(F32)).astype(BF16)
    return (jnp.dot(hi, ones_bd, preferred_element_type=F32)
            + jnp.dot(lo, ones_bd, preferred_element_type=F32))


def _block_ones(width, group):
    idx = np.arange(width) // group
    return jnp.asarray((idx[:, None] == idx[None, :]).astype(np.float32) / group, dtype=BF16)


def _rms_mod(x, gain, scale, shift):
    xn = x * lax.rsqrt(jnp.mean(x * x, axis=-1, keepdims=True) + EPS) * gain
    return xn * (1.0 + scale) + shift


def _mod_spec(latent):
    if latent:
        return pl.BlockSpec((1, 1, 6 * D_MODEL), lambda i, *_: (1 + i // (DEC_SEQ // TB), 0, 0))
    return pl.BlockSpec((1, 1, 6 * D_MODEL), lambda i, *_: (0, 0, 0))


def _mod_kernel(c_ref, w_ref, b_ref, o_ref):
    c = c_ref[...]
    s = c * _sigmoid(c)
    o_ref[0] = _dot(s, w_ref[0]) + b_ref[0]


def _modulation(cvec, w_mod, b_mod):
    tn = 1536
    return pl.pallas_call(
        _mod_kernel,
        grid=(DEPTH, 6 * D_MODEL // tn),
        in_specs=[pl.BlockSpec((MOD_ROWS, D_MODEL), lambda l, j: (0, 0)),
                  pl.BlockSpec((1, D_MODEL, tn), lambda l, j: (l, 0, j)),
                  pl.BlockSpec((1, 1, tn), lambda l, j: (l, 0, j))],
        out_specs=pl.BlockSpec((1, MOD_ROWS, tn), lambda l, j: (l, 0, j)),
        out_shape=jax.ShapeDtypeStruct((DEPTH, MOD_ROWS, 6 * D_MODEL), F32),
        compiler_params=_cparams("arbitrary", "arbitrary"),
        name="modulation",
    )(cvec, w_mod, b_mod.reshape(DEPTH, 1, 6 * D_MODEL))


def _rope_tables():
    t = np.arange(DEC_SEQ)
    pos = (t // GRID_W, t % GRID_W)
    out = []
    for d in (HEAD_DIM, DIFF_DIM):
        half, quarter = d // 2, d // 4
        inv = ROPE_BASE ** (-np.arange(quarter, dtype=np.float32) * 2.0 / half)
        lane = np.arange(128) % d
        part, j = lane // half, lane % half
        ang = np.stack([pos[0][:, None] * inv[None, :], pos[1][:, None] * inv[None, :]], axis=1)
        a = ang[:, part, j % quarter].astype(np.float32)
        cos, sin = np.cos(a), np.sin(a)
        out += [jnp.asarray(cos, dtype=F32),
                jnp.asarray(np.where(j < quarter, -sin, 0.0), dtype=F32),
                jnp.asarray(np.where(j >= quarter, sin, 0.0), dtype=F32)]
    return out


def _rope(y, c, sa, sb, quarter):
    w = y.shape[-1]
    rep = w // 128
    if rep > 1:
        c, sa, sb = (jnp.concatenate([t] * rep, axis=-1) for t in (c, sa, sb))
    up = pltpu.roll(y, w - quarter, 1)
    dn = pltpu.roll(y, quarter, 1)
    return y * c + up * sa + dn * sb


def _inproj_kernel(*refs, latent):
    (x_ref, mod_ref, n1_ref, w_ref, g_ref, bd64_ref, bd32_ref, bd256_ref,
     sgn_ref, sgw_ref, sgb_ref) = refs[:11]
    o_ref = refs[-1]
    mod = mod_ref[0]
    h = _rms_mod(x_ref[...], n1_ref[...], mod[:, D_MODEL:2 * D_MODEL], mod[:, 0:D_MODEL])
    y = jnp.dot(h.astype(BF16), w_ref[...], preferred_element_type=F32)

    def qk_norm(lo, width, bd, gain_row):
        v = y[:, lo:lo + width]
        ms = _group_mean(v * v, bd)
        return v * lax.rsqrt(ms + EPS) * g_ref[gain_row:gain_row + 1, 0:width]

    bd64 = bd64_ref[...]
    bd32 = bd32_ref[...]
    if latent:
        c64, sa64, sb64, c32, sa32, sb32 = (r[...] for r in refs[11:17])
        rope64 = functools.partial(_rope, c=c64, sa=sa64, sb=sb64, quarter=HEAD_DIM // 4)
        rope32 = functools.partial(_rope, c=c32, sa=sa32, sb=sb32, quarter=DIFF_DIM // 4)
    else:
        rope64 = rope32 = lambda v: v
    o_ref[:, 0:256] = qk_norm(0, 256, bd64, 0)
    o_ref[:, 256:512] = qk_norm(256, 256, bd64, 1)
    o_ref[:, 512:768] = y[:, 512:768]
    o_ref[:, 768:1024] = rope64(qk_norm(768, 256, bd64, 2))
    o_ref[:, 1024:1152] = rope64(qk_norm(1024, 128, bd64[0:128, 0:128], 3))
    o_ref[:, 1152:1280] = y[:, 1152:1280]
    o_ref[:, 1280:1536] = rope32(qk_norm(1280, 256, bd32, 4))
    o_ref[:, 1536:1792] = rope32(qk_norm(1536, 256, bd32, 5))
    o_ref[:, 1792:2048] = y[:, 1792:2048]
    u = _gelu(y[:, 2048:2304])
    v = _gelu(y[:, 2304:2560])
    vn = v * lax.rsqrt(_group_mean(v * v, bd256_ref[...]) + EPS) * sgn_ref[...]
    vnb = vn.astype(BF16)
    lane_group = lax.broadcasted_iota(jnp.int32, (CHUNK, SGU_WIDTH), 1) // (SGU_WIDTH // SGU_GROUPS)
    for n in range(TB // CHUNK):
        vc = vnb[n * CHUNK:(n + 1) * CHUNK, :]
        s = sgb_ref[...]
        for g in range(SGU_GROUPS):
            sg = jnp.dot(sgw_ref[g], vc, preferred_element_type=F32)
            s = s + jnp.where(lane_group == g, sg, 0.0)
        o_ref[n * CHUNK:(n + 1) * CHUNK, 2048:2304] = u[n * CHUNK:(n + 1) * CHUNK, :] * s


def _in_projection(x, mod_l, lp, latent):
    n_tok = x.shape[0]
    in_specs = [pl.BlockSpec((TB, D_MODEL), lambda i: (i, 0)),
                _mod_spec(latent),
                _const_spec((1, D_MODEL)),
                _const_spec((D_MODEL, IN_W)),
                _const_spec((8, 256)),
                _const_spec((256, 256)), _const_spec((256, 256)), _const_spec((256, 256)),
                _const_spec((1, SGU_WIDTH)),
                _const_spec((SGU_GROUPS, CHUNK, CHUNK)),
                _const_spec((CHUNK, SGU_WIDTH))]
    args = [x, mod_l, lp["norm1_g"], lp["w_in"], lp["qk_gains"], lp["bd64"], lp["bd32"], lp["bd256"],
            lp["sgu_norm_g"], lp["sgu_w"], lp["sgu_b"]]
    if latent:
        in_specs += [pl.BlockSpec((TB, 128), lambda i: (i % (DEC_SEQ // TB), 0))] * 6
        args += lp["rope"]
    return pl.pallas_call(
        functools.partial(_inproj_kernel, latent=latent),
        grid=(n_tok // TB,),
        in_specs=in_specs,
        out_specs=pl.BlockSpec((TB, PROJ_W), lambda i: (i, 0)),
        out_shape=jax.ShapeDtypeStruct((n_tok, PROJ_W), F32),
        compiler_params=_cparams("arbitrary"),
        name="in_projection_lat" if latent else "in_projection_ctx",
    )(*args)


LOG2E = 1.4426950408889634


def _softmax_parts(scores):
    m = functools.reduce(jnp.maximum, [jnp.max(s, axis=-1, keepdims=True) for s in scores])
    ps = [jnp.exp2(s - m) for s in scores]
    l = functools.reduce(jnp.add, [jnp.sum(p, axis=-1, keepdims=True) for p in ps])
    return [p.astype(BF16) for p in ps], 1.0 / l


def _diff_lambda(lam_ref, lam_init):
    lv = lam_ref[...]
    a = jnp.sum(lv[0:1] * lv[1:2], axis=-1, keepdims=True)
    b = jnp.sum(lv[2:3] * lv[3:4], axis=-1, keepdims=True)
    return jnp.exp(a) - jnp.exp(b) + lam_init


def _head(ref_or_val, h, width=HEAD_DIM):
    return ref_or_val[:, h * width:(h + 1) * width]


def _mha(q, ks, vs, n_heads, kv_group, scale, biases=None):
    outs = []
    ks, vs = [k.astype(BF16) for k in ks], [v.astype(BF16) for v in vs]
    qs = (q * (scale * LOG2E)).astype(BF16)
    for h in range(n_heads):
        g = h // kv_group
        scores = [_dot_nt(_head(qs, h), _head(k, g)) for k in ks]
        if biases is not None:
            scores = [s if b is None else s + b[h] * LOG2E for s, b in zip(scores, biases)]
        ps, rl = _softmax_parts(scores)
        o = functools.reduce(jnp.add, [_dot(p, _head(v, g)) for p, v in zip(ps, vs)])
        outs.append(o * rl)
    return jnp.concatenate(outs, axis=-1)


def _diff_attn(q, ks, vs, lam, sub_gain, bd64, lam_init):
    qs = (q * (DIFF_DIM ** -0.5 * LOG2E)).astype(BF16)
    ks, vs = [k.astype(BF16) for k in ks], [v.astype(BF16) for v in vs]
    outs = []
    for h in range(DIFF_HEADS):
        o_h = None
        for m in range(2):
            scores = [_dot_nt(_head(qs, 2 * h + m, DIFF_DIM), _head(k, 2 * h + m, DIFF_DIM)) for k in ks]
            ps, rl = _softmax_parts(scores)
            pv = functools.reduce(jnp.add, [_dot(p, _head(v, h)) for p, v in zip(ps, vs)]) * rl
            o_h = pv if m == 0 else o_h - lam * pv
        outs.append(o_h)
    o = jnp.concatenate(outs, axis=-1)
    ms = _group_mean(o * o, bd64)
    return o * lax.rsqrt(ms + EPS) * sub_gain * (1.0 - lam_init)


def _ctx_attn_kernel(nq, nk, nv, gq, gk, gv, dq, dk, dv, lam_ref, subg_ref, bd64_ref, o_ref, *, lam_init):
    scale = HEAD_DIM ** -0.5
    o_ref[:, 0:256] = _mha(nq[...], [nk[...]], [nv[...]], NAT_HEADS, 1, scale)
    o_ref[:, 256:512] = _mha(gq[...], [gk[...]], [gv[...]], GQA_HEADS, GQA_HEADS // GQA_KV_HEADS, scale)
    lam = _diff_lambda(lam_ref, lam_init)
    o_ref[:, 512:768] = _diff_attn(dq[...], [dk[...]], [dv[...]], lam, subg_ref[...], bd64_ref[...], lam_init)


_COL = dict(nq=0, nk=1, nv=2, gq=3, gk=8, gv=9, dq=5, dk=6, dv=7, sgu=8)


def _ctx_attention(proj, lp, lam_init):
    blk = lambda name, w: pl.BlockSpec((SEQ, w), lambda b: (b, _COL[name]))
    return pl.pallas_call(
        functools.partial(_ctx_attn_kernel, lam_init=lam_init),
        grid=(BATCH,),
        in_specs=[blk("nq", 256), blk("nk", 256), blk("nv", 256),
                  blk("gq", 256), blk("gk", 128), blk("gv", 128),
                  blk("dq", 256), blk("dk", 256), blk("dv", 256),
                  _const_spec((4, DIFF_DIM)), _const_spec((1, 256)), _const_spec((256, 256))],
        out_specs=pl.BlockSpec((SEQ, 768), lambda b: (b, 0)),
        out_shape=jax.ShapeDtypeStruct((N_CTX, 768), F32),
        compiler_params=_cparams("arbitrary"),
        name="ctx_attention",
    )(*([proj] * 9), lp["diff_lambda"], lp["diff_sub_g"], lp["bd64"])


def _gqa_lat_kernel(q_ref, k_ref, v_ref, ck_ref, cv_ref, o_ref):
    o_ref[...] = _mha(q_ref[...], [k_ref[...], ck_ref[...]], [v_ref[...], cv_ref[...]],
                      GQA_HEADS, GQA_HEADS // GQA_KV_HEADS, HEAD_DIM ** -0.5)


def _diff_lat_kernel(q_ref, k_ref, v_ref, ck_ref, cv_ref, lam_ref, subg_ref, bd64_ref, o_ref, *, lam_init):
    lam = _diff_lambda(lam_ref, lam_init)
    o_ref[...] = _diff_attn(q_ref[...], [k_ref[...], ck_ref[...]], [v_ref[...], cv_ref[...]],
                            lam, subg_ref[...], bd64_ref[...], lam_init)


def _nat_lat_kernel(q_ref, k_ref, v_ref, ck_ref, cv_ref, bias_ref, o_ref):
    i = pl.program_id(1)
    k_row0 = jnp.clip(NAT_QROWS * i - NAT_ROWS // 2, 0, GRID_W - NAT_KROWS)
    start = pl.multiple_of(k_row0 * GRID_W, GRID_W)
    kw = k_ref[pl.ds(start, NAT_KROWS * GRID_W), :]
    vw = v_ref[pl.ds(start, NAT_KROWS * GRID_W), :]
    o_ref[...] = _mha(q_ref[...], [kw, ck_ref[...]], [vw, cv_ref[...]], NAT_HEADS, 1, HEAD_DIM ** -0.5,
                      biases=[bias_ref[0], None])


def _nat_bias_table(rpb):
    rows = DEC_SEQ // GRID_W
    nblk = rows // NAT_QROWS
    pad = jnp.pad(rpb, ((0, 0), (0, 0), (GRID_W - NAT_COLS, GRID_W - NAT_COLS)))
    toep = jnp.stack([pad[:, :, GRID_W - 1 - c:2 * GRID_W - 1 - c] for c in range(GRID_W)], axis=2)
    col = np.arange(GRID_W)
    cs = np.clip(col - NAT_COLS // 2, 0, GRID_W - NAT_COLS)
    col_ok = (col[None, :] >= cs[:, None]) & (col[None, :] < cs[:, None] + NAT_COLS)
    toep = jnp.where(col_ok, toep, NEG)
    masked = jnp.full((NAT_HEADS, GRID_W, GRID_W), NEG, F32)
    cases = []
    for blk in (0, 1, nblk - 1):
        r0 = blk * NAT_QROWS
        k0 = int(np.clip(r0 - NAT_ROWS // 2, 0, rows - NAT_KROWS))
        q_rows = []
        for qr in range(r0, r0 + NAT_QROWS):
            rs = int(np.clip(qr - NAT_ROWS // 2, 0, rows - NAT_ROWS))
            q_rows.append(jnp.concatenate(
                [toep[:, kr - qr + NAT_ROWS - 1] if rs <= kr < rs + NAT_ROWS else masked
                 for kr in range(k0, k0 + NAT_KROWS)], axis=-1))
        cases.append(jnp.concatenate(q_rows, axis=-2))
    return jnp.stack(cases, axis=0)


def _lat_attention(proj, caches, lp, l, lam_init):
    cnk, cnv, cgk, cgv, cdk, cdv = caches
    nq_blocks = DEC_SEQ // TQ
    qspec = lambda name: pl.BlockSpec((TQ, 256), lambda b, i: (b * nq_blocks + i, _COL[name]))
    kvspec = lambda name, w: pl.BlockSpec((DEC_SEQ, w), lambda b, i: (b, _COL[name]))
    cspec = lambda w: pl.BlockSpec((None, None, PAST_LEN, w), lambda b, i: (b, l, 0, 0))
    ospec = pl.BlockSpec((TQ, 256), lambda b, i: (b * nq_blocks + i, 0))
    oshape = jax.ShapeDtypeStruct((N_LAT, 256), F32)
    o_gqa = pl.pallas_call(
        _gqa_lat_kernel,
        grid=(DEC_BATCH, nq_blocks),
        in_specs=[qspec("gq"), kvspec("gk", 128), kvspec("gv", 128), cspec(128), cspec(128)],
        out_specs=ospec, out_shape=oshape,
        compiler_params=_cparams("arbitrary", "arbitrary"),
        name="gqa_lat_attention",
    )(proj, proj, proj, cgk.reshape(DEC_BATCH, DEPTH, PAST_LEN, 128), cgv.reshape(DEC_BATCH, DEPTH, PAST_LEN, 128))
    o_dif = pl.pallas_call(
        functools.partial(_diff_lat_kernel, lam_init=lam_init),
        grid=(DEC_BATCH, nq_blocks),
        in_specs=[qspec("dq"), kvspec("dk", 256), kvspec("dv", 256), cspec(256), cspec(256),
                  _const_spec((4, DIFF_DIM)), _const_spec((1, 256)), _const_spec((256, 256))],
        out_specs=ospec, out_shape=oshape,
        compiler_params=_cparams("arbitrary", "arbitrary"),
        name="diff_lat_attention",
    )(proj, proj, proj, cdk.reshape(DEC_BATCH, DEPTH, PAST_LEN, 256), cdv.reshape(DEC_BATCH, DEPTH, PAST_LEN, 256),
      lp["diff_lambda"], lp["diff_sub_g"], lp["bd64"])
    nblk = DEC_SEQ // (NAT_QROWS * GRID_W)
    nat_q = NAT_QROWS * GRID_W
    o_nat = pl.pallas_call(
        _nat_lat_kernel,
        grid=(DEC_BATCH, nblk),
        in_specs=[pl.BlockSpec((nat_q, 256), lambda b, i: (b * nblk + i, _COL["nq"])),
                  kvspec("nk", 256), kvspec("nv", 256), cspec(256), cspec(256),
                  pl.BlockSpec((1, NAT_HEADS, nat_q, NAT_KROWS * GRID_W),
                               lambda b, i: (jnp.where(i == 0, 0, jnp.where(i == nblk - 1, 2, 1)), 0, 0, 0))],
        out_specs=pl.BlockSpec((nat_q, 256), lambda b, i: (b * nblk + i, 0)),
        out_shape=oshape,
        compiler_params=_cparams("arbitrary", "arbitrary"),
        name="nat_lat_attention",
    )(proj, proj, proj, cnk.reshape(DEC_BATCH, DEPTH, PAST_LEN, 256), cnv.reshape(DEC_BATCH, DEPTH, PAST_LEN, 256),
      lp["nat_bias"])
    return o_nat, o_gqa, o_dif


def _merge_kernel(x_ref, mod_ref, n1_ref, b0_ref, b1_ref, b2_ref, b3_ref, wb_ref, wg_ref, bg_ref, wo_ref, o_ref):
    x = x_ref[...]
    mod = mod_ref[0]
    hb = _rms_mod(x, n1_ref[...], mod[:, D_MODEL:2 * D_MODEL], mod[:, 0:D_MODEL]).astype(BF16)
    merged = None
    for n, b_ref in enumerate((b0_ref, b1_ref, b2_ref, b3_ref)):
        cols = slice(n * D_MODEL, (n + 1) * D_MODEL)
        gate = _sigmoid(jnp.dot(hb, wg_ref[:, cols], preferred_element_type=F32) + bg_ref[:, cols])
        term = gate * _dot(b_ref[...], wb_ref[n])
        merged = term if merged is None else merged + term
    out = _dot(merged, wo_ref[...])
    o_ref[...] = x + mod[:, 2 * D_MODEL:3 * D_MODEL] * out


def _merge(x, mod_l, lp, branches, latent):
    n_tok = x.shape[0]
    return pl.pallas_call(
        _merge_kernel,
        grid=(n_tok // TB,),
        in_specs=[pl.BlockSpec((TB, D_MODEL), lambda i: (i, 0)), _mod_spec(latent), _const_spec((1, D_MODEL))]
                 + [pl.BlockSpec((TB, BRANCH_W), functools.partial(lambda i, c: (i, c), c=col)) for _, col in branches]
                 + [_const_spec((N_BRANCH, BRANCH_W, D_MODEL)), _const_spec((D_MODEL, N_BRANCH * D_MODEL)),
                    _const_spec((1, N_BRANCH * D_MODEL)), _const_spec((D_MODEL, D_MODEL))],
        out_specs=pl.BlockSpec((TB, D_MODEL), lambda i: (i, 0)),
        out_shape=jax.ShapeDtypeStruct((n_tok, D_MODEL), F32),
        compiler_params=_cparams("arbitrary"),
        name="merge_lat" if latent else "merge_ctx",
    )(x, mod_l, lp["norm1_g"], *[a for a, _ in branches], lp["w_branch"], lp["w_gate"], lp["b_gate"], lp["w_out"])


def _top_desc(s, k, with_rank=False):
    rows = []
    cur = s
    rank = jnp.full(s.shape, float(k), F32) if with_rank else None
    for i in range(k):
        m = jnp.max(cur, axis=0, keepdims=True)
        rows.append(m)
        hit = cur == m
        if with_rank:
            rank = jnp.where(hit, float(i), rank)
        cur = jnp.where(hit, NEG, cur)
    top = jnp.concatenate(rows, axis=0)
    return (top, rank) if with_rank else top


def _peer_score_kernel(x_ref, mod_ref, n2_ref, wqt_ref, sk_ref, ht_ref, e1_ref, cnt_ref, rk_ref, e2_ref):
    mod = mod_ref[0]
    h2 = _rms_mod(x_ref[...], n2_ref[...], mod[:, 4 * D_MODEL:5 * D_MODEL], mod[:, 3 * D_MODEL:4 * D_MODEL])
    htb = h2.T.astype(BF16)
    ht_ref[...] = htb
    qt = jnp.dot(wqt_ref[...], htb, preferred_element_type=F32)
    k1 = PEER_TOPK + 1
    half = PEER_TOPK // 2
    row = lax.broadcasted_iota(jnp.int32, (k1, 1), 0)
    for h in range(PEER_HEADS):
        s1, s2 = (jnp.dot(sk_ref[p], qt[(2 * h + p) * PEER_KEYS:(2 * h + p + 1) * PEER_KEYS, :].astype(BF16),
                          preferred_element_type=F32) for p in range(2))
        t1 = _top_desc(s1, k1)
        t2, rk2 = _top_desc(s2, k1, with_rank=True)
        cands = [t1[0:1] + t2, jnp.where(row >= 1, t1 + t2[0:1], NEG)]
        for a in range(1, half):
            nb = k1 // (a + 1)
            cands.append(jnp.where((row[0:half] >= 1) & (row[0:half] < nb), t1[a:a + 1] + t2[0:half], NEG))
        best = _top_desc(jnp.concatenate(cands, axis=0), k1)
        z = jnp.sum(jnp.exp(best[0:PEER_TOPK] - best[0:1]), axis=0, keepdims=True)
        thr = 0.5 * (best[PEER_TOPK - 1:PEER_TOPK] + best[PEER_TOPK:k1])
        need = thr - s1
        cnt = None
        for b in range(PEER_TOPK):
            ok = jnp.where(t2[b:b + 1] >= need, 1.0, 0.0)
            cnt = ok if cnt is None else cnt + ok
        e1_ref[h] = jnp.exp(s1 - t1[0:1]) * (0.5 / z)
        cnt_ref[h] = cnt
        rk_ref[h] = pltpu.bitcast(rk2.astype(BF16), jnp.int32)
        e2_ref[h] = pltpu.bitcast(jnp.exp(s2 - t2[0:1]).astype(BF16), jnp.int32)


def _peer_dense_kernel(ht_ref, e1_ref, cnt_ref, rk_ref, e2_ref, u_ref, v_ref, x_ref, mod_ref, o_ref,
                       acc_ref, a_ref, w_ref):
    e = pl.program_id(1)
    n_tok = acc_ref.shape[0]

    @pl.when(e == 0)
    def _():
        acc_ref[...] = jnp.zeros_like(acc_ref)

    a_ref[...] = jnp.dot(u_ref[...], ht_ref[...], preferred_element_type=F32)
    zero = jnp.zeros((), BF16)
    for rr in range(PEER_ROWS):
        r = (e % (SUBLANES // PEER_ROWS)) * PEER_ROWS + rr
        rows = slice(rr * PEER_KEYS, (rr + 1) * PEER_KEYS)
        def tile_row(ref, h):
            t16 = jnp.broadcast_to(ref[h, pl.ds(r, 1), :], (BF16_ROWS, n_tok)).astype(BF16)
            return jnp.concatenate([t16] * (PEER_KEYS // BF16_ROWS), axis=0)
        cnt_rows = [tile_row(cnt_ref, h) for h in range(PEER_HEADS)]
        e1_rows = [tile_row(e1_ref, h) for h in range(PEER_HEADS)]
        for lt in range(n_tok // 128):
            lanes = slice(lt * 128, (lt + 1) * 128)
            g = None
            for h in range(PEER_HEADS):
                sel = pltpu.bitcast(rk_ref[h, :, lanes], BF16) < cnt_rows[h][:, lanes]
                term = jnp.where(sel, pltpu.bitcast(e2_ref[h, :, lanes], BF16) * e1_rows[h][:, lanes], zero)
                g = term if g is None else g + term
            a = a_ref[rows, lanes]
            t = jnp.tanh(a * (0.7978845608028654 + (0.7978845608028654 * 0.044715) * (a * a)))
            w_ref[rr * (PEER_KEYS // 2):(rr + 1) * (PEER_KEYS // 2), lanes] = pltpu.bitcast(
                g * (a + a * t).astype(BF16), jnp.int32)
    acc_ref[...] += jnp.dot(pltpu.bitcast(w_ref[...], BF16).T, v_ref[...], preferred_element_type=F32)

    @pl.when(e == pl.num_programs(1) - 1)
    def _():
        o_ref[...] = x_ref[...] + mod_ref[0][:, 5 * D_MODEL:6 * D_MODEL] * acc_ref[...]


def _peer(x, mod_l, lp, latent):
    n_tok = x.shape[0]

    def mod_spec(tb):
        if latent:
            return pl.BlockSpec((1, 1, 6 * D_MODEL), lambda i, *_: (1 + i // (DEC_SEQ // tb), 0, 0))
        return pl.BlockSpec((1, 1, 6 * D_MODEL), lambda i, *_: (0, 0, 0))

    n_keys2 = PEER_HEADS * PEER_QDIM
    f32_tab = lambda n: (PEER_HEADS, PEER_KEYS, n)
    bf16_tab = lambda n: (PEER_HEADS, PEER_KEYS // 2, n)
    ht, *tabs = pl.pallas_call(
        _peer_score_kernel,
        grid=(n_tok // TBS,),
        in_specs=[pl.BlockSpec((TBS, D_MODEL), lambda i: (i, 0)), mod_spec(TBS), _const_spec((1, D_MODEL)),
                  _const_spec((n_keys2, D_MODEL)), _const_spec((2, PEER_KEYS, PEER_QDIM // 2))],
        out_specs=[pl.BlockSpec((D_MODEL, TBS), lambda i: (0, i))]
                  + [pl.BlockSpec(f32_tab(TBS), lambda i: (0, 0, i))] * 2
                  + [pl.BlockSpec(bf16_tab(TBS), lambda i: (0, 0, i))] * 2,
        out_shape=[jax.ShapeDtypeStruct((D_MODEL, n_tok), BF16),
                   jax.ShapeDtypeStruct(f32_tab(n_tok), F32), jax.ShapeDtypeStruct(f32_tab(n_tok), F32),
                   jax.ShapeDtypeStruct(bf16_tab(n_tok), jnp.int32), jax.ShapeDtypeStruct(bf16_tab(n_tok), jnp.int32)],
        compiler_params=_cparams("arbitrary"),
        name="peer_scores_lat" if latent else "peer_scores_ctx",
    )(x, mod_l, lp["norm2_g"], lp["peer_wqt"], lp["peer_subkeys"])
    nb = n_tok // TBP
    n_et = PEER_KEYS * PEER_KEYS // PEER_ET
    row_spec = pl.BlockSpec((PEER_HEADS, SUBLANES, TBP), lambda i, e: (0, e // (SUBLANES // PEER_ROWS), i))
    return pl.pallas_call(
        _peer_dense_kernel,
        grid=(nb, n_et),
        in_specs=[pl.BlockSpec((D_MODEL, TBP), lambda i, e: (0, i)), row_spec, row_spec]
                 + [pl.BlockSpec(bf16_tab(TBP), lambda i, e: (0, 0, i))] * 2
                 + [pl.BlockSpec((PEER_ET, D_MODEL), lambda i, e: (e, 0)),
                    pl.BlockSpec((PEER_ET, D_MODEL), lambda i, e: (e, 0)),
                    pl.BlockSpec((TBP, D_MODEL), lambda i, e: (i, 0)),
                    mod_spec(TBP)],
        out_specs=pl.BlockSpec((TBP, D_MODEL), lambda i, e: (i, 0)),
        out_shape=jax.ShapeDtypeStruct((n_tok, D_MODEL), F32),
        scratch_shapes=[pltpu.VMEM((TBP, D_MODEL), F32), pltpu.VMEM((PEER_ET, TBP), F32),
                        pltpu.VMEM((PEER_ET // 2, TBP), jnp.int32)],
        compiler_params=_cparams("arbitrary", "arbitrary"),
        name="peer_dense_lat" if latent else "peer_dense_ctx",
    )(ht, *tabs, lp["peer_u"], lp["peer_v"], x, mod_l)


def _layer_params(l, w):
    tile = lambda g, n: jnp.tile(g, n)
    gains = jnp.stack([
        tile(w["nat_qk_g"][l, 0], 4), tile(w["nat_qk_g"][l, 1], 4),
        tile(w["gqa_qk_g"][l, 0], 4), tile(w["gqa_qk_g"][l, 1], 4),
        tile(w["diff_qk_g"][l, 0], 8), tile(w["diff_qk_g"][l, 1], 8),
        jnp.zeros((256,), F32), jnp.zeros((256,), F32)])
    return dict(
        norm1_g=w["norm1_g"][l].reshape(1, D_MODEL),
        norm2_g=w["norm2_g"][l].reshape(1, D_MODEL),
        w_in=w["w_in"][l].astype(BF16),
        qk_gains=gains,
        bd64=_block_ones(256, 64), bd32=_block_ones(256, 32), bd256=_block_ones(256, 256),
        rope=w["rope"],
        sgu_norm_g=w["sgu_norm_g"][l].reshape(1, SGU_WIDTH),
        sgu_w=w["sgu_w"][l].astype(BF16),
        sgu_b=jnp.repeat(w["sgu_b"][l].T, SGU_WIDTH // SGU_GROUPS, axis=1),
        diff_lambda=w["diff_lambda"][l],
        diff_sub_g=tile(w["diff_sub_g"][l], 4).reshape(1, 256),
        nat_bias=_nat_bias_table(w["nat_rpb"][l]),
        w_branch=w["w_branch"][l].astype(BF16),
        w_gate=w["w_gate"][l].astype(BF16),
        b_gate=w["b_gate"][l].reshape(1, N_BRANCH * D_MODEL),
        w_out=w["w_out"][l].astype(BF16),
        peer_wqt=w["peer_wq"][l].T.astype(BF16),
        peer_subkeys=w["peer_subkeys"][l].astype(BF16),
        peer_u=w["peer_u"][l].astype(BF16),
        peer_v=w["peer_v"][l].astype(BF16),
    )


def kernel(x_prompt, x_sample, c, cache_nat_k, cache_nat_v, cache_gqa_k, cache_gqa_v, cache_diff_k, cache_diff_v, c_ctx, w_mod, b_mod, norm1_g, norm2_g, w_in, nat_qk_g, nat_rpb, gqa_qk_g, diff_qk_g, diff_lambda, diff_sub_g, sgu_norm_g, sgu_w, sgu_b, w_branch, w_gate, b_gate, w_out, peer_wq, peer_subkeys, peer_u, peer_v):
    w = dict(norm1_g=norm1_g, norm2_g=norm2_g, w_in=w_in, nat_qk_g=nat_qk_g, nat_rpb=nat_rpb,
             gqa_qk_g=gqa_qk_g, diff_qk_g=diff_qk_g, diff_lambda=diff_lambda, diff_sub_g=diff_sub_g,
             sgu_norm_g=sgu_norm_g, sgu_w=sgu_w, sgu_b=sgu_b, w_branch=w_branch, w_gate=w_gate,
             b_gate=b_gate, w_out=w_out, peer_wq=peer_wq, peer_subkeys=peer_subkeys, peer_u=peer_u,
             peer_v=peer_v, rope=_rope_tables())
    cvec = jnp.concatenate([c_ctx[None], c, jnp.zeros((MOD_ROWS - 1 - DEC_BATCH, D_MODEL), F32)], axis=0)
    mod = _modulation(cvec, w_mod, b_mod).reshape(DEPTH, MOD_ROWS, 1, 6 * D_MODEL)
    xp = x_prompt.reshape(N_CTX, D_MODEL)
    xs = x_sample.reshape(N_LAT, D_MODEL)
    new = []
    for l in range(DEPTH):
        lp = _layer_params(l, w)
        lam_init = 0.8 - 0.6 * math.exp(-0.3 * l)
        caches = (cache_nat_k, cache_nat_v, cache_gqa_k, cache_gqa_v, cache_diff_k, cache_diff_v)
        proj = _in_projection(xp, mod[l], lp, latent=False)
        new.append(proj)
        attn = _ctx_attention(proj, lp, lam_init)
        xp = _merge(xp, mod[l], lp, [(attn, 0), (attn, 1), (attn, 2), (proj, _COL["sgu"])], latent=False)
        xp = _peer(xp, mod[l], lp, latent=False)
        proj = _in_projection(xs, mod[l], lp, latent=True)
        o_nat, o_gqa, o_dif = _lat_attention(proj, caches, lp, l, lam_init)
        xs = _merge(xs, mod[l], lp, [(o_nat, 0), (o_gqa, 0), (o_dif, 0), (proj, _COL["sgu"])], latent=True)
        xs = _peer(xs, mod[l], lp, latent=True)

    def cache_out(lo, width, tail):
        a = jnp.stack([p[:, lo:lo + width].reshape(BATCH, SEQ, width) for p in new], axis=1)
        return a.reshape((BATCH, DEPTH, SEQ) + tail)

    return (xp.reshape(BATCH, SEQ, D_MODEL), xs.reshape(DEC_BATCH, DEC_SEQ, D_MODEL),
            cache_out(256, 256, (NAT_HEADS, HEAD_DIM)), cache_out(512, 256, (NAT_HEADS, HEAD_DIM)),
            cache_out(1024, 128, (GQA_KV_HEADS, HEAD_DIM)), cache_out(1152, 128, (GQA_KV_HEADS, HEAD_DIM)),
            cache_out(1536, 256, (DIFF_HEADS, 2, DIFF_DIM)), cache_out(1792, 256, (DIFF_HEADS, HEAD_DIM)))
```

```python
import functools
import math

import numpy as np
import jax
import jax.numpy as jnp
from jax import lax
from jax.experimental import pallas as pl
from jax.experimental.pallas import tpu as pltpu

F32 = jnp.float32
BF16 = jnp.bfloat16

D_MODEL = 1024
BATCH = 16
SEQ = 256
DEPTH = 2
DEC_BATCH = 2
DEC_SEQ = 4096
PAST_LEN = 512
GRID_W = 64
HEAD_DIM = 64
NAT_HEADS = 4
NAT_ROWS = 8
NAT_COLS = 16
GQA_HEADS = 4
GQA_KV_HEADS = 2
DIFF_HEADS = 4
DIFF_DIM = 32
SGU_WIDTH = 256
SGU_GROUPS = 4
CHUNK = 128
N_BRANCH = 4
BRANCH_W = 256
IN_W = 2560
PEER_HEADS = 8
PEER_KEYS = 128
PEER_QDIM = 256
PEER_TOPK = 16
ROPE_BASE = 10000.0
EPS = 1e-6

N_CTX = BATCH * SEQ
N_LAT = DEC_BATCH * DEC_SEQ
MOD_ROWS = 8
TB = 256
TQ = 256
PROJ_W = 2304
NAT_QROWS = 4
NAT_KROWS = NAT_QROWS + NAT_ROWS
TBS = 512
TBP = 1024
PEER_ET = 512
PEER_ROWS = PEER_ET // PEER_KEYS
SUBLANES = 8
BF16_ROWS = 16
VMEM_LIMIT = 56 * 1024 * 1024
NEG = -1e30


def _cparams(*sem):
    return pltpu.CompilerParams(dimension_semantics=sem, vmem_limit_bytes=VMEM_LIMIT)


def _const_spec(shape):
    return pl.BlockSpec(shape, lambda *_: (0,) * len(shape))


def _gelu(x):
    return 0.5 * x * (1.0 + jnp.tanh(0.7978845608028654 * (x + 0.044715 * (x * x * x))))


def _sigmoid(x):
    return 1.0 / (1.0 + jnp.exp(-x))


def _dot(a, b):
    return jnp.dot(a.astype(BF16), b.astype(BF16), preferred_element_type=F32)


def _dot_nt(a, b):
    return lax.dot_general(a.astype(BF16), b.astype(BF16), (((1,), (1,)), ((), ())),
                           preferred_element_type=F32)


def _group_mean(y2, ones_bd):
    hi = y2.astype(BF16)
    lo = (y2 - hi.astype(F32)).astype(BF16)
    return (jnp.dot(hi, ones_bd, preferred_element_type=F32)
            + jnp.dot(lo, ones_bd, preferred_element_type=F32))


def _block_ones(width, group):
    idx = np.arange(width) // group
    return jnp.asarray((idx[:, None] == idx[None, :]).astype(np.float32) / group, dtype=BF16)


def _rms_mod(x, gain, scale, shift):
    xn = x * lax.rsqrt(jnp.mean(x * x, axis=-1, keepdims=True) + EPS) * gain
    return xn * (1.0 + scale) + shift


def _mod_spec(latent):
    if latent:
        return pl.BlockSpec((1, 1, 6 * D_MODEL), lambda i, *_: (1 + i // (DEC_SEQ // TB), 0, 0))
    return pl.BlockSpec((1, 1, 6 * D_MODEL), lambda i, *_: (0, 0, 0))


def _mod_kernel(c_ref, w_ref, b_ref, o_ref):
    c = c_ref[...]
    s = c * _sigmoid(c)
    o_ref[0] = _dot(s, w_ref[0]) + b_ref[0]


def _modulation(cvec, w_mod, b_mod):
    tn = 1536
    return pl.pallas_call(
        _mod_kernel,
        grid=(DEPTH, 6 * D_MODEL // tn),
        in_specs=[pl.BlockSpec((MOD_ROWS, D_MODEL), lambda l, j: (0, 0)),
                  pl.BlockSpec((1, D_MODEL, tn), lambda l, j: (l, 0, j)),
                  pl.BlockSpec((1, 1, tn), lambda l, j: (l, 0, j))],
        out_specs=pl.BlockSpec((1, MOD_ROWS, tn), lambda l, j: (l, 0, j)),
        out_shape=jax.ShapeDtypeStruct((DEPTH, MOD_ROWS, 6 * D_MODEL), F32),
        compiler_params=_cparams("arbitrary", "arbitrary"),
        name="modulation",
    )(cvec, w_mod, b_mod.reshape(DEPTH, 1, 6 * D_MODEL))


def _rope_tables():
    t = np.arange(DEC_SEQ)
    pos = (t // GRID_W, t % GRID_W)
    out = []
    for d in (HEAD_DIM, DIFF_DIM):
        half, quarter = d // 2, d // 4
        inv = ROPE_BASE ** (-np.arange(quarter, dtype=np.float32) * 2.0 / half)
        lane = np.arange(128) % d
        part, j = lane // half, lane % half
        ang = np.stack([pos[0][:, None] * inv[None, :], pos[1][:, None] * inv[None, :]], axis=1)
        a = ang[:, part, j % quarter].astype(np.float32)
        cos, sin = np.cos(a), np.sin(a)
        out += [jnp.asarray(cos, dtype=F32),
                jnp.asarray(np.where(j < quarter, -sin, 0.0), dtype=F32),
                jnp.asarray(np.where(j >= quarter, sin, 0.0), dtype=F32)]
    return out


def _rope(y, c, sa, sb, quarter):
    w = y.shape[-1]
    rep = w // 128
    if rep > 1:
        c, sa, sb = (jnp.concatenate([t] * rep, axis=-1) for t in (c, sa, sb))
    up = pltpu.roll(y, w - quarter, 1)
    dn = pltpu.roll(y, quarter, 1)
    return y * c + up * sa + dn * sb


def _inproj_kernel(*refs, latent):
    (x_ref, mod_ref, n1_ref, w_ref, g_ref, bd64_ref, bd32_ref, bd256_ref,
     sgn_ref, sgw_ref, sgb_ref) = refs[:11]
    o_ref = refs[-1]
    mod = mod_ref[0]
    h = _rms_mod(x_ref[...], n1_ref[...], mod[:, D_MODEL:2 * D_MODEL], mod[:, 0:D_MODEL])
    y = jnp.dot(h.astype(BF16), w_ref[...], preferred_element_type=F32)

    def qk_norm(lo, width, bd, gain_row):
        v = y[:, lo:lo + width]
        ms = _group_mean(v * v, bd)
        return v * lax.rsqrt(ms + EPS) * g_ref[gain_row:gain_row + 1, 0:width]

    bd64 = bd64_ref[...]
    bd32 = bd32_ref[...]
    if latent:
        c64, sa64, sb64, c32, sa32, sb32 = (r[...] for r in refs[11:17])
        rope64 = functools.partial(_rope, c=c64, sa=sa64, sb=sb64, quarter=HEAD_DIM // 4)
        rope32 = functools.partial(_rope, c=c32, sa=sa32, sb=sb32, quarter=DIFF_DIM // 4)
    else:
        rope64 = rope32 = lambda v: v
    o_ref[:, 0:256] = qk_norm(0, 256, bd64, 0)
    o_ref[:, 256:512] = qk_norm(256, 256, bd64, 1)
    o_ref[:, 512:768] = y[:, 512:768]
    o_ref[:, 768:1024] = rope64(qk_norm(768, 256, bd64, 2))
    o_ref[:, 1024:1152] = rope64(qk_norm(1024, 128, bd64[0:128, 0:128], 3))
    o_ref[:, 1152:1280] = y[:, 1152:1280]
    o_ref[:, 1280:1536] = rope32(qk_norm(1280, 256, bd32, 4))
    o_ref[:, 1536:1792] = rope32(qk_norm(1536, 256, bd32, 5))
    o_ref[:, 1792:2048] = y[:, 1792:2048]
    u = _gelu(y[:, 2048:2304])
    v = _gelu(y[:, 2304:2560])
    vn = v * lax.rsqrt(_group_mean(v * v, bd256_ref[...]) + EPS) * sgn_ref[...]
    vnb = vn.astype(BF16)
    lane_group = lax.broadcasted_iota(jnp.int32, (CHUNK, SGU_WIDTH), 1) // (SGU_WIDTH // SGU_GROUPS)
    for n in range(TB // CHUNK):
        vc = vnb[n * CHUNK:(n + 1) * CHUNK, :]
        s = sgb_ref[...]
        for g in range(SGU_GROUPS):
            sg = jnp.dot(sgw_ref[g], vc, preferred_element_type=F32)
            s = s + jnp.where(lane_group == g, sg, 0.0)
        o_ref[n * CHUNK:(n + 1) * CHUNK, 2048:2304] = u[n * CHUNK:(n + 1) * CHUNK, :] * s


def _in_projection(x, mod_l, lp, latent):
    n_tok = x.shape[0]
    in_specs = [pl.BlockSpec((TB, D_MODEL), lambda i: (i, 0)),
                _mod_spec(latent),
                _const_spec((1, D_MODEL)),
                _const_spec((D_MODEL, IN_W)),
                _const_spec((8, 256)),
                _const_spec((256, 256)), _const_spec((256, 256)), _const_spec((256, 256)),
                _const_spec((1, SGU_WIDTH)),
                _const_spec((SGU_GROUPS, CHUNK, CHUNK)),
                _const_spec((CHUNK, SGU_WIDTH))]
    args = [x, mod_l, lp["norm1_g"], lp["w_in"], lp["qk_gains"], lp["bd64"], lp["bd32"], lp["bd256"],
            lp["sgu_norm_g"], lp["sgu_w"], lp["sgu_b"]]
    if latent:
        in_specs += [pl.BlockSpec((TB, 128), lambda i: (i % (DEC_SEQ // TB), 0))] * 6
        args += lp["rope"]
    return pl.pallas_call(
        functools.partial(_inproj_kernel, latent=latent),
        grid=(n_tok // TB,),
        in_specs=in_specs,
        out_specs=pl.BlockSpec((TB, PROJ_W), lambda i: (i, 0)),
        out_shape=jax.ShapeDtypeStruct((n_tok, PROJ_W), F32),
        compiler_params=_cparams("arbitrary"),
        name="in_projection_lat" if latent else "in_projection_ctx",
    )(*args)


LOG2E = 1.4426950408889634


def _softmax_parts(scores):
    m = functools.reduce(jnp.maximum, [jnp.max(s, axis=-1, keepdims=True) for s in scores])
    ps = [jnp.exp2(s - m) for s in scores]
    l = functools.reduce(jnp.add, [jnp.sum(p, axis=-1, keepdims=True) for p in ps])
    return [p.astype(BF16) for p in ps], 1.0 / l


def _diff_lambda(lam_ref, lam_init):
    lv = lam_ref[...]
    a = jnp.sum(lv[0:1] * lv[1:2], axis=-1, keepdims=True)
    b = jnp.sum(lv[2:3] * lv[3:4], axis=-1, keepdims=True)
    return jnp.exp(a) - jnp.exp(b) + lam_init


def _head(ref_or_val, h, width=HEAD_DIM):
    return ref_or_val[:, h * width:(h + 1) * width]


def _mha(q, ks, vs, n_heads, kv_group, scale, biases=None):
    outs = []
    ks, vs = [k.astype(BF16) for k in ks], [v.astype(BF16) for v in vs]
    qs = (q * (scale * LOG2E)).astype(BF16)
    for h in range(n_heads):
        g = h // kv_group
        scores = [_dot_nt(_head(qs, h), _head(k, g)) for k in ks]
        if biases is not None:
            scores = [s if b is None else s + b[h] * LOG2E for s, b in zip(scores, biases)]
        ps, rl = _softmax_parts(scores)
        o = functools.reduce(jnp.add, [_dot(p, _head(v, g)) for p, v in zip(ps, vs)])
        outs.append(o * rl)
    return jnp.concatenate(outs, axis=-1)


def _diff_attn(q, ks, vs, lam, sub_gain, bd64, lam_init):
    qs = (q * (DIFF_DIM ** -0.5 * LOG2E)).astype(BF16)
    ks, vs = [k.astype(BF16) for k in ks], [v.astype(BF16) for v in vs]
    outs = []
    for h in range(DIFF_HEADS):
        o_h = None
        for m in range(2):
            scores = [_dot_nt(_head(qs, 2 * h + m, DIFF_DIM), _head(k, 2 * h + m, DIFF_DIM)) for k in ks]
            ps, rl = _softmax_parts(scores)
            pv = functools.reduce(jnp.add, [_dot(p, _head(v, h)) for p, v in zip(ps, vs)]) * rl
            o_h = pv if m == 0 else o_h - lam * pv
        outs.append(o_h)
    o = jnp.concatenate(outs, axis=-1)
    ms = _group_mean(o * o, bd64)
    return o * lax.rsqrt(ms + EPS) * sub_gain * (1.0 - lam_init)


def _ctx_attn_kernel(nq, nk, nv, gq, gk, gv, dq, dk, dv, lam_ref, subg_ref, bd64_ref, o_ref, *, lam_init):
    scale = HEAD_DIM ** -0.5
    o_ref[:, 0:256] = _mha(nq[...], [nk[...]], [nv[...]], NAT_HEADS, 1, scale)
    o_ref[:, 256:512] = _mha(gq[...], [gk[...]], [gv[...]], GQA_HEADS, GQA_HEADS // GQA_KV_HEADS, scale)
    lam = _diff_lambda(lam_ref, lam_init)
    o_ref[:, 512:768] = _diff_attn(dq[...], [dk[...]], [dv[...]], lam, subg_ref[...], bd64_ref[...], lam_init)


_COL = dict(nq=0, nk=1, nv=2, gq=3, gk=8, gv=9, dq=5, dk=6, dv=7, sgu=8)


def _ctx_attention(proj, lp, lam_init):
    blk = lambda name, w: pl.BlockSpec((SEQ, w), lambda b: (b, _COL[name]))
    return pl.pallas_call(
        functools.partial(_ctx_attn_kernel, lam_init=lam_init),
        grid=(BATCH,),
        in_specs=[blk("nq", 256), blk("nk", 256), blk("nv", 256),
                  blk("gq", 256), blk("gk", 128), blk("gv", 128),
                  blk("dq", 256), blk("dk", 256), blk("dv", 256),
                  _const_spec((4, DIFF_DIM)), _const_spec((1, 256)), _const_spec((256, 256))],
        out_specs=pl.BlockSpec((SEQ, 768), lambda b: (b, 0)),
        out_shape=jax.ShapeDtypeStruct((N_CTX, 768), F32),
        compiler_params=_cparams("arbitrary"),
        name="ctx_attention",
    )(*([proj] * 9), lp["diff_lambda"], lp["diff_sub_g"], lp["bd64"])


def _gqa_lat_kernel(q_ref, k_ref, v_ref, ck_ref, cv_ref, o_ref):
    o_ref[...] = _mha(q_ref[...], [k_ref[...], ck_ref[...]], [v_ref[...], cv_ref[...]],
                      GQA_HEADS, GQA_HEADS // GQA_KV_HEADS, HEAD_DIM ** -0.5)


def _diff_lat_kernel(q_ref, k_ref, v_ref, ck_ref, cv_ref, lam_ref, subg_ref, bd64_ref, o_ref, *, lam_init):
    lam = _diff_lambda(lam_ref, lam_init)
    o_ref[...] = _diff_attn(q_ref[...], [k_ref[...], ck_ref[...]], [v_ref[...], cv_ref[...]],
                            lam, subg_ref[...], bd64_ref[...], lam_init)


def _nat_lat_kernel(q_ref, k_ref, v_ref, ck_ref, cv_ref, bias_ref, o_ref):
    i = pl.program_id(1)
    k_row0 = jnp.clip(NAT_QROWS * i - NAT_ROWS // 2, 0, GRID_W - NAT_KROWS)
    start = pl.multiple_of(k_row0 * GRID_W, GRID_W)
    kw = k_ref[pl.ds(start, NAT_KROWS * GRID_W), :]
    vw = v_ref[pl.ds(start, NAT_KROWS * GRID_W), :]
    o_ref[...] = _mha(q_ref[...], [kw, ck_ref[...]], [vw, cv_ref[...]], NAT_HEADS, 1, HEAD_DIM ** -0.5,
                      biases=[bias_ref[0], None])


def _nat_bias_table(rpb):
    rows = DEC_SEQ // GRID_W
    nblk = rows // NAT_QROWS
    pad = jnp.pad(rpb, ((0, 0), (0, 0), (GRID_W - NAT_COLS, GRID_W - NAT_COLS)))
    toep = jnp.stack([pad[:, :, GRID_W - 1 - c:2 * GRID_W - 1 - c] for c in range(GRID_W)], axis=2)
    col = np.arange(GRID_W)
    cs = np.clip(col - NAT_COLS // 2, 0, GRID_W - NAT_COLS)
    col_ok = (col[None, :] >= cs[:, None]) & (col[None, :] < cs[:, None] + NAT_COLS)
    toep = jnp.where(col_ok, toep, NEG)
    masked = jnp.full((NAT_HEADS, GRID_W, GRID_W), NEG, F32)
    cases = []
    for blk in (0, 1, nblk - 1):
        r0 = blk * NAT_QROWS
        k0 = int(np.clip(r0 - NAT_ROWS // 2, 0, rows - NAT_KROWS))
        q_rows = []
        for qr in range(r0, r0 + NAT_QROWS):
            rs = int(np.clip(qr - NAT_ROWS // 2, 0, rows - NAT_ROWS))
            q_rows.append(jnp.concatenate(
                [toep[:, kr - qr + NAT_ROWS - 1] if rs <= kr < rs + NAT_ROWS else masked
                 for kr in range(k0, k0 + NAT_KROWS)], axis=-1))
        cases.append(jnp.concatenate(q_rows, axis=-2))
    return jnp.stack(cases, axis=0)


def _lat_attention(proj, caches, lp, l, lam_init):
    cnk, cnv, cgk, cgv, cdk, cdv = caches
    nq_blocks = DEC_SEQ // TQ
    qspec = lambda name: pl.BlockSpec((TQ, 256), lambda b, i: (b * nq_blocks + i, _COL[name]))
    kvspec = lambda name, w: pl.BlockSpec((DEC_SEQ, w), lambda b, i: (b, _COL[name]))
    cspec = lambda w: pl.BlockSpec((None, None, PAST_LEN, w), lambda b, i: (b, l, 0, 0))
    ospec = pl.BlockSpec((TQ, 256), lambda b, i: (b * nq_blocks + i, 0))
    oshape = jax.ShapeDtypeStruct((N_LAT, 256), F32)
    o_gqa = pl.pallas_call(
        _gqa_lat_kernel,
        grid=(DEC_BATCH, nq_blocks),
        in_specs=[qspec("gq"), kvspec("gk", 128), kvspec("gv", 128), cspec(128), cspec(128)],
        out_specs=ospec, out_shape=oshape,
        compiler_params=_cparams("arbitrary", "arbitrary"),
        name="gqa_lat_attention",
    )(proj, proj, proj, cgk.reshape(DEC_BATCH, DEPTH, PAST_LEN, 128), cgv.reshape(DEC_BATCH, DEPTH, PAST_LEN, 128))
    o_dif = pl.pallas_call(
        functools.partial(_diff_lat_kernel, lam_init=lam_init),
        grid=(DEC_BATCH, nq_blocks),
        in_specs=[qspec("dq"), kvspec("dk", 256), kvspec("dv", 256), cspec(256), cspec(256),
                  _const_spec((4, DIFF_DIM)), _const_spec((1, 256)), _const_spec((256, 256))],
        out_specs=ospec, out_shape=oshape,
        compiler_params=_cparams("arbitrary", "arbitrary"),
        name="diff_lat_attention",
    )(proj, proj, proj, cdk.reshape(DEC_BATCH, DEPTH, PAST_LEN, 256), cdv.reshape(DEC_BATCH, DEPTH, PAST_LEN, 256),
      lp["diff_lambda"], lp["diff_sub_g"], lp["bd64"])
    nblk = DEC_SEQ // (NAT_QROWS * GRID_W)
    nat_q = NAT_QROWS * GRID_W
    o_nat = pl.pallas_call(
        _nat_lat_kernel,
        grid=(DEC_BATCH, nblk),
        in_specs=[pl.BlockSpec((nat_q, 256), lambda b, i: (b * nblk + i, _COL["nq"])),
                  kvspec("nk", 256), kvspec("nv", 256), cspec(256), cspec(256),
                  pl.BlockSpec((1, NAT_HEADS, nat_q, NAT_KROWS * GRID_W),
                               lambda b, i: (jnp.where(i == 0, 0, jnp.where(i == nblk - 1, 2, 1)), 0, 0, 0))],
        out_specs=pl.BlockSpec((nat_q, 256), lambda b, i: (b * nblk + i, 0)),
        out_shape=oshape,
        compiler_params=_cparams("arbitrary", "arbitrary"),
        name="nat_lat_attention",
    )(proj, proj, proj, cnk.reshape(DEC_BATCH, DEPTH, PAST_LEN, 256), cnv.reshape(DEC_BATCH, DEPTH, PAST_LEN, 256),
      lp["nat_bias"])
    return o_nat, o_gqa, o_dif


def _merge_kernel(x_ref, mod_ref, n1_ref, b0_ref, b1_ref, b2_ref, b3_ref, wb_ref, wg_ref, bg_ref, wo_ref, o_ref):
    x = x_ref[...]
    mod = mod_ref[0]
    hb = _rms_mod(x, n1_ref[...], mod[:, D_MODEL:2 * D_MODEL], mod[:, 0:D_MODEL]).astype(BF16)
    merged = None
    for n, b_ref in enumerate((b0_ref, b1_ref, b2_ref, b3_ref)):
        cols = slice(n * D_MODEL, (n + 1) * D_MODEL)
        gate = _sigmoid(jnp.dot(hb, wg_ref[:, cols], preferred_element_type=F32) + bg_ref[:, cols])
        term = gate * _dot(b_ref[...], wb_ref[n])
        merged = term if merged is None else merged + term
    out = _dot(merged, wo_ref[...])
    o_ref[...] = x + mod[:, 2 * D_MODEL:3 * D_MODEL] * out


def _merge(x, mod_l, lp, branches, latent):
    n_tok = x.shape[0]
    return pl.pallas_call(
        _merge_kernel,
        grid=(n_tok // TB,),
        in_specs=[pl.BlockSpec((TB, D_MODEL), lambda i: (i, 0)), _mod_spec(latent), _const_spec((1, D_MODEL))]
                 + [pl.BlockSpec((TB, BRANCH_W), functools.partial(lambda i, c: (i, c), c=col)) for _, col in branches]
                 + [_const_spec((N_BRANCH, BRANCH_W, D_MODEL)), _const_spec((D_MODEL, N_BRANCH * D_MODEL)),
                    _const_spec((1, N_BRANCH * D_MODEL)), _const_spec((D_MODEL, D_MODEL))],
        out_specs=pl.BlockSpec((TB, D_MODEL), lambda i: (i, 0)),
        out_shape=jax.ShapeDtypeStruct((n_tok, D_MODEL), F32),
        compiler_params=_cparams("arbitrary"),
        name="merge_lat" if latent else "merge_ctx",
    )(x, mod_l, lp["norm1_g"], *[a for a, _ in branches], lp["w_branch"], lp["w_gate"], lp["b_gate"], lp["w_out"])


def _top_desc(s, k, with_rank=False):
    rows = []
    cur = s
    rank = jnp.full(s.shape, float(k), F32) if with_rank else None
    for i in range(k):
        m = jnp.max(cur, axis=0, keepdims=True)
        rows.append(m)
        hit = cur == m
        if with_rank:
            rank = jnp.where(hit, float(i), rank)
        cur = jnp.where(hit, NEG, cur)
    top = jnp.concatenate(rows, axis=0)
    return (top, rank) if with_rank else top


def _peer_score_kernel(x_ref, mod_ref, n2_ref, wqt_ref, sk_ref, ht_ref, e1_ref, cnt_ref, rk_ref, e2_ref):
    mod = mod_ref[0]
    h2 = _rms_mod(x_ref[...], n2_ref[...], mod[:, 4 * D_MODEL:5 * D_MODEL], mod[:, 3 * D_MODEL:4 * D_MODEL])
    htb = h2.T.astype(BF16)
    ht_ref[...] = htb
    qt = jnp.dot(wqt_ref[...], htb, preferred_element_type=F32)
    k1 = PEER_TOPK + 1
    half = PEER_TOPK // 2
    row = lax.broadcasted_iota(jnp.int32, (k1, 1), 0)
    for h in range(PEER_HEADS):
        s1, s2 = (jnp.dot(sk_ref[p], qt[(2 * h + p) * PEER_KEYS:(2 * h + p + 1) * PEER_KEYS, :].astype(BF16),
                          preferred_element_type=F32) for p in range(2))
        t1 = _top_desc(s1, k1)
        t2, rk2 = _top_desc(s2, k1, with_rank=True)
        cands = [t1[0:1] + t2, jnp.where(row >= 1, t1 + t2[0:1], NEG)]
        for a in range(1, half):
            nb = k1 // (a + 1)
            cands.append(jnp.where((row[0:half] >= 1) & (row[0:half] < nb), t1[a:a + 1] + t2[0:half], NEG))
        best = _top_desc(jnp.concatenate(cands, axis=0), k1)
        z = jnp.sum(jnp.exp(best[0:PEER_TOPK] - best[0:1]), axis=0, keepdims=True)
        thr = 0.5 * (best[PEER_TOPK - 1:PEER_TOPK] + best[PEER_TOPK:k1])
        need = thr - s1
        cnt = None
        for b in range(PEER_TOPK):
            ok = jnp.where(t2[b:b + 1] >= need, 1.0, 0.0)
            cnt = ok if cnt is None else cnt + ok
        e1_ref[h] = jnp.exp(s1 - t1[0:1]) * (0.5 / z)
        cnt_ref[h] = cnt
        rk_ref[h] = pltpu.bitcast(rk2.astype(BF16), jnp.int32)
        e2_ref[h] = pltpu.bitcast(jnp.exp(s2 - t2[0:1]).astype(BF16), jnp.int32)


def _peer_dense_kernel(ht_ref, e1_ref, cnt_ref, rk_ref, e2_ref, u_ref, v_ref, vlast_ref, x_ref, mod_ref, o_ref,
                       acc_ref, a_ref, g_ref, w_ref, wt_ref):
    e = pl.program_id(1)
    n_tok = acc_ref.shape[0]

    @pl.when(e == 0)
    def _():
        acc_ref[...] = jnp.zeros_like(acc_ref)
        w_ref[...] = jnp.zeros(w_ref.shape, jnp.int32)

    wt_ref[...] = pltpu.bitcast(pltpu.bitcast(w_ref[...], BF16).T, jnp.int32)

    zero = jnp.zeros((), BF16)
    for rr in range(PEER_ROWS):
        r = (e % (SUBLANES // PEER_ROWS)) * PEER_ROWS + rr
        packed_rows = slice(rr * (PEER_KEYS // 2), (rr + 1) * (PEER_KEYS // 2))
        def tile_row(ref, h):
            t16 = jnp.broadcast_to(ref[h, pl.ds(r, 1), :], (BF16_ROWS, n_tok)).astype(BF16)
            return jnp.concatenate([t16] * (PEER_KEYS // BF16_ROWS), axis=0)
        cnt_rows = [tile_row(cnt_ref, h) for h in range(PEER_HEADS)]
        e1_rows = [tile_row(e1_ref, h) for h in range(PEER_HEADS)]
        for lt in range(n_tok // 128):
            lanes = slice(lt * 128, (lt + 1) * 128)
            g = None
            for h in range(PEER_HEADS):
                sel = pltpu.bitcast(rk_ref[h, :, lanes], BF16) < cnt_rows[h][:, lanes]
                term = jnp.where(sel, pltpu.bitcast(e2_ref[h, :, lanes], BF16) * e1_rows[h][:, lanes], zero)
                g = term if g is None else g + term
            g_ref[packed_rows, lanes] = pltpu.bitcast(g, jnp.int32)
    a_ref[...] = jnp.dot(u_ref[...], ht_ref[...], preferred_element_type=F32)
    for rr in range(PEER_ROWS):
        rows = slice(rr * PEER_KEYS, (rr + 1) * PEER_KEYS)
        packed_rows = slice(rr * (PEER_KEYS // 2), (rr + 1) * (PEER_KEYS // 2))
        for lt in range(n_tok // 128):
            lanes = slice(lt * 128, (lt + 1) * 128)
            a = a_ref[rows, lanes]
            t = jnp.tanh(a * (0.7978845608028654 + (0.7978845608028654 * 0.044715) * (a * a)))
            w_ref[packed_rows, lanes] = pltpu.bitcast(
                pltpu.bitcast(g_ref[packed_rows, lanes], BF16) * (a + a * t).astype(BF16), jnp.int32)
    acc_ref[...] += jnp.dot(pltpu.bitcast(wt_ref[...], BF16), v_ref[...], preferred_element_type=F32)

    @pl.when(e == pl.num_programs(1) - 1)
    def _():
        acc = acc_ref[...] + jnp.dot(pltpu.bitcast(w_ref[...], BF16).T, vlast_ref[...], preferred_element_type=F32)
        o_ref[...] = x_ref[...] + mod_ref[0][:, 5 * D_MODEL:6 * D_MODEL] * acc


def _peer(x, mod_l, lp, latent):
    n_tok = x.shape[0]

    def mod_spec(tb):
        if latent:
            return pl.BlockSpec((1, 1, 6 * D_MODEL), lambda i, *_: (1 + i // (DEC_SEQ // tb), 0, 0))
        return pl.BlockSpec((1, 1, 6 * D_MODEL), lambda i, *_: (0, 0, 0))

    n_keys2 = PEER_HEADS * PEER_QDIM
    f32_tab = lambda n: (PEER_HEADS, PEER_KEYS, n)
    bf16_tab = lambda n: (PEER_HEADS, PEER_KEYS // 2, n)
    ht, *tabs = pl.pallas_call(
        _peer_score_kernel,
        grid=(n_tok // TBS,),
        in_specs=[pl.BlockSpec((TBS, D_MODEL), lambda i: (i, 0)), mod_spec(TBS), _const_spec((1, D_MODEL)),
                  _const_spec((n_keys2, D_MODEL)), _const_spec((2, PEER_KEYS, PEER_QDIM // 2))],
        out_specs=[pl.BlockSpec((D_MODEL, TBS), lambda i: (0, i))]
                  + [pl.BlockSpec(f32_tab(TBS), lambda i: (0, 0, i))] * 2
                  + [pl.BlockSpec(bf16_tab(TBS), lambda i: (0, 0, i))] * 2,
        out_shape=[jax.ShapeDtypeStruct((D_MODEL, n_tok), BF16),
                   jax.ShapeDtypeStruct(f32_tab(n_tok), F32), jax.ShapeDtypeStruct(f32_tab(n_tok), F32),
                   jax.ShapeDtypeStruct(bf16_tab(n_tok), jnp.int32), jax.ShapeDtypeStruct(bf16_tab(n_tok), jnp.int32)],
        compiler_params=_cparams("arbitrary"),
        name="peer_scores_lat" if latent else "peer_scores_ctx",
    )(x, mod_l, lp["norm2_g"], lp["peer_wqt"], lp["peer_subkeys"])
    nb = n_tok // TBP
    n_et = PEER_KEYS * PEER_KEYS // PEER_ET
    row_spec = pl.BlockSpec((PEER_HEADS, SUBLANES, TBP), lambda i, e: (0, e // (SUBLANES // PEER_ROWS), i))
    return pl.pallas_call(
        _peer_dense_kernel,
        grid=(nb, n_et),
        in_specs=[pl.BlockSpec((D_MODEL, TBP), lambda i, e: (0, i)), row_spec, row_spec]
                 + [pl.BlockSpec(bf16_tab(TBP), lambda i, e: (0, 0, i))] * 2
                 + [pl.BlockSpec((PEER_ET, D_MODEL), lambda i, e: (e, 0)),
                    pl.BlockSpec((PEER_ET, D_MODEL), lambda i, e: (jnp.maximum(e - 1, 0), 0)),
                    pl.BlockSpec((PEER_ET, D_MODEL), lambda i, e: (jnp.where(e == n_et - 1, n_et - 1, 0), 0)),
                    pl.BlockSpec((TBP, D_MODEL), lambda i, e: (i, 0)),
                    mod_spec(TBP)],
        out_specs=pl.BlockSpec((TBP, D_MODEL), lambda i, e: (i, 0)),
        out_shape=jax.ShapeDtypeStruct((n_tok, D_MODEL), F32),
        scratch_shapes=[pltpu.VMEM((TBP, D_MODEL), F32), pltpu.VMEM((PEER_ET, TBP), F32),
                        pltpu.VMEM((PEER_ET // 2, TBP), jnp.int32), pltpu.VMEM((PEER_ET // 2, TBP), jnp.int32),
                        pltpu.VMEM((TBP // 2, PEER_ET), jnp.int32)],
        compiler_params=_cparams("arbitrary", "arbitrary"),
        name="peer_dense_lat" if latent else "peer_dense_ctx",
    )(ht, *tabs, lp["peer_u"], lp["peer_v"], lp["peer_v"], x, mod_l)


def _layer_params(l, w):
    tile = lambda g, n: jnp.tile(g, n)
    gains = jnp.stack([
        tile(w["nat_qk_g"][l, 0], 4), tile(w["nat_qk_g"][l, 1], 4),
        tile(w["gqa_qk_g"][l, 0], 4), tile(w["gqa_qk_g"][l, 1], 4),
        tile(w["diff_qk_g"][l, 0], 8), tile(w["diff_qk_g"][l, 1], 8),
        jnp.zeros((256,), F32), jnp.zeros((256,), F32)])
    return dict(
        norm1_g=w["norm1_g"][l].reshape(1, D_MODEL),
        norm2_g=w["norm2_g"][l].reshape(1, D_MODEL),
        w_in=w["w_in"][l].astype(BF16),
        qk_gains=gains,
        bd64=_block_ones(256, 64), bd32=_block_ones(256, 32), bd256=_block_ones(256, 256),
        rope=w["rope"],
        sgu_norm_g=w["sgu_norm_g"][l].reshape(1, SGU_WIDTH),
        sgu_w=w["sgu_w"][l].astype(BF16),
        sgu_b=jnp.repeat(w["sgu_b"][l].T, SGU_WIDTH // SGU_GROUPS, axis=1),
        diff_lambda=w["diff_lambda"][l],
        diff_sub_g=tile(w["diff_sub_g"][l], 4).reshape(1, 256),
        nat_bias=_nat_bias_table(w["nat_rpb"][l]),
        w_branch=w["w_branch"][l].astype(BF16),
        w_gate=w["w_gate"][l].astype(BF16),
        b_gate=w["b_gate"][l].reshape(1, N_BRANCH * D_MODEL),
        w_out=w["w_out"][l].astype(BF16),
        peer_wqt=w["peer_wq"][l].T.astype(BF16),
        peer_subkeys=w["peer_subkeys"][l].astype(BF16),
        peer_u=w["peer_u"][l].astype(BF16),
        peer_v=w["peer_v"][l].astype(BF16),
    )


def kernel(x_prompt, x_sample, c, cache_nat_k, cache_nat_v, cache_gqa_k, cache_gqa_v, cache_diff_k, cache_diff_v, c_ctx, w_mod, b_mod, norm1_g, norm2_g, w_in, nat_qk_g, nat_rpb, gqa_qk_g, diff_qk_g, diff_lambda, diff_sub_g, sgu_norm_g, sgu_w, sgu_b, w_branch, w_gate, b_gate, w_out, peer_wq, peer_subkeys, peer_u, peer_v):
    w = dict(norm1_g=norm1_g, norm2_g=norm2_g, w_in=w_in, nat_qk_g=nat_qk_g, nat_rpb=nat_rpb,
             gqa_qk_g=gqa_qk_g, diff_qk_g=diff_qk_g, diff_lambda=diff_lambda, diff_sub_g=diff_sub_g,
             sgu_norm_g=sgu_norm_g, sgu_w=sgu_w, sgu_b=sgu_b, w_branch=w_branch, w_gate=w_gate,
             b_gate=b_gate, w_out=w_out, peer_wq=peer_wq, peer_subkeys=peer_subkeys, peer_u=peer_u,
             peer_v=peer_v, rope=_rope_tables())
    cvec = jnp.concatenate([c_ctx[None], c, jnp.zeros((MOD_ROWS - 1 - DEC_BATCH, D_MODEL), F32)], axis=0)
    mod = _modulation(cvec, w_mod, b_mod).reshape(DEPTH, MOD_ROWS, 1, 6 * D_MODEL)
    xp = x_prompt.reshape(N_CTX, D_MODEL)
    xs = x_sample.reshape(N_LAT, D_MODEL)
    new = []
    for l in range(DEPTH):
        lp = _layer_params(l, w)
        lam_init = 0.8 - 0.6 * math.exp(-0.3 * l)
        caches = (cache_nat_k, cache_nat_v, cache_gqa_k, cache_gqa_v, cache_diff_k, cache_diff_v)
        proj = _in_projection(xp, mod[l], lp, latent=False)
        new.append(proj)
        attn = _ctx_attention(proj, lp, lam_init)
        xp = _merge(xp, mod[l], lp, [(attn, 0), (attn, 1), (attn, 2), (proj, _COL["sgu"])], latent=False)
        xp = _peer(xp, mod[l], lp, latent=False)
        proj = _in_projection(xs, mod[l], lp, latent=True)
        o_nat, o_gqa, o_dif = _lat_attention(proj, caches, lp, l, lam_init)
        xs = _merge(xs, mod[l], lp, [(o_nat, 0), (o_gqa, 0), (o_dif, 0), (proj, _COL["sgu"])], latent=True)
        xs = _peer(xs, mod[l], lp, latent=True)

    def cache_out(lo, width, tail):
        a = jnp.stack([p[:, lo:lo + width].reshape(BATCH, SEQ, width) for p in new], axis=1)
        return a.reshape((BATCH, DEPTH, SEQ) + tail)

    return (xp.reshape(BATCH, SEQ, D_MODEL), xs.reshape(DEC_BATCH, DEC_SEQ, D_MODEL),
            cache_out(256, 256, (NAT_HEADS, HEAD_DIM)), cache_out(512, 256, (NAT_HEADS, HEAD_DIM)),
            cache_out(1024, 128, (GQA_KV_HEADS, HEAD_DIM)), cache_out(1152, 128, (GQA_KV_HEADS, HEAD_DIM)),
            cache_out(1536, 256, (DIFF_HEADS, 2, DIFF_DIM)), cache_out(1792, 256, (DIFF_HEADS, HEAD_DIM)))
```

```python
import functools
import math

import numpy as np
import jax
import jax.numpy as jnp
from jax import lax
from jax.experimental import pallas as pl
from jax.experimental.pallas import tpu as pltpu

F32 = jnp.float32
BF16 = jnp.bfloat16

D_MODEL = 1024
BATCH = 16
SEQ = 256
DEPTH = 2
DEC_BATCH = 2
DEC_SEQ = 4096
PAST_LEN = 512
GRID_W = 64
HEAD_DIM = 64
NAT_HEADS = 4
NAT_ROWS = 8
NAT_COLS = 16
GQA_HEADS = 4
GQA_KV_HEADS = 2
DIFF_HEADS = 4
DIFF_DIM = 32
SGU_WIDTH = 256
SGU_GROUPS = 4
CHUNK = 128
N_BRANCH = 4
BRANCH_W = 256
IN_W = 2560
PEER_HEADS = 8
PEER_KEYS = 128
PEER_QDIM = 256
PEER_TOPK = 16
ROPE_BASE = 10000.0
EPS = 1e-6

N_CTX = BATCH * SEQ
N_LAT = DEC_BATCH * DEC_SEQ
MOD_ROWS = 8
TB = 256
TQ = 256
PROJ_W = 2304
NAT_QROWS = 4
NAT_KROWS = NAT_QROWS + NAT_ROWS
TBS = 512
TBP = 1024
PEER_ET = 512
PEER_ROWS = PEER_ET // PEER_KEYS
SUBLANES = 8
BF16_ROWS = 16
VMEM_LIMIT = 56 * 1024 * 1024
NEG = -1e30


def _cparams(*sem):
    return pltpu.CompilerParams(dimension_semantics=sem, vmem_limit_bytes=VMEM_LIMIT)


def _const_spec(shape):
    return pl.BlockSpec(shape, lambda *_: (0,) * len(shape))


def _gelu(x):
    return 0.5 * x * (1.0 + jnp.tanh(0.7978845608028654 * (x + 0.044715 * (x * x * x))))


def _sigmoid(x):
    return 1.0 / (1.0 + jnp.exp(-x))


def _dot(a, b):
    return jnp.dot(a.astype(BF16), b.astype(BF16), preferred_element_type=F32)


def _dot_nt(a, b):
    return lax.dot_general(a.astype(BF16), b.astype(BF16), (((1,), (1,)), ((), ())),
                           preferred_element_type=F32)


def _group_mean(y2, ones_bd):
    hi = y2.astype(BF16)
    lo = (y2 - hi.astype(F32)).astype(BF16)
    return (jnp.dot(hi, ones_bd, preferred_element_type=F32)
            + jnp.dot(lo, ones_bd, preferred_element_type=F32))


def _block_ones(width, group):
    idx = np.arange(width) // group
    return jnp.asarray((idx[:, None] == idx[None, :]).astype(np.float32) / group, dtype=BF16)


def _rms_mod(x, gain, scale, shift):
    xn = x * lax.rsqrt(jnp.mean(x * x, axis=-1, keepdims=True) + EPS) * gain
    return xn * (1.0 + scale) + shift


def _mod_spec(latent):
    if latent:
        return pl.BlockSpec((1, 1, 6 * D_MODEL), lambda i, *_: (1 + i // (DEC_SEQ // TB), 0, 0))
    return pl.BlockSpec((1, 1, 6 * D_MODEL), lambda i, *_: (0, 0, 0))


def _mod_kernel(c_ref, w_ref, b_ref, o_ref):
    c = c_ref[...]
    s = c * _sigmoid(c)
    o_ref[0] = _dot(s, w_ref[0]) + b_ref[0]


def _modulation(cvec, w_mod, b_mod):
    tn = 1536
    return pl.pallas_call(
        _mod_kernel,
        grid=(DEPTH, 6 * D_MODEL // tn),
        in_specs=[pl.BlockSpec((MOD_ROWS, D_MODEL), lambda l, j: (0, 0)),
                  pl.BlockSpec((1, D_MODEL, tn), lambda l, j: (l, 0, j)),
                  pl.BlockSpec((1, 1, tn), lambda l, j: (l, 0, j))],
        out_specs=pl.BlockSpec((1, MOD_ROWS, tn), lambda l, j: (l, 0, j)),
        out_shape=jax.ShapeDtypeStruct((DEPTH, MOD_ROWS, 6 * D_MODEL), F32),
        compiler_params=_cparams("arbitrary", "arbitrary"),
        name="modulation",
    )(cvec, w_mod, b_mod.reshape(DEPTH, 1, 6 * D_MODEL))


def _rope_tables():
    t = np.arange(DEC_SEQ)
    pos = (t // GRID_W, t % GRID_W)
    out = []
    for d in (HEAD_DIM, DIFF_DIM):
        half, quarter = d // 2, d // 4
        inv = ROPE_BASE ** (-np.arange(quarter, dtype=np.float32) * 2.0 / half)
        lane = np.arange(128) % d
        part, j = lane // half, lane % half
        ang = np.stack([pos[0][:, None] * inv[None, :], pos[1][:, None] * inv[None, :]], axis=1)
        a = ang[:, part, j % quarter].astype(np.float32)
        cos, sin = np.cos(a), np.sin(a)
        out += [jnp.asarray(cos, dtype=F32),
                jnp.asarray(np.where(j < quarter, -sin, 0.0), dtype=F32),
                jnp.asarray(np.where(j >= quarter, sin, 0.0), dtype=F32)]
    return out


def _rope(y, c, sa, sb, quarter):
    w = y.shape[-1]
    rep = w // 128
    if rep > 1:
        c, sa, sb = (jnp.concatenate([t] * rep, axis=-1) for t in (c, sa, sb))
    up = pltpu.roll(y, w - quarter, 1)
    dn = pltpu.roll(y, quarter, 1)
    return y * c + up * sa + dn * sb


def _inproj_kernel(*refs, latent):
    (x_ref, mod_ref, n1_ref, w_ref, g_ref, bd64_ref, bd32_ref, bd256_ref,
     sgn_ref, sgw_ref, sgb_ref) = refs[:11]
    o_ref = refs[-1]
    mod = mod_ref[0]
    h = _rms_mod(x_ref[...], n1_ref[...], mod[:, D_MODEL:2 * D_MODEL], mod[:, 0:D_MODEL])
    y = jnp.dot(h.astype(BF16), w_ref[...], preferred_element_type=F32)

    def qk_norm(lo, width, bd, gain_row):
        v = y[:, lo:lo + width]
        ms = _group_mean(v * v, bd)
        return v * lax.rsqrt(ms + EPS) * g_ref[gain_row:gain_row + 1, 0:width]

    bd64 = bd64_ref[...]
    bd32 = bd32_ref[...]
    if latent:
        c64, sa64, sb64, c32, sa32, sb32 = (r[...] for r in refs[11:17])
        rope64 = functools.partial(_rope, c=c64, sa=sa64, sb=sb64, quarter=HEAD_DIM // 4)
        rope32 = functools.partial(_rope, c=c32, sa=sa32, sb=sb32, quarter=DIFF_DIM // 4)
    else:
        rope64 = rope32 = lambda v: v
    o_ref[:, 0:256] = qk_norm(0, 256, bd64, 0)
    o_ref[:, 256:512] = qk_norm(256, 256, bd64, 1)
    o_ref[:, 512:768] = y[:, 512:768]
    o_ref[:, 768:1024] = rope64(qk_norm(768, 256, bd64, 2))
    o_ref[:, 1024:1152] = rope64(qk_norm(1024, 128, bd64[0:128, 0:128], 3))
    o_ref[:, 1152:1280] = y[:, 1152:1280]
    o_ref[:, 1280:1536] = rope32(qk_norm(1280, 256, bd32, 4))
    o_ref[:, 1536:1792] = rope32(qk_norm(1536, 256, bd32, 5))
    o_ref[:, 1792:2048] = y[:, 1792:2048]
    u = _gelu(y[:, 2048:2304])
    v = _gelu(y[:, 2304:2560])
    vn = v * lax.rsqrt(_group_mean(v * v, bd256_ref[...]) + EPS) * sgn_ref[...]
    vnb = vn.astype(BF16)
    lane_group = lax.broadcasted_iota(jnp.int32, (CHUNK, SGU_WIDTH), 1) // (SGU_WIDTH // SGU_GROUPS)
    for n in range(TB // CHUNK):
        vc = vnb[n * CHUNK:(n + 1) * CHUNK, :]
        s = sgb_ref[...]
        for g in range(SGU_GROUPS):
            sg = jnp.dot(sgw_ref[g], vc, preferred_element_type=F32)
            s = s + jnp.where(lane_group == g, sg, 0.0)
        o_ref[n * CHUNK:(n + 1) * CHUNK, 2048:2304] = u[n * CHUNK:(n + 1) * CHUNK, :] * s


def _in_projection(x, mod_l, lp, latent):
    n_tok = x.shape[0]
    in_specs = [pl.BlockSpec((TB, D_MODEL), lambda i: (i, 0)),
                _mod_spec(latent),
                _const_spec((1, D_MODEL)),
                _const_spec((D_MODEL, IN_W)),
                _const_spec((8, 256)),
                _const_spec((256, 256)), _const_spec((256, 256)), _const_spec((256, 256)),
                _const_spec((1, SGU_WIDTH)),
                _const_spec((SGU_GROUPS, CHUNK, CHUNK)),
                _const_spec((CHUNK, SGU_WIDTH))]
    args = [x, mod_l, lp["norm1_g"], lp["w_in"], lp["qk_gains"], lp["bd64"], lp["bd32"], lp["bd256"],
            lp["sgu_norm_g"], lp["sgu_w"], lp["sgu_b"]]
    if latent:
        in_specs += [pl.BlockSpec((TB, 128), lambda i: (i % (DEC_SEQ // TB), 0))] * 6
        args += lp["rope"]
    return pl.pallas_call(
        functools.partial(_inproj_kernel, latent=latent),
        grid=(n_tok // TB,),
        in_specs=in_specs,
        out_specs=pl.BlockSpec((TB, PROJ_W), lambda i: (i, 0)),
        out_shape=jax.ShapeDtypeStruct((n_tok, PROJ_W), F32),
        compiler_params=_cparams("arbitrary"),
        name="in_projection_lat" if latent else "in_projection_ctx",
    )(*args)


LOG2E = 1.4426950408889634


def _softmax_parts(scores):
    m = functools.reduce(jnp.maximum, [jnp.max(s, axis=-1, keepdims=True) for s in scores])
    ps = [jnp.exp2(s - m) for s in scores]
    l = functools.reduce(jnp.add, [jnp.sum(p, axis=-1, keepdims=True) for p in ps])
    return [p.astype(BF16) for p in ps], 1.0 / l


def _diff_lambda(lam_ref, lam_init):
    lv = lam_ref[...]
    a = jnp.sum(lv[0:1] * lv[1:2], axis=-1, keepdims=True)
    b = jnp.sum(lv[2:3] * lv[3:4], axis=-1, keepdims=True)
    return jnp.exp(a) - jnp.exp(b) + lam_init


def _head(ref_or_val, h, width=HEAD_DIM):
    return ref_or_val[:, h * width:(h + 1) * width]


def _mha(q, ks, vs, n_heads, kv_group, scale, biases=None):
    ks, vs = [k.astype(BF16) for k in ks], [v.astype(BF16) for v in vs]
    qs = (q * (scale * LOG2E)).astype(BF16)

    def scores_of(h):
        scores = [_dot_nt(_head(qs, h), _head(k, h // kv_group)) for k in ks]
        if biases is not None:
            scores = [s if b is None else s + b[h] * LOG2E for s, b in zip(scores, biases)]
        return scores

    def finish(h, ps, rl):
        return functools.reduce(jnp.add, [_dot(p, _head(v, h // kv_group)) for p, v in zip(ps, vs)]) * rl

    return jnp.concatenate(_softmax_pipeline(n_heads, scores_of, finish), axis=-1)


def _softmax_pipeline(n, scores_of, finish):
    outs = []
    scores = scores_of(0)
    for i in range(n):
        ps, rl = _softmax_parts(scores)
        if i + 1 < n:
            scores = scores_of(i + 1)
        outs.append(finish(i, ps, rl))
    return outs


def _diff_attn(q, ks, vs, lam, sub_gain, bd64, lam_init):
    qs = (q * (DIFF_DIM ** -0.5 * LOG2E)).astype(BF16)
    ks, vs = [k.astype(BF16) for k in ks], [v.astype(BF16) for v in vs]
    scores_of = lambda i: [_dot_nt(_head(qs, i, DIFF_DIM), _head(k, i, DIFF_DIM)) for k in ks]
    finish = lambda i, ps, rl: functools.reduce(jnp.add, [_dot(p, _head(v, i // 2)) for p, v in zip(ps, vs)]) * rl
    pv = _softmax_pipeline(2 * DIFF_HEADS, scores_of, finish)
    o = jnp.concatenate([pv[2 * h] - lam * pv[2 * h + 1] for h in range(DIFF_HEADS)], axis=-1)
    ms = _group_mean(o * o, bd64)
    return o * lax.rsqrt(ms + EPS) * sub_gain * (1.0 - lam_init)


def _ctx_attn_kernel(nq, nk, nv, gq, gk, gv, dq, dk, dv, lam_ref, subg_ref, bd64_ref, o_ref, *, lam_init):
    scale = HEAD_DIM ** -0.5
    o_ref[:, 0:256] = _mha(nq[...], [nk[...]], [nv[...]], NAT_HEADS, 1, scale)
    o_ref[:, 256:512] = _mha(gq[...], [gk[...]], [gv[...]], GQA_HEADS, GQA_HEADS // GQA_KV_HEADS, scale)
    lam = _diff_lambda(lam_ref, lam_init)
    o_ref[:, 512:768] = _diff_attn(dq[...], [dk[...]], [dv[...]], lam, subg_ref[...], bd64_ref[...], lam_init)


_COL = dict(nq=0, nk=1, nv=2, gq=3, gk=8, gv=9, dq=5, dk=6, dv=7, sgu=8)


def _ctx_attention(proj, lp, lam_init):
    blk = lambda name, w: pl.BlockSpec((SEQ, w), lambda b: (b, _COL[name]))
    return pl.pallas_call(
        functools.partial(_ctx_attn_kernel, lam_init=lam_init),
        grid=(BATCH,),
        in_specs=[blk("nq", 256), blk("nk", 256), blk("nv", 256),
                  blk("gq", 256), blk("gk", 128), blk("gv", 128),
                  blk("dq", 256), blk("dk", 256), blk("dv", 256),
                  _const_spec((4, DIFF_DIM)), _const_spec((1, 256)), _const_spec((256, 256))],
        out_specs=pl.BlockSpec((SEQ, 768), lambda b: (b, 0)),
        out_shape=jax.ShapeDtypeStruct((N_CTX, 768), F32),
        compiler_params=_cparams("arbitrary"),
        name="ctx_attention",
    )(*([proj] * 9), lp["diff_lambda"], lp["diff_sub_g"], lp["bd64"])


def _gqa_lat_kernel(q_ref, k_ref, v_ref, ck_ref, cv_ref, o_ref):
    o_ref[...] = _mha(q_ref[...], [k_ref[...], ck_ref[...]], [v_ref[...], cv_ref[...]],
                      GQA_HEADS, GQA_HEADS // GQA_KV_HEADS, HEAD_DIM ** -0.5)


def _diff_lat_kernel(q_ref, k_ref, v_ref, ck_ref, cv_ref, lam_ref, subg_ref, bd64_ref, o_ref, *, lam_init):
    lam = _diff_lambda(lam_ref, lam_init)
    o_ref[...] = _diff_attn(q_ref[...], [k_ref[...], ck_ref[...]], [v_ref[...], cv_ref[...]],
                            lam, subg_ref[...], bd64_ref[...], lam_init)


def _nat_lat_kernel(q_ref, k_ref, v_ref, ck_ref, cv_ref, bias_ref, o_ref):
    i = pl.program_id(1)
    k_row0 = jnp.clip(NAT_QROWS * i - NAT_ROWS // 2, 0, GRID_W - NAT_KROWS)
    start = pl.multiple_of(k_row0 * GRID_W, GRID_W)
    kw = k_ref[pl.ds(start, NAT_KROWS * GRID_W), :]
    vw = v_ref[pl.ds(start, NAT_KROWS * GRID_W), :]
    o_ref[...] = _mha(q_ref[...], [kw, ck_ref[...]], [vw, cv_ref[...]], NAT_HEADS, 1, HEAD_DIM ** -0.5,
                      biases=[bias_ref[0], None])


def _nat_bias_table(rpb):
    rows = DEC_SEQ // GRID_W
    nblk = rows // NAT_QROWS
    pad = jnp.pad(rpb, ((0, 0), (0, 0), (GRID_W - NAT_COLS, GRID_W - NAT_COLS)))
    toep = jnp.stack([pad[:, :, GRID_W - 1 - c:2 * GRID_W - 1 - c] for c in range(GRID_W)], axis=2)
    col = np.arange(GRID_W)
    cs = np.clip(col - NAT_COLS // 2, 0, GRID_W - NAT_COLS)
    col_ok = (col[None, :] >= cs[:, None]) & (col[None, :] < cs[:, None] + NAT_COLS)
    toep = jnp.where(col_ok, toep, NEG)
    masked = jnp.full((NAT_HEADS, GRID_W, GRID_W), NEG, F32)
    cases = []
    for blk in (0, 1, nblk - 1):
        r0 = blk * NAT_QROWS
        k0 = int(np.clip(r0 - NAT_ROWS // 2, 0, rows - NAT_KROWS))
        q_rows = []
        for qr in range(r0, r0 + NAT_QROWS):
            rs = int(np.clip(qr - NAT_ROWS // 2, 0, rows - NAT_ROWS))
            q_rows.append(jnp.concatenate(
                [toep[:, kr - qr + NAT_ROWS - 1] if rs <= kr < rs + NAT_ROWS else masked
                 for kr in range(k0, k0 + NAT_KROWS)], axis=-1))
        cases.append(jnp.concatenate(q_rows, axis=-2))
    return jnp.stack(cases, axis=0)


def _lat_attention(proj, caches, lp, l, lam_init):
    cnk, cnv, cgk, cgv, cdk, cdv = caches
    nq_blocks = DEC_SEQ // TQ
    qspec = lambda name: pl.BlockSpec((TQ, 256), lambda b, i: (b * nq_blocks + i, _COL[name]))
    kvspec = lambda name, w: pl.BlockSpec((DEC_SEQ, w), lambda b, i: (b, _COL[name]))
    cspec = lambda w: pl.BlockSpec((None, None, PAST_LEN, w), lambda b, i: (b, l, 0, 0))
    ospec = pl.BlockSpec((TQ, 256), lambda b, i: (b * nq_blocks + i, 0))
    oshape = jax.ShapeDtypeStruct((N_LAT, 256), F32)
    o_gqa = pl.pallas_call(
        _gqa_lat_kernel,
        grid=(DEC_BATCH, nq_blocks),
        in_specs=[qspec("gq"), kvspec("gk", 128), kvspec("gv", 128), cspec(128), cspec(128)],
        out_specs=ospec, out_shape=oshape,
        compiler_params=_cparams("arbitrary", "arbitrary"),
        name="gqa_lat_attention",
    )(proj, proj, proj, cgk.reshape(DEC_BATCH, DEPTH, PAST_LEN, 128), cgv.reshape(DEC_BATCH, DEPTH, PAST_LEN, 128))
    o_dif = pl.pallas_call(
        functools.partial(_diff_lat_kernel, lam_init=lam_init),
        grid=(DEC_BATCH, nq_blocks),
        in_specs=[qspec("dq"), kvspec("dk", 256), kvspec("dv", 256), cspec(256), cspec(256),
                  _const_spec((4, DIFF_DIM)), _const_spec((1, 256)), _const_spec((256, 256))],
        out_specs=ospec, out_shape=oshape,
        compiler_params=_cparams("arbitrary", "arbitrary"),
        name="diff_lat_attention",
    )(proj, proj, proj, cdk.reshape(DEC_BATCH, DEPTH, PAST_LEN, 256), cdv.reshape(DEC_BATCH, DEPTH, PAST_LEN, 256),
      lp["diff_lambda"], lp["diff_sub_g"], lp["bd64"])
    nblk = DEC_SEQ // (NAT_QROWS * GRID_W)
    nat_q = NAT_QROWS * GRID_W
    o_nat = pl.pallas_call(
        _nat_lat_kernel,
        grid=(DEC_BATCH, nblk),
        in_specs=[pl.BlockSpec((nat_q, 256), lambda b, i: (b * nblk + i, _COL["nq"])),
                  kvspec("nk", 256), kvspec("nv", 256), cspec(256), cspec(256),
                  pl.BlockSpec((1, NAT_HEADS, nat_q, NAT_KROWS * GRID_W),
                               lambda b, i: (jnp.where(i == 0, 0, jnp.where(i == nblk - 1, 2, 1)), 0, 0, 0))],
        out_specs=pl.BlockSpec((nat_q, 256), lambda b, i: (b * nblk + i, 0)),
        out_shape=oshape,
        compiler_params=_cparams("arbitrary", "arbitrary"),
        name="nat_lat_attention",
    )(proj, proj, proj, cnk.reshape(DEC_BATCH, DEPTH, PAST_LEN, 256), cnv.reshape(DEC_BATCH, DEPTH, PAST_LEN, 256),
      lp["nat_bias"])
    return o_nat, o_gqa, o_dif


def _merge_kernel(x_ref, mod_ref, n1_ref, b0_ref, b1_ref, b2_ref, b3_ref, wb_ref, wg_ref, bg_ref, wo_ref, o_ref):
    x = x_ref[...]
    mod = mod_ref[0]
    hb = _rms_mod(x, n1_ref[...], mod[:, D_MODEL:2 * D_MODEL], mod[:, 0:D_MODEL]).astype(BF16)
    merged = None
    for n, b_ref in enumerate((b0_ref, b1_ref, b2_ref, b3_ref)):
        cols = slice(n * D_MODEL, (n + 1) * D_MODEL)
        gate = _sigmoid(jnp.dot(hb, wg_ref[:, cols], preferred_element_type=F32) + bg_ref[:, cols])
        term = gate * _dot(b_ref[...], wb_ref[n])
        merged = term if merged is None else merged + term
    out = _dot(merged, wo_ref[...])
    o_ref[...] = x + mod[:, 2 * D_MODEL:3 * D_MODEL] * out


def _merge(x, mod_l, lp, branches, latent):
    n_tok = x.shape[0]
    return pl.pallas_call(
        _merge_kernel,
        grid=(n_tok // TB,),
        in_specs=[pl.BlockSpec((TB, D_MODEL), lambda i: (i, 0)), _mod_spec(latent), _const_spec((1, D_MODEL))]
                 + [pl.BlockSpec((TB, BRANCH_W), functools.partial(lambda i, c: (i, c), c=col)) for _, col in branches]
                 + [_const_spec((N_BRANCH, BRANCH_W, D_MODEL)), _const_spec((D_MODEL, N_BRANCH * D_MODEL)),
                    _const_spec((1, N_BRANCH * D_MODEL)), _const_spec((D_MODEL, D_MODEL))],
        out_specs=pl.BlockSpec((TB, D_MODEL), lambda i: (i, 0)),
        out_shape=jax.ShapeDtypeStruct((n_tok, D_MODEL), F32),
        compiler_params=_cparams("arbitrary"),
        name="merge_lat" if latent else "merge_ctx",
    )(x, mod_l, lp["norm1_g"], *[a for a, _ in branches], lp["w_branch"], lp["w_gate"], lp["b_gate"], lp["w_out"])


def _top_desc(s, k, with_rank=False):
    rows = []
    cur = s
    rank = jnp.full(s.shape, float(k), F32) if with_rank else None
    for i in range(k):
        m = jnp.max(cur, axis=0, keepdims=True)
        rows.append(m)
        hit = cur == m
        if with_rank:
            rank = jnp.where(hit, float(i), rank)
        cur = jnp.where(hit, NEG, cur)
    top = jnp.concatenate(rows, axis=0)
    return (top, rank) if with_rank else top


def _peer_score_kernel(x_ref, mod_ref, n2_ref, wqt_ref, sk_ref, ht_ref, e1_ref, cnt_ref, rk_ref, e2_ref):
    mod = mod_ref[0]
    h2 = _rms_mod(x_ref[...], n2_ref[...], mod[:, 4 * D_MODEL:5 * D_MODEL], mod[:, 3 * D_MODEL:4 * D_MODEL])
    htb = h2.T.astype(BF16)
    ht_ref[...] = htb
    qt = jnp.dot(wqt_ref[...], htb, preferred_element_type=F32)
    k1 = PEER_TOPK + 1
    half = PEER_TOPK // 2
    row = lax.broadcasted_iota(jnp.int32, (k1, 1), 0)
    for h in range(PEER_HEADS):
        s1, s2 = (jnp.dot(sk_ref[p], qt[(2 * h + p) * PEER_KEYS:(2 * h + p + 1) * PEER_KEYS, :].astype(BF16),
                          preferred_element_type=F32) for p in range(2))
        t1 = _top_desc(s1, k1)
        t2, rk2 = _top_desc(s2, k1, with_rank=True)
        cands = [t1[0:1] + t2, jnp.where(row >= 1, t1 + t2[0:1], NEG)]
        for a in range(1, half):
            nb = k1 // (a + 1)
            cands.append(jnp.where((row[0:half] >= 1) & (row[0:half] < nb), t1[a:a + 1] + t2[0:half], NEG))
        best = _top_desc(jnp.concatenate(cands, axis=0), k1)
        z = jnp.sum(jnp.exp(best[0:PEER_TOPK] - best[0:1]), axis=0, keepdims=True)
        thr = 0.5 * (best[PEER_TOPK - 1:PEER_TOPK] + best[PEER_TOPK:k1])
        need = thr - s1
        cnt = None
        for b in range(PEER_TOPK):
            ok = jnp.where(t2[b:b + 1] >= need, 1.0, 0.0)
            cnt = ok if cnt is None else cnt + ok
        e1_ref[h] = jnp.exp(s1 - t1[0:1]) * (0.5 / z)
        cnt_ref[h] = cnt
        rk_ref[h] = pltpu.bitcast(rk2.astype(BF16), jnp.int32)
        e2_ref[h] = pltpu.bitcast(jnp.exp(s2 - t2[0:1]).astype(BF16), jnp.int32)


def _peer_dense_kernel(ht_ref, e1_ref, cnt_ref, rk_ref, e2_ref, u_ref, v_ref, vlast_ref, x_ref, mod_ref, o_ref,
                       acc_ref, a_ref, g_ref, w_ref, wt_ref):
    e = pl.program_id(1)
    n_tok = acc_ref.shape[0]

    @pl.when(e == 0)
    def _():
        acc_ref[...] = jnp.zeros_like(acc_ref)
        w_ref[...] = jnp.zeros(w_ref.shape, jnp.int32)

    wt_ref[...] = pltpu.bitcast(pltpu.bitcast(w_ref[...], BF16).T, jnp.int32)

    zero = jnp.zeros((), BF16)
    for rr in range(PEER_ROWS):
        r = (e % (SUBLANES // PEER_ROWS)) * PEER_ROWS + rr
        packed_rows = slice(rr * (PEER_KEYS // 2), (rr + 1) * (PEER_KEYS // 2))
        def tile_row(ref, h):
            t16 = jnp.broadcast_to(ref[h, pl.ds(r, 1), :], (BF16_ROWS, n_tok)).astype(BF16)
            return jnp.concatenate([t16] * (PEER_KEYS // BF16_ROWS), axis=0)
        cnt_rows = [tile_row(cnt_ref, h) for h in range(PEER_HEADS)]
        e1_rows = [tile_row(e1_ref, h) for h in range(PEER_HEADS)]
        for lt in range(n_tok // 128):
            lanes = slice(lt * 128, (lt + 1) * 128)
            g = None
            for h in range(PEER_HEADS):
                sel = pltpu.bitcast(rk_ref[h, :, lanes], BF16) < cnt_rows[h][:, lanes]
                term = jnp.where(sel, pltpu.bitcast(e2_ref[h, :, lanes], BF16) * e1_rows[h][:, lanes], zero)
                g = term if g is None else g + term
            g_ref[packed_rows, lanes] = pltpu.bitcast(g, jnp.int32)
    a_ref[...] = jnp.dot(u_ref[...], ht_ref[...], preferred_element_type=F32)
    for rr in range(PEER_ROWS):
        rows = slice(rr * PEER_KEYS, (rr + 1) * PEER_KEYS)
        packed_rows = slice(rr * (PEER_KEYS // 2), (rr + 1) * (PEER_KEYS // 2))
        for lt in range(n_tok // 128):
            lanes = slice(lt * 128, (lt + 1) * 128)
            a = a_ref[rows, lanes]
            t = jnp.tanh(a * (0.7978845608028654 + (0.7978845608028654 * 0.044715) * (a * a)))
            w_ref[packed_rows, lanes] = pltpu.bitcast(
                pltpu.bitcast(g_ref[packed_rows, lanes], BF16) * (a + a * t).astype(BF16), jnp.int32)
    acc_ref[...] += jnp.dot(pltpu.bitcast(wt_ref[...], BF16), v_ref[...], preferred_element_type=F32)

    @pl.when(e == pl.num_programs(1) - 1)
    def _():
        acc = acc_ref[...] + jnp.dot(pltpu.bitcast(w_ref[...], BF16).T, vlast_ref[...], preferred_element_type=F32)
        o_ref[...] = x_ref[...] + mod_ref[0][:, 5 * D_MODEL:6 * D_MODEL] * acc


def _peer(x, mod_l, lp, latent):
    n_tok = x.shape[0]

    def mod_spec(tb):
        if latent:
            return pl.BlockSpec((1, 1, 6 * D_MODEL), lambda i, *_: (1 + i // (DEC_SEQ // tb), 0, 0))
        return pl.BlockSpec((1, 1, 6 * D_MODEL), lambda i, *_: (0, 0, 0))

    n_keys2 = PEER_HEADS * PEER_QDIM
    f32_tab = lambda n: (PEER_HEADS, PEER_KEYS, n)
    bf16_tab = lambda n: (PEER_HEADS, PEER_KEYS // 2, n)
    ht, *tabs = pl.pallas_call(
        _peer_score_kernel,
        grid=(n_tok // TBS,),
        in_specs=[pl.BlockSpec((TBS, D_MODEL), lambda i: (i, 0)), mod_spec(TBS), _const_spec((1, D_MODEL)),
                  _const_spec((n_keys2, D_MODEL)), _const_spec((2, PEER_KEYS, PEER_QDIM // 2))],
        out_specs=[pl.BlockSpec((D_MODEL, TBS), lambda i: (0, i))]
                  + [pl.BlockSpec(f32_tab(TBS), lambda i: (0, 0, i))] * 2
                  + [pl.BlockSpec(bf16_tab(TBS), lambda i: (0, 0, i))] * 2,
        out_shape=[jax.ShapeDtypeStruct((D_MODEL, n_tok), BF16),
                   jax.ShapeDtypeStruct(f32_tab(n_tok), F32), jax.ShapeDtypeStruct(f32_tab(n_tok), F32),
                   jax.ShapeDtypeStruct(bf16_tab(n_tok), jnp.int32), jax.ShapeDtypeStruct(bf16_tab(n_tok), jnp.int32)],
        compiler_params=_cparams("arbitrary"),
        name="peer_scores_lat" if latent else "peer_scores_ctx",
    )(x, mod_l, lp["norm2_g"], lp["peer_wqt"], lp["peer_subkeys"])
    nb = n_tok // TBP
    n_et = PEER_KEYS * PEER_KEYS // PEER_ET
    row_spec = pl.BlockSpec((PEER_HEADS, SUBLANES, TBP), lambda i, e: (0, e // (SUBLANES // PEER_ROWS), i))
    return pl.pallas_call(
        _peer_dense_kernel,
        grid=(nb, n_et),
        in_specs=[pl.BlockSpec((D_MODEL, TBP), lambda i, e: (0, i)), row_spec, row_spec]
                 + [pl.BlockSpec(bf16_tab(TBP), lambda i, e: (0, 0, i))] * 2
                 + [pl.BlockSpec((PEER_ET, D_MODEL), lambda i, e: (e, 0)),
                    pl.BlockSpec((PEER_ET, D_MODEL), lambda i, e: (jnp.maximum(e - 1, 0), 0)),
                    pl.BlockSpec((PEER_ET, D_MODEL), lambda i, e: (jnp.where(e == n_et - 1, n_et - 1, 0), 0)),
                    pl.BlockSpec((TBP, D_MODEL), lambda i, e: (i, 0)),
                    mod_spec(TBP)],
        out_specs=pl.BlockSpec((TBP, D_MODEL), lambda i, e: (i, 0)),
        out_shape=jax.ShapeDtypeStruct((n_tok, D_MODEL), F32),
        scratch_shapes=[pltpu.VMEM((TBP, D_MODEL), F32), pltpu.VMEM((PEER_ET, TBP), F32),
                        pltpu.VMEM((PEER_ET // 2, TBP), jnp.int32), pltpu.VMEM((PEER_ET // 2, TBP), jnp.int32),
                        pltpu.VMEM((TBP // 2, PEER_ET), jnp.int32)],
        compiler_params=_cparams("arbitrary", "arbitrary"),
        name="peer_dense_lat" if latent else "peer_dense_ctx",
    )(ht, *tabs, lp["peer_u"], lp["peer_v"], lp["peer_v"], x, mod_l)


def _layer_params(l, w):
    tile = lambda g, n: jnp.tile(g, n)
    gains = jnp.stack([
        tile(w["nat_qk_g"][l, 0], 4), tile(w["nat_qk_g"][l, 1], 4),
        tile(w["gqa_qk_g"][l, 0], 4), tile(w["gqa_qk_g"][l, 1], 4),
        tile(w["diff_qk_g"][l, 0], 8), tile(w["diff_qk_g"][l, 1], 8),
        jnp.zeros((256,), F32), jnp.zeros((256,), F32)])
    return dict(
        norm1_g=w["norm1_g"][l].reshape(1, D_MODEL),
        norm2_g=w["norm2_g"][l].reshape(1, D_MODEL),
        w_in=w["w_in"][l].astype(BF16),
        qk_gains=gains,
        bd64=_block_ones(256, 64), bd32=_block_ones(256, 32), bd256=_block_ones(256, 256),
        rope=w["rope"],
        sgu_norm_g=w["sgu_norm_g"][l].reshape(1, SGU_WIDTH),
        sgu_w=w["sgu_w"][l].astype(BF16),
        sgu_b=jnp.repeat(w["sgu_b"][l].T, SGU_WIDTH // SGU_GROUPS, axis=1),
        diff_lambda=w["diff_lambda"][l],
        diff_sub_g=tile(w["diff_sub_g"][l], 4).reshape(1, 256),
        nat_bias=_nat_bias_table(w["nat_rpb"][l]),
        w_branch=w["w_branch"][l].astype(BF16),
        w_gate=w["w_gate"][l].astype(BF16),
        b_gate=w["b_gate"][l].reshape(1, N_BRANCH * D_MODEL),
        w_out=w["w_out"][l].astype(BF16),
        peer_wqt=w["peer_wq"][l].T.astype(BF16),
        peer_subkeys=w["peer_subkeys"][l].astype(BF16),
        peer_u=w["peer_u"][l].astype(BF16),
        peer_v=w["peer_v"][l].astype(BF16),
    )


def kernel(x_prompt, x_sample, c, cache_nat_k, cache_nat_v, cache_gqa_k, cache_gqa_v, cache_diff_k, cache_diff_v, c_ctx, w_mod, b_mod, norm1_g, norm2_g, w_in, nat_qk_g, nat_rpb, gqa_qk_g, diff_qk_g, diff_lambda, diff_sub_g, sgu_norm_g, sgu_w, sgu_b, w_branch, w_gate, b_gate, w_out, peer_wq, peer_subkeys, peer_u, peer_v):
    w = dict(norm1_g=norm1_g, norm2_g=norm2_g, w_in=w_in, nat_qk_g=nat_qk_g, nat_rpb=nat_rpb,
             gqa_qk_g=gqa_qk_g, diff_qk_g=diff_qk_g, diff_lambda=diff_lambda, diff_sub_g=diff_sub_g,
             sgu_norm_g=sgu_norm_g, sgu_w=sgu_w, sgu_b=sgu_b, w_branch=w_branch, w_gate=w_gate,
             b_gate=b_gate, w_out=w_out, peer_wq=peer_wq, peer_subkeys=peer_subkeys, peer_u=peer_u,
             peer_v=peer_v, rope=_rope_tables())
    cvec = jnp.concatenate([c_ctx[None], c, jnp.zeros((MOD_ROWS - 1 - DEC_BATCH, D_MODEL), F32)], axis=0)
    mod = _modulation(cvec, w_mod, b_mod).reshape(DEPTH, MOD_ROWS, 1, 6 * D_MODEL)
    xp = x_prompt.reshape(N_CTX, D_MODEL)
    xs = x_sample.reshape(N_LAT, D_MODEL)
    new = []
    for l in range(DEPTH):
        lp = _layer_params(l, w)
        lam_init = 0.8 - 0.6 * math.exp(-0.3 * l)
        caches = (cache_nat_k, cache_nat_v, cache_gqa_k, cache_gqa_v, cache_diff_k, cache_diff_v)
        proj = _in_projection(xp, mod[l], lp, latent=False)
        new.append(proj)
        attn = _ctx_attention(proj, lp, lam_init)
        xp = _merge(xp, mod[l], lp, [(attn, 0), (attn, 1), (attn, 2), (proj, _COL["sgu"])], latent=False)
        xp = _peer(xp, mod[l], lp, latent=False)
        proj = _in_projection(xs, mod[l], lp, latent=True)
        o_nat, o_gqa, o_dif = _lat_attention(proj, caches, lp, l, lam_init)
        xs = _merge(xs, mod[l], lp, [(o_nat, 0), (o_gqa, 0), (o_dif, 0), (proj, _COL["sgu"])], latent=True)
        xs = _peer(xs, mod[l], lp, latent=True)

    def cache_out(lo, width, tail):
        a = jnp.stack([p[:, lo:lo + width].reshape(BATCH, SEQ, width) for p in new], axis=1)
        return a.reshape((BATCH, DEPTH, SEQ) + tail)

    return (xp.reshape(BATCH, SEQ, D_MODEL), xs.reshape(DEC_BATCH, DEC_SEQ, D_MODEL),
            cache_out(256, 256, (NAT_HEADS, HEAD_DIM)), cache_out(512, 256, (NAT_HEADS, HEAD_DIM)),
            cache_out(1024, 128, (GQA_KV_HEADS, HEAD_DIM)), cache_out(1152, 128, (GQA_KV_HEADS, HEAD_DIM)),
            cache_out(1536, 256, (DIFF_HEADS, 2, DIFF_DIM)), cache_out(1792, 256, (DIFF_HEADS, HEAD_DIM)))
```

```python
import functools
import math

import numpy as np
import jax
import jax.numpy as jnp
from jax import lax
from jax.experimental import pallas as pl
from jax.experimental.pallas import tpu as pltpu

F32 = jnp.float32
BF16 = jnp.bfloat16

D_MODEL = 1024
BATCH = 16
SEQ = 256
DEPTH = 2
DEC_BATCH = 2
DEC_SEQ = 4096
PAST_LEN = 512
GRID_W = 64
HEAD_DIM = 64
NAT_HEADS = 4
NAT_ROWS = 8
NAT_COLS = 16
GQA_HEADS = 4
GQA_KV_HEADS = 2
DIFF_HEADS = 4
DIFF_DIM = 32
SGU_WIDTH = 256
SGU_GROUPS = 4
CHUNK = 128
N_BRANCH = 4
BRANCH_W = 256
IN_W = 2560
PEER_HEADS = 8
PEER_KEYS = 128
PEER_QDIM = 256
PEER_TOPK = 16
ROPE_BASE = 10000.0
EPS = 1e-6

N_CTX = BATCH * SEQ
N_LAT = DEC_BATCH * DEC_SEQ
MOD_ROWS = 8
TB = 256
TQ = 256
PROJ_W = 2304
NAT_QROWS = 4
NAT_KROWS = NAT_QROWS + NAT_ROWS
TBS = 512
TBP = 1024
PEER_ET = 512
PEER_ROWS = PEER_ET // PEER_KEYS
SUBLANES = 8
BF16_ROWS = 16
VMEM_LIMIT = 56 * 1024 * 1024
NEG = -1e30


def _cparams(*sem):
    return pltpu.CompilerParams(dimension_semantics=sem, vmem_limit_bytes=VMEM_LIMIT)


def _const_spec(shape):
    return pl.BlockSpec(shape, lambda *_: (0,) * len(shape))


def _gelu(x):
    return 0.5 * x * (1.0 + jnp.tanh(0.7978845608028654 * (x + 0.044715 * (x * x * x))))


def _sigmoid(x):
    return 1.0 / (1.0 + jnp.exp(-x))


def _dot(a, b):
    return jnp.dot(a.astype(BF16), b.astype(BF16), preferred_element_type=F32)


def _dot_nt(a, b):
    return lax.dot_general(a.astype(BF16), b.astype(BF16), (((1,), (1,)), ((), ())),
                           preferred_element_type=F32)


def _group_mean(y2, ones_bd):
    hi = y2.astype(BF16)
    lo = (y2 - hi.astype(F32)).astype(BF16)
    return (jnp.dot(hi, ones_bd, preferred_element_type=F32)
            + jnp.dot(lo, ones_bd, preferred_element_type=F32))


def _block_ones(width, group):
    idx = np.arange(width) // group
    return jnp.asarray((idx[:, None] == idx[None, :]).astype(np.float32) / group, dtype=BF16)


def _rms_mod(x, gain, scale, shift):
    xn = x * lax.rsqrt(jnp.mean(x * x, axis=-1, keepdims=True) + EPS) * gain
    return xn * (1.0 + scale) + shift


def _mod_spec(latent):
    if latent:
        return pl.BlockSpec((1, 1, 6 * D_MODEL), lambda i, *_: (1 + i // (DEC_SEQ // TB), 0, 0))
    return pl.BlockSpec((1, 1, 6 * D_MODEL), lambda i, *_: (0, 0, 0))


def _mod_kernel(c_ref, w_ref, b_ref, o_ref):
    c = c_ref[...]
    s = c * _sigmoid(c)
    o_ref[0] = _dot(s, w_ref[0]) + b_ref[0]


def _modulation(cvec, w_mod, b_mod):
    tn = 1536
    return pl.pallas_call(
        _mod_kernel,
        grid=(DEPTH, 6 * D_MODEL // tn),
        in_specs=[pl.BlockSpec((MOD_ROWS, D_MODEL), lambda l, j: (0, 0)),
                  pl.BlockSpec((1, D_MODEL, tn), lambda l, j: (l, 0, j)),
                  pl.BlockSpec((1, 1, tn), lambda l, j: (l, 0, j))],
        out_specs=pl.BlockSpec((1, MOD_ROWS, tn), lambda l, j: (l, 0, j)),
        out_shape=jax.ShapeDtypeStruct((DEPTH, MOD_ROWS, 6 * D_MODEL), F32),
        compiler_params=_cparams("arbitrary", "arbitrary"),
        name="modulation",
    )(cvec, w_mod, b_mod.reshape(DEPTH, 1, 6 * D_MODEL))


def _rope_tables():
    t = np.arange(DEC_SEQ)
    pos = (t // GRID_W, t % GRID_W)
    out = []
    for d in (HEAD_DIM, DIFF_DIM):
        half, quarter = d // 2, d // 4
        inv = ROPE_BASE ** (-np.arange(quarter, dtype=np.float32) * 2.0 / half)
        lane = np.arange(128) % d
        part, j = lane // half, lane % half
        ang = np.stack([pos[0][:, None] * inv[None, :], pos[1][:, None] * inv[None, :]], axis=1)
        a = ang[:, part, j % quarter].astype(np.float32)
        cos, sin = np.cos(a), np.sin(a)
        out += [jnp.asarray(cos, dtype=F32),
                jnp.asarray(np.where(j < quarter, -sin, 0.0), dtype=F32),
                jnp.asarray(np.where(j >= quarter, sin, 0.0), dtype=F32)]
    return out


def _rope(y, c, sa, sb, quarter):
    w = y.shape[-1]
    rep = w // 128
    if rep > 1:
        c, sa, sb = (jnp.concatenate([t] * rep, axis=-1) for t in (c, sa, sb))
    up = pltpu.roll(y, w - quarter, 1)
    dn = pltpu.roll(y, quarter, 1)
    return y * c + up * sa + dn * sb


def _inproj_kernel(*refs, latent):
    (x_ref, mod_ref, n1_ref, w_ref, g_ref, bd64_ref, bd32_ref, bd256_ref,
     sgn_ref, sgw_ref, sgb_ref) = refs[:11]
    o_ref = refs[-1]
    mod = mod_ref[0]
    h = _rms_mod(x_ref[...], n1_ref[...], mod[:, D_MODEL:2 * D_MODEL], mod[:, 0:D_MODEL])
    y = jnp.dot(h.astype(BF16), w_ref[...], preferred_element_type=F32)

    def qk_norm(lo, width, bd, gain_row):
        v = y[:, lo:lo + width]
        ms = _group_mean(v * v, bd)
        return v * lax.rsqrt(ms + EPS) * g_ref[gain_row:gain_row + 1, 0:width]

    bd64 = bd64_ref[...]
    bd32 = bd32_ref[...]
    if latent:
        c64, sa64, sb64, c32, sa32, sb32 = (r[...] for r in refs[11:17])
        rope64 = functools.partial(_rope, c=c64, sa=sa64, sb=sb64, quarter=HEAD_DIM // 4)
        rope32 = functools.partial(_rope, c=c32, sa=sa32, sb=sb32, quarter=DIFF_DIM // 4)
    else:
        rope64 = rope32 = lambda v: v
    o_ref[:, 0:256] = qk_norm(0, 256, bd64, 0)
    o_ref[:, 256:512] = qk_norm(256, 256, bd64, 1)
    o_ref[:, 512:768] = y[:, 512:768]
    o_ref[:, 768:1024] = rope64(qk_norm(768, 256, bd64, 2))
    o_ref[:, 1024:1152] = rope64(qk_norm(1024, 128, bd64[0:128, 0:128], 3))
    o_ref[:, 1152:1280] = y[:, 1152:1280]
    o_ref[:, 1280:1536] = rope32(qk_norm(1280, 256, bd32, 4))
    o_ref[:, 1536:1792] = rope32(qk_norm(1536, 256, bd32, 5))
    o_ref[:, 1792:2048] = y[:, 1792:2048]
    u = _gelu(y[:, 2048:2304])
    v = _gelu(y[:, 2304:2560])
    vn = v * lax.rsqrt(_group_mean(v * v, bd256_ref[...]) + EPS) * sgn_ref[...]
    vnb = vn.astype(BF16)
    lane_group = lax.broadcasted_iota(jnp.int32, (CHUNK, SGU_WIDTH), 1) // (SGU_WIDTH // SGU_GROUPS)
    for n in range(TB // CHUNK):
        vc = vnb[n * CHUNK:(n + 1) * CHUNK, :]
        s = sgb_ref[...]
        for g in range(SGU_GROUPS):
            sg = jnp.dot(sgw_ref[g], vc, preferred_element_type=F32)
            s = s + jnp.where(lane_group == g, sg, 0.0)
        o_ref[n * CHUNK:(n + 1) * CHUNK, 2048:2304] = u[n * CHUNK:(n + 1) * CHUNK, :] * s


def _in_projection(x, mod_l, lp, latent):
    n_tok = x.shape[0]
    in_specs = [pl.BlockSpec((TB, D_MODEL), lambda i: (i, 0)),
                _mod_spec(latent),
                _const_spec((1, D_MODEL)),
                _const_spec((D_MODEL, IN_W)),
                _const_spec((8, 256)),
                _const_spec((256, 256)), _const_spec((256, 256)), _const_spec((256, 256)),
                _const_spec((1, SGU_WIDTH)),
                _const_spec((SGU_GROUPS, CHUNK, CHUNK)),
                _const_spec((CHUNK, SGU_WIDTH))]
    args = [x, mod_l, lp["norm1_g"], lp["w_in"], lp["qk_gains"], lp["bd64"], lp["bd32"], lp["bd256"],
            lp["sgu_norm_g"], lp["sgu_w"], lp["sgu_b"]]
    if latent:
        in_specs += [pl.BlockSpec((TB, 128), lambda i: (i % (DEC_SEQ // TB), 0))] * 6
        args += lp["rope"]
    return pl.pallas_call(
        functools.partial(_inproj_kernel, latent=latent),
        grid=(n_tok // TB,),
        in_specs=in_specs,
        out_specs=pl.BlockSpec((TB, PROJ_W), lambda i: (i, 0)),
        out_shape=jax.ShapeDtypeStruct((n_tok, PROJ_W), F32),
        compiler_params=_cparams("arbitrary"),
        name="in_projection_lat" if latent else "in_projection_ctx",
    )(*args)


LOG2E = 1.4426950408889634


def _softmax_parts(scores):
    m = functools.reduce(jnp.maximum, [jnp.max(s, axis=-1, keepdims=True) for s in scores])
    ps = [jnp.exp2(s - m) for s in scores]
    l = functools.reduce(jnp.add, [jnp.sum(p, axis=-1, keepdims=True) for p in ps])
    return [p.astype(BF16) for p in ps], 1.0 / l


def _diff_lambda(lam_ref, lam_init):
    lv = lam_ref[...]
    a = jnp.sum(lv[0:1] * lv[1:2], axis=-1, keepdims=True)
    b = jnp.sum(lv[2:3] * lv[3:4], axis=-1, keepdims=True)
    return jnp.exp(a) - jnp.exp(b) + lam_init


def _head(ref_or_val, h, width=HEAD_DIM):
    return ref_or_val[:, h * width:(h + 1) * width]


def _mha(q, ks, vs, n_heads, kv_group, scale, biases=None):
    ks, vs = [k.astype(BF16) for k in ks], [v.astype(BF16) for v in vs]
    qs = (q * (scale * LOG2E)).astype(BF16)

    def scores_of(h):
        scores = [_dot_nt(_head(qs, h), _head(k, h // kv_group)) for k in ks]
        if biases is not None:
            scores = [s if b is None else s + b[h] * LOG2E for s, b in zip(scores, biases)]
        return scores

    def finish(h, ps, rl):
        return functools.reduce(jnp.add, [_dot(p, _head(v, h // kv_group)) for p, v in zip(ps, vs)]) * rl

    return jnp.concatenate(_softmax_pipeline(n_heads, scores_of, finish), axis=-1)


def _softmax_pipeline(n, scores_of, finish):
    outs = []
    scores = scores_of(0)
    for i in range(n):
        ps, rl = _softmax_parts(scores)
        if i + 1 < n:
            scores = scores_of(i + 1)
        outs.append(finish(i, ps, rl))
    return outs


def _diff_attn(q, ks, vs, lam, sub_gain, bd64, lam_init):
    qs = (q * (DIFF_DIM ** -0.5 * LOG2E)).astype(BF16)
    ks, vs = [k.astype(BF16) for k in ks], [v.astype(BF16) for v in vs]
    scores_of = lambda i: [_dot_nt(_head(qs, i, DIFF_DIM), _head(k, i, DIFF_DIM)) for k in ks]
    finish = lambda i, ps, rl: functools.reduce(jnp.add, [_dot(p, _head(v, i // 2)) for p, v in zip(ps, vs)]) * rl
    pv = _softmax_pipeline(2 * DIFF_HEADS, scores_of, finish)
    o = jnp.concatenate([pv[2 * h] - lam * pv[2 * h + 1] for h in range(DIFF_HEADS)], axis=-1)
    ms = _group_mean(o * o, bd64)
    return o * lax.rsqrt(ms + EPS) * sub_gain * (1.0 - lam_init)


def _ctx_attn_kernel(nq, nk, nv, gq, gk, gv, dq, dk, dv, lam_ref, subg_ref, bd64_ref, o_ref, *, lam_init):
    scale = HEAD_DIM ** -0.5
    o_ref[:, 0:256] = _mha(nq[...], [nk[...]], [nv[...]], NAT_HEADS, 1, scale)
    o_ref[:, 256:512] = _mha(gq[...], [gk[...]], [gv[...]], GQA_HEADS, GQA_HEADS // GQA_KV_HEADS, scale)
    lam = _diff_lambda(lam_ref, lam_init)
    o_ref[:, 512:768] = _diff_attn(dq[...], [dk[...]], [dv[...]], lam, subg_ref[...], bd64_ref[...], lam_init)


_COL = dict(nq=0, nk=1, nv=2, gq=3, gk=8, gv=9, dq=5, dk=6, dv=7, sgu=8)


def _ctx_attention(proj, lp, lam_init):
    blk = lambda name, w: pl.BlockSpec((SEQ, w), lambda b: (b, _COL[name]))
    return pl.pallas_call(
        functools.partial(_ctx_attn_kernel, lam_init=lam_init),
        grid=(BATCH,),
        in_specs=[blk("nq", 256), blk("nk", 256), blk("nv", 256),
                  blk("gq", 256), blk("gk", 128), blk("gv", 128),
                  blk("dq", 256), blk("dk", 256), blk("dv", 256),
                  _const_spec((4, DIFF_DIM)), _const_spec((1, 256)), _const_spec((256, 256))],
        out_specs=pl.BlockSpec((SEQ, 768), lambda b: (b, 0)),
        out_shape=jax.ShapeDtypeStruct((N_CTX, 768), F32),
        compiler_params=_cparams("arbitrary"),
        name="ctx_attention",
    )(*([proj] * 9), lp["diff_lambda"], lp["diff_sub_g"], lp["bd64"])


def _gqa_lat_kernel(q_ref, k_ref, v_ref, ck_ref, cv_ref, o_ref):
    o_ref[...] = _mha(q_ref[...], [k_ref[...], ck_ref[...]], [v_ref[...], cv_ref[...]],
                      GQA_HEADS, GQA_HEADS // GQA_KV_HEADS, HEAD_DIM ** -0.5)


def _diff_lat_kernel(q_ref, k_ref, v_ref, ck_ref, cv_ref, lam_ref, subg_ref, bd64_ref, o_ref, *, lam_init):
    lam = _diff_lambda(lam_ref, lam_init)
    o_ref[...] = _diff_attn(q_ref[...], [k_ref[...], ck_ref[...]], [v_ref[...], cv_ref[...]],
                            lam, subg_ref[...], bd64_ref[...], lam_init)


def _nat_lat_kernel(q_ref, k_ref, v_ref, ck_ref, cv_ref, bias_ref, o_ref):
    i = pl.program_id(1)
    k_row0 = jnp.clip(NAT_QROWS * i - NAT_ROWS // 2, 0, GRID_W - NAT_KROWS)
    start = pl.multiple_of(k_row0 * GRID_W, GRID_W)
    kw = k_ref[pl.ds(start, NAT_KROWS * GRID_W), :]
    vw = v_ref[pl.ds(start, NAT_KROWS * GRID_W), :]
    o_ref[...] = _mha(q_ref[...], [kw, ck_ref[...]], [vw, cv_ref[...]], NAT_HEADS, 1, HEAD_DIM ** -0.5,
                      biases=[bias_ref[0], None])


def _nat_bias_table(rpb):
    rows = DEC_SEQ // GRID_W
    nblk = rows // NAT_QROWS
    pad = jnp.pad(rpb, ((0, 0), (0, 0), (GRID_W - NAT_COLS, GRID_W - NAT_COLS)))
    toep = jnp.stack([pad[:, :, GRID_W - 1 - c:2 * GRID_W - 1 - c] for c in range(GRID_W)], axis=2)
    col = np.arange(GRID_W)
    cs = np.clip(col - NAT_COLS // 2, 0, GRID_W - NAT_COLS)
    col_ok = (col[None, :] >= cs[:, None]) & (col[None, :] < cs[:, None] + NAT_COLS)
    toep = jnp.where(col_ok, toep, NEG)
    masked = jnp.full((NAT_HEADS, GRID_W, GRID_W), NEG, F32)
    cases = []
    for blk in (0, 1, nblk - 1):
        r0 = blk * NAT_QROWS
        k0 = int(np.clip(r0 - NAT_ROWS // 2, 0, rows - NAT_KROWS))
        q_rows = []
        for qr in range(r0, r0 + NAT_QROWS):
            rs = int(np.clip(qr - NAT_ROWS // 2, 0, rows - NAT_ROWS))
            q_rows.append(jnp.concatenate(
                [toep[:, kr - qr + NAT_ROWS - 1] if rs <= kr < rs + NAT_ROWS else masked
                 for kr in range(k0, k0 + NAT_KROWS)], axis=-1))
        cases.append(jnp.concatenate(q_rows, axis=-2))
    return jnp.stack(cases, axis=0)


def _lat_attention(proj, caches, lp, l, lam_init):
    cnk, cnv, cgk, cgv, cdk, cdv = caches
    nq_blocks = DEC_SEQ // TQ
    qspec = lambda name: pl.BlockSpec((TQ, 256), lambda b, i: (b * nq_blocks + i, _COL[name]))
    kvspec = lambda name, w: pl.BlockSpec((DEC_SEQ, w), lambda b, i: (b, _COL[name]))
    cspec = lambda w: pl.BlockSpec((None, None, PAST_LEN, w), lambda b, i: (b, l, 0, 0))
    ospec = pl.BlockSpec((TQ, 256), lambda b, i: (b * nq_blocks + i, 0))
    oshape = jax.ShapeDtypeStruct((N_LAT, 256), F32)
    o_gqa = pl.pallas_call(
        _gqa_lat_kernel,
        grid=(DEC_BATCH, nq_blocks),
        in_specs=[qspec("gq"), kvspec("gk", 128), kvspec("gv", 128), cspec(128), cspec(128)],
        out_specs=ospec, out_shape=oshape,
        compiler_params=_cparams("arbitrary", "arbitrary"),
        name="gqa_lat_attention",
    )(proj, proj, proj, cgk.reshape(DEC_BATCH, DEPTH, PAST_LEN, 128), cgv.reshape(DEC_BATCH, DEPTH, PAST_LEN, 128))
    o_dif = pl.pallas_call(
        functools.partial(_diff_lat_kernel, lam_init=lam_init),
        grid=(DEC_BATCH, nq_blocks),
        in_specs=[qspec("dq"), kvspec("dk", 256), kvspec("dv", 256), cspec(256), cspec(256),
                  _const_spec((4, DIFF_DIM)), _const_spec((1, 256)), _const_spec((256, 256))],
        out_specs=ospec, out_shape=oshape,
        compiler_params=_cparams("arbitrary", "arbitrary"),
        name="diff_lat_attention",
    )(proj, proj, proj, cdk.reshape(DEC_BATCH, DEPTH, PAST_LEN, 256), cdv.reshape(DEC_BATCH, DEPTH, PAST_LEN, 256),
      lp["diff_lambda"], lp["diff_sub_g"], lp["bd64"])
    nblk = DEC_SEQ // (NAT_QROWS * GRID_W)
    nat_q = NAT_QROWS * GRID_W
    o_nat = pl.pallas_call(
        _nat_lat_kernel,
        grid=(DEC_BATCH, nblk),
        in_specs=[pl.BlockSpec((nat_q, 256), lambda b, i: (b * nblk + i, _COL["nq"])),
                  kvspec("nk", 256), kvspec("nv", 256), cspec(256), cspec(256),
                  pl.BlockSpec((1, NAT_HEADS, nat_q, NAT_KROWS * GRID_W),
                               lambda b, i: (jnp.where(i == 0, 0, jnp.where(i == nblk - 1, 2, 1)), 0, 0, 0))],
        out_specs=pl.BlockSpec((nat_q, 256), lambda b, i: (b * nblk + i, 0)),
        out_shape=oshape,
        compiler_params=_cparams("arbitrary", "arbitrary"),
        name="nat_lat_attention",
    )(proj, proj, proj, cnk.reshape(DEC_BATCH, DEPTH, PAST_LEN, 256), cnv.reshape(DEC_BATCH, DEPTH, PAST_LEN, 256),
      lp["nat_bias"])
    return o_nat, o_gqa, o_dif


def _merge_kernel(x_ref, mod_ref, n1_ref, b0_ref, b1_ref, b2_ref, b3_ref, wb_ref, wg_ref, bg_ref, wo_ref, o_ref):
    x = x_ref[...]
    mod = mod_ref[0]
    hb = _rms_mod(x, n1_ref[...], mod[:, D_MODEL:2 * D_MODEL], mod[:, 0:D_MODEL]).astype(BF16)
    merged = None
    for n, b_ref in enumerate((b0_ref, b1_ref, b2_ref, b3_ref)):
        cols = slice(n * D_MODEL, (n + 1) * D_MODEL)
        gate = _sigmoid(jnp.dot(hb, wg_ref[:, cols], preferred_element_type=F32) + bg_ref[:, cols])
        term = gate * _dot(b_ref[...], wb_ref[n])
        merged = term if merged is None else merged + term
    out = _dot(merged, wo_ref[...])
    o_ref[...] = x + mod[:, 2 * D_MODEL:3 * D_MODEL] * out


def _merge(x, mod_l, lp, branches, latent):
    n_tok = x.shape[0]
    return pl.pallas_call(
        _merge_kernel,
        grid=(n_tok // TB,),
        in_specs=[pl.BlockSpec((TB, D_MODEL), lambda i: (i, 0)), _mod_spec(latent), _const_spec((1, D_MODEL))]
                 + [pl.BlockSpec((TB, BRANCH_W), functools.partial(lambda i, c: (i, c), c=col)) for _, col in branches]
                 + [_const_spec((N_BRANCH, BRANCH_W, D_MODEL)), _const_spec((D_MODEL, N_BRANCH * D_MODEL)),
                    _const_spec((1, N_BRANCH * D_MODEL)), _const_spec((D_MODEL, D_MODEL))],
        out_specs=pl.BlockSpec((TB, D_MODEL), lambda i: (i, 0)),
        out_shape=jax.ShapeDtypeStruct((n_tok, D_MODEL), F32),
        compiler_params=_cparams("arbitrary"),
        name="merge_lat" if latent else "merge_ctx",
    )(x, mod_l, lp["norm1_g"], *[a for a, _ in branches], lp["w_branch"], lp["w_gate"], lp["b_gate"], lp["w_out"])


def _top_desc(s, k, with_rank=False):
    rows = []
    cur = s
    rank = jnp.full(s.shape, float(k), F32) if with_rank else None
    for i in range(k):
        m = jnp.max(cur, axis=0, keepdims=True)
        rows.append(m)
        hit = cur == m
        if with_rank:
            rank = jnp.where(hit, float(i), rank)
        cur = jnp.where(hit, NEG, cur)
    top = jnp.concatenate(rows, axis=0)
    return (top, rank) if with_rank else top


def _oddeven_merge_pairs(n):
    pairs = []
    p = 1
    while p < n:
        k = p
        while k >= 1:
            for j in range(k % p, n - k, 2 * k):
                for i in range(min(k, n - j - k)):
                    if (i + j) // (2 * p) == (i + j + k) // (2 * p):
                        pairs.append((i + j, i + j + k))
            k //= 2
        p *= 2
    return pairs


def _top_desc_sorted(s, k):
    n = s.shape[0] // SUBLANES
    tiles = [s[j * SUBLANES:(j + 1) * SUBLANES, :] for j in range(n)]
    for a, b in _oddeven_merge_pairs(n):
        tiles[a], tiles[b] = jnp.maximum(tiles[a], tiles[b]), jnp.minimum(tiles[a], tiles[b])
    rows = []
    for i in range(k):
        m = jnp.max(tiles[0], axis=0, keepdims=True)
        rows.append(m)
        hit = tiles[0] == m
        for j in range(min(n, k - 1 - i)):
            tiles[j] = jnp.where(hit, tiles[j + 1] if j + 1 < n else NEG, tiles[j])
    return jnp.concatenate(rows, axis=0)


def _count_above(t, x, strict):
    above = (lambda p: p > x) if strict else (lambda p: p >= x)
    row = lambda b: t[b:b + 1]
    b8 = above(row(7))
    b4 = above(jnp.where(b8, row(11), row(3)))
    b2 = above(jnp.where(b8, jnp.where(b4, row(13), row(9)), jnp.where(b4, row(5), row(1))))
    hi = jnp.where(b4, jnp.where(b2, row(14), row(12)), jnp.where(b2, row(10), row(8)))
    lo = jnp.where(b4, jnp.where(b2, row(6), row(4)), jnp.where(b2, row(2), row(0)))
    b1 = above(jnp.where(b8, hi, lo))
    count = jnp.where(b8, 8.0, 0.0) + jnp.where(b4, 4.0, 0.0) + jnp.where(b2, 2.0, 0.0) + jnp.where(b1, 1.0, 0.0)
    return count + jnp.where(above(row(15)), 1.0, 0.0)


def _peer_score_kernel(x_ref, mod_ref, n2_ref, wqt_ref, sk_ref, ht_ref, e1_ref, cnt_ref, rk_ref, e2_ref):
    mod = mod_ref[0]
    h2 = _rms_mod(x_ref[...], n2_ref[...], mod[:, 4 * D_MODEL:5 * D_MODEL], mod[:, 3 * D_MODEL:4 * D_MODEL])
    htb = h2.T.astype(BF16)
    ht_ref[...] = htb
    qt = jnp.dot(wqt_ref[...], htb, preferred_element_type=F32)
    k1 = PEER_TOPK + 1
    half = PEER_TOPK // 2
    row = lax.broadcasted_iota(jnp.int32, (k1, 1), 0)
    for h in range(PEER_HEADS):
        s1, s2 = (jnp.dot(sk_ref[p], qt[(2 * h + p) * PEER_KEYS:(2 * h + p + 1) * PEER_KEYS, :].astype(BF16),
                          preferred_element_type=F32) for p in range(2))
        t1 = _top_desc_sorted(s1, k1)
        t2 = _top_desc_sorted(s2, k1)
        rk2 = _count_above(t2, s2, strict=True)
        cands = [t1[0:1] + t2, jnp.where(row >= 1, t1 + t2[0:1], NEG)]
        for a in range(1, half):
            nb = k1 // (a + 1)
            cands.append(jnp.where((row[0:half] >= 1) & (row[0:half] < nb), t1[a:a + 1] + t2[0:half], NEG))
        best = _top_desc(jnp.concatenate(cands, axis=0), k1)
        z = jnp.sum(jnp.exp(best[0:PEER_TOPK] - best[0:1]), axis=0, keepdims=True)
        thr = 0.5 * (best[PEER_TOPK - 1:PEER_TOPK] + best[PEER_TOPK:k1])
        e1_ref[h] = jnp.exp(s1 - t1[0:1]) * (0.5 / z)
        cnt_ref[h] = _count_above(t2, thr - s1, strict=False)
        rk_ref[h] = pltpu.bitcast(rk2.astype(BF16), jnp.int32)
        e2_ref[h] = pltpu.bitcast(jnp.exp(s2 - t2[0:1]).astype(BF16), jnp.int32)


def _peer_dense_kernel(ht_ref, e1_ref, cnt_ref, rk_ref, e2_ref, u_ref, v_ref, vlast_ref, x_ref, mod_ref, o_ref,
                       acc_ref, a_ref, g_ref, w_ref, wt_ref):
    e = pl.program_id(1)
    n_tok = acc_ref.shape[0]

    @pl.when(e == 0)
    def _():
        acc_ref[...] = jnp.zeros_like(acc_ref)
        w_ref[...] = jnp.zeros(w_ref.shape, jnp.int32)

    wt_ref[...] = pltpu.bitcast(pltpu.bitcast(w_ref[...], BF16).T, jnp.int32)

    zero = jnp.zeros((), BF16)
    for rr in range(PEER_ROWS):
        r = (e % (SUBLANES // PEER_ROWS)) * PEER_ROWS + rr
        packed_rows = slice(rr * (PEER_KEYS // 2), (rr + 1) * (PEER_KEYS // 2))
        def tile_row(ref, h):
            t16 = jnp.broadcast_to(ref[h, pl.ds(r, 1), :], (BF16_ROWS, n_tok)).astype(BF16)
            return jnp.concatenate([t16] * (PEER_KEYS // BF16_ROWS), axis=0)
        cnt_rows = [tile_row(cnt_ref, h) for h in range(PEER_HEADS)]
        e1_rows = [tile_row(e1_ref, h) for h in range(PEER_HEADS)]
        for lt in range(n_tok // 128):
            lanes = slice(lt * 128, (lt + 1) * 128)
            g = None
            for h in range(PEER_HEADS):
                sel = pltpu.bitcast(rk_ref[h, :, lanes], BF16) < cnt_rows[h][:, lanes]
                term = jnp.where(sel, pltpu.bitcast(e2_ref[h, :, lanes], BF16) * e1_rows[h][:, lanes], zero)
                g = term if g is None else g + term
            g_ref[packed_rows, lanes] = pltpu.bitcast(g, jnp.int32)
    a_ref[...] = jnp.dot(u_ref[...], ht_ref[...], preferred_element_type=F32)
    for rr in range(PEER_ROWS):
        rows = slice(rr * PEER_KEYS, (rr + 1) * PEER_KEYS)
        packed_rows = slice(rr * (PEER_KEYS // 2), (rr + 1) * (PEER_KEYS // 2))
        for lt in range(n_tok // 128):
            lanes = slice(lt * 128, (lt + 1) * 128)
            a = a_ref[rows, lanes]
            t = jnp.tanh(a * (0.7978845608028654 + (0.7978845608028654 * 0.044715) * (a * a)))
            w_ref[packed_rows, lanes] = pltpu.bitcast(
                pltpu.bitcast(g_ref[packed_rows, lanes], BF16) * (a + a * t).astype(BF16), jnp.int32)
    acc_ref[...] += jnp.dot(pltpu.bitcast(wt_ref[...], BF16), v_ref[...], preferred_element_type=F32)

    @pl.when(e == pl.num_programs(1) - 1)
    def _():
        acc = acc_ref[...] + jnp.dot(pltpu.bitcast(w_ref[...], BF16).T, vlast_ref[...], preferred_element_type=F32)
        o_ref[...] = x_ref[...] + mod_ref[0][:, 5 * D_MODEL:6 * D_MODEL] * acc


def _peer(x, mod_l, lp, latent):
    n_tok = x.shape[0]

    def mod_spec(tb):
        if latent:
            return pl.BlockSpec((1, 1, 6 * D_MODEL), lambda i, *_: (1 + i // (DEC_SEQ // tb), 0, 0))
        return pl.BlockSpec((1, 1, 6 * D_MODEL), lambda i, *_: (0, 0, 0))

    n_keys2 = PEER_HEADS * PEER_QDIM
    f32_tab = lambda n: (PEER_HEADS, PEER_KEYS, n)
    bf16_tab = lambda n: (PEER_HEADS, PEER_KEYS // 2, n)
    ht, *tabs = pl.pallas_call(
        _peer_score_kernel,
        grid=(n_tok // TBS,),
        in_specs=[pl.BlockSpec((TBS, D_MODEL), lambda i: (i, 0)), mod_spec(TBS), _const_spec((1, D_MODEL)),
                  _const_spec((n_keys2, D_MODEL)), _const_spec((2, PEER_KEYS, PEER_QDIM // 2))],
        out_specs=[pl.BlockSpec((D_MODEL, TBS), lambda i: (0, i))]
                  + [pl.BlockSpec(f32_tab(TBS), lambda i: (0, 0, i))] * 2
                  + [pl.BlockSpec(bf16_tab(TBS), lambda i: (0, 0, i))] * 2,
        out_shape=[jax.ShapeDtypeStruct((D_MODEL, n_tok), BF16),
                   jax.ShapeDtypeStruct(f32_tab(n_tok), F32), jax.ShapeDtypeStruct(f32_tab(n_tok), F32),
                   jax.ShapeDtypeStruct(bf16_tab(n_tok), jnp.int32), jax.ShapeDtypeStruct(bf16_tab(n_tok), jnp.int32)],
        compiler_params=_cparams("arbitrary"),
        name="peer_scores_lat" if latent else "peer_scores_ctx",
    )(x, mod_l, lp["norm2_g"], lp["peer_wqt"], lp["peer_subkeys"])
    nb = n_tok // TBP
    n_et = PEER_KEYS * PEER_KEYS // PEER_ET
    row_spec = pl.BlockSpec((PEER_HEADS, SUBLANES, TBP), lambda i, e: (0, e // (SUBLANES // PEER_ROWS), i))
    return pl.pallas_call(
        _peer_dense_kernel,
        grid=(nb, n_et),
        in_specs=[pl.BlockSpec((D_MODEL, TBP), lambda i, e: (0, i)), row_spec, row_spec]
                 + [pl.BlockSpec(bf16_tab(TBP), lambda i, e: (0, 0, i))] * 2
                 + [pl.BlockSpec((PEER_ET, D_MODEL), lambda i, e: (e, 0)),
                    pl.BlockSpec((PEER_ET, D_MODEL), lambda i, e: (jnp.maximum(e - 1, 0), 0)),
                    pl.BlockSpec((PEER_ET, D_MODEL), lambda i, e: (jnp.where(e == n_et - 1, n_et - 1, 0), 0)),
                    pl.BlockSpec((TBP, D_MODEL), lambda i, e: (i, 0)),
                    mod_spec(TBP)],
        out_specs=pl.BlockSpec((TBP, D_MODEL), lambda i, e: (i, 0)),
        out_shape=jax.ShapeDtypeStruct((n_tok, D_MODEL), F32),
        scratch_shapes=[pltpu.VMEM((TBP, D_MODEL), F32), pltpu.VMEM((PEER_ET, TBP), F32),
                        pltpu.VMEM((PEER_ET // 2, TBP), jnp.int32), pltpu.VMEM((PEER_ET // 2, TBP), jnp.int32),
                        pltpu.VMEM((TBP // 2, PEER_ET), jnp.int32)],
        compiler_params=_cparams("arbitrary", "arbitrary"),
        name="peer_dense_lat" if latent else "peer_dense_ctx",
    )(ht, *tabs, lp["peer_u"], lp["peer_v"], lp["peer_v"], x, mod_l)


def _layer_params(l, w):
    tile = lambda g, n: jnp.tile(g, n)
    gains = jnp.stack([
        tile(w["nat_qk_g"][l, 0], 4), tile(w["nat_qk_g"][l, 1], 4),
        tile(w["gqa_qk_g"][l, 0], 4), tile(w["gqa_qk_g"][l, 1], 4),
        tile(w["diff_qk_g"][l, 0], 8), tile(w["diff_qk_g"][l, 1], 8),
        jnp.zeros((256,), F32), jnp.zeros((256,), F32)])
    return dict(
        norm1_g=w["norm1_g"][l].reshape(1, D_MODEL),
        norm2_g=w["norm2_g"][l].reshape(1, D_MODEL),
        w_in=w["w_in"][l].astype(BF16),
        qk_gains=gains,
        bd64=_block_ones(256, 64), bd32=_block_ones(256, 32), bd256=_block_ones(256, 256),
        rope=w["rope"],
        sgu_norm_g=w["sgu_norm_g"][l].reshape(1, SGU_WIDTH),
        sgu_w=w["sgu_w"][l].astype(BF16),
        sgu_b=jnp.repeat(w["sgu_b"][l].T, SGU_WIDTH // SGU_GROUPS, axis=1),
        diff_lambda=w["diff_lambda"][l],
        diff_sub_g=tile(w["diff_sub_g"][l], 4).reshape(1, 256),
        nat_bias=_nat_bias_table(w["nat_rpb"][l]),
        w_branch=w["w_branch"][l].astype(BF16),
        w_gate=w["w_gate"][l].astype(BF16),
        b_gate=w["b_gate"][l].reshape(1, N_BRANCH * D_MODEL),
        w_out=w["w_out"][l].astype(BF16),
        peer_wqt=w["peer_wq"][l].T.astype(BF16),
        peer_subkeys=w["peer_subkeys"][l].astype(BF16),
        peer_u=w["peer_u"][l].astype(BF16),
        peer_v=w["peer_v"][l].astype(BF16),
    )


def kernel(x_prompt, x_sample, c, cache_nat_k, cache_nat_v, cache_gqa_k, cache_gqa_v, cache_diff_k, cache_diff_v, c_ctx, w_mod, b_mod, norm1_g, norm2_g, w_in, nat_qk_g, nat_rpb, gqa_qk_g, diff_qk_g, diff_lambda, diff_sub_g, sgu_norm_g, sgu_w, sgu_b, w_branch, w_gate, b_gate, w_out, peer_wq, peer_subkeys, peer_u, peer_v):
    w = dict(norm1_g=norm1_g, norm2_g=norm2_g, w_in=w_in, nat_qk_g=nat_qk_g, nat_rpb=nat_rpb,
             gqa_qk_g=gqa_qk_g, diff_qk_g=diff_qk_g, diff_lambda=diff_lambda, diff_sub_g=diff_sub_g,
             sgu_norm_g=sgu_norm_g, sgu_w=sgu_w, sgu_b=sgu_b, w_branch=w_branch, w_gate=w_gate,
             b_gate=b_gate, w_out=w_out, peer_wq=peer_wq, peer_subkeys=peer_subkeys, peer_u=peer_u,
             peer_v=peer_v, rope=_rope_tables())
    cvec = jnp.concatenate([c_ctx[None], c, jnp.zeros((MOD_ROWS - 1 - DEC_BATCH, D_MODEL), F32)], axis=0)
    mod = _modulation(cvec, w_mod, b_mod).reshape(DEPTH, MOD_ROWS, 1, 6 * D_MODEL)
    xp = x_prompt.reshape(N_CTX, D_MODEL)
    xs = x_sample.reshape(N_LAT, D_MODEL)
    new = []
    for l in range(DEPTH):
        lp = _layer_params(l, w)
        lam_init = 0.8 - 0.6 * math.exp(-0.3 * l)
        caches = (cache_nat_k, cache_nat_v, cache_gqa_k, cache_gqa_v, cache_diff_k, cache_diff_v)
        proj = _in_projection(xp, mod[l], lp, latent=False)
        new.append(proj)
        attn = _ctx_attention(proj, lp, lam_init)
        xp = _merge(xp, mod[l], lp, [(attn, 0), (attn, 1), (attn, 2), (proj, _COL["sgu"])], latent=False)
        xp = _peer(xp, mod[l], lp, latent=False)
        proj = _in_projection(xs, mod[l], lp, latent=True)
        o_nat, o_gqa, o_dif = _lat_attention(proj, caches, lp, l, lam_init)
        xs = _merge(xs, mod[l], lp, [(o_nat, 0), (o_gqa, 0), (o_dif, 0), (proj, _COL["sgu"])], latent=True)
        xs = _peer(xs, mod[l], lp, latent=True)

    def cache_out(lo, width, tail):
        a = jnp.stack([p[:, lo:lo + width].reshape(BATCH, SEQ, width) for p in new], axis=1)
        return a.reshape((BATCH, DEPTH, SEQ) + tail)

    return (xp.reshape(BATCH, SEQ, D_MODEL), xs.reshape(DEC_BATCH, DEC_SEQ, D_MODEL),
            cache_out(256, 256, (NAT_HEADS, HEAD_DIM)), cache_out(512, 256, (NAT_HEADS, HEAD_DIM)),
            cache_out(1024, 128, (GQA_KV_HEADS, HEAD_DIM)), cache_out(1152, 128, (GQA_KV_HEADS, HEAD_DIM)),
            cache_out(1536, 256, (DIFF_HEADS, 2, DIFF_DIM)), cache_out(1792, 256, (DIFF_HEADS, HEAD_DIM)))
```

```python
import functools
import math

import numpy as np
import jax
import jax.numpy as jnp
from jax import lax
from jax.experimental import pallas as pl
from jax.experimental.pallas import tpu as pltpu

F32 = jnp.float32
BF16 = jnp.bfloat16

D_MODEL = 1024
BATCH = 16
SEQ = 256
DEPTH = 2
DEC_BATCH = 2
DEC_SEQ = 4096
PAST_LEN = 512
GRID_W = 64
HEAD_DIM = 64
NAT_HEADS = 4
NAT_ROWS = 8
NAT_COLS = 16
GQA_HEADS = 4
GQA_KV_HEADS = 2
DIFF_HEADS = 4
DIFF_DIM = 32
SGU_WIDTH = 256
SGU_GROUPS = 4
CHUNK = 128
N_BRANCH = 4
BRANCH_W = 256
IN_W = 2560
PEER_HEADS = 8
PEER_KEYS = 128
PEER_QDIM = 256
PEER_TOPK = 16
ROPE_BASE = 10000.0
EPS = 1e-6

N_CTX = BATCH * SEQ
N_LAT = DEC_BATCH * DEC_SEQ
MOD_ROWS = 8
TB = 512
TQ = 256
PROJ_W = 2304
NAT_QROWS = 4
NAT_KROWS = NAT_QROWS + NAT_ROWS
TBS = 512
TBP = 1024
PEER_ET = 512
PEER_ROWS = PEER_ET // PEER_KEYS
SUBLANES = 8
BF16_ROWS = 16
VMEM_LIMIT = 56 * 1024 * 1024
NEG = -1e30


def _cparams(*sem):
    return pltpu.CompilerParams(dimension_semantics=sem, vmem_limit_bytes=VMEM_LIMIT)


def _const_spec(shape):
    return pl.BlockSpec(shape, lambda *_: (0,) * len(shape))


def _gelu(x):
    return 0.5 * x * (1.0 + jnp.tanh(0.7978845608028654 * (x + 0.044715 * (x * x * x))))


def _sigmoid(x):
    return 1.0 / (1.0 + jnp.exp(-x))


def _dot(a, b):
    return jnp.dot(a.astype(BF16), b.astype(BF16), preferred_element_type=F32)


def _dot_nt(a, b):
    return lax.dot_general(a.astype(BF16), b.astype(BF16), (((1,), (1,)), ((), ())),
                           preferred_element_type=F32)


def _group_mean(y2, ones_bd):
    hi = y2.astype(BF16)
    lo = (y2 - hi.astype(F32)).astype(BF16)
    return (jnp.dot(hi, ones_bd, preferred_element_type=F32)
            + jnp.dot(lo, ones_bd, preferred_element_type=F32))


def _block_ones(width, group):
    idx = np.arange(width) // group
    return jnp.asarray((idx[:, None] == idx[None, :]).astype(np.float32) / group, dtype=BF16)


def _rms_mod(x, gain, scale, shift):
    xn = x * lax.rsqrt(jnp.mean(x * x, axis=-1, keepdims=True) + EPS) * gain
    return xn * (1.0 + scale) + shift


def _mod_spec(latent):
    if latent:
        return pl.BlockSpec((1, 1, 6 * D_MODEL), lambda i, *_: (1 + i // (DEC_SEQ // TB), 0, 0))
    return pl.BlockSpec((1, 1, 6 * D_MODEL), lambda i, *_: (0, 0, 0))


def _mod_kernel(c_ref, w_ref, b_ref, o_ref):
    c = c_ref[...]
    s = c * _sigmoid(c)
    o_ref[0] = _dot(s, w_ref[0]) + b_ref[0]


def _modulation(cvec, w_mod, b_mod):
    tn = 1536
    return pl.pallas_call(
        _mod_kernel,
        grid=(DEPTH, 6 * D_MODEL // tn),
        in_specs=[pl.BlockSpec((MOD_ROWS, D_MODEL), lambda l, j: (0, 0)),
                  pl.BlockSpec((1, D_MODEL, tn), lambda l, j: (l, 0, j)),
                  pl.BlockSpec((1, 1, tn), lambda l, j: (l, 0, j))],
        out_specs=pl.BlockSpec((1, MOD_ROWS, tn), lambda l, j: (l, 0, j)),
        out_shape=jax.ShapeDtypeStruct((DEPTH, MOD_ROWS, 6 * D_MODEL), F32),
        compiler_params=_cparams("arbitrary", "arbitrary"),
        name="modulation",
    )(cvec, w_mod, b_mod.reshape(DEPTH, 1, 6 * D_MODEL))


def _rope_tables():
    t = np.arange(DEC_SEQ)
    pos = (t // GRID_W, t % GRID_W)
    out = []
    for d in (HEAD_DIM, DIFF_DIM):
        half, quarter = d // 2, d // 4
        inv = ROPE_BASE ** (-np.arange(quarter, dtype=np.float32) * 2.0 / half)
        lane = np.arange(128) % d
        part, j = lane // half, lane % half
        ang = np.stack([pos[0][:, None] * inv[None, :], pos[1][:, None] * inv[None, :]], axis=1)
        a = ang[:, part, j % quarter].astype(np.float32)
        cos, sin = np.cos(a), np.sin(a)
        out += [jnp.asarray(cos, dtype=F32),
                jnp.asarray(np.where(j < quarter, -sin, 0.0), dtype=F32),
                jnp.asarray(np.where(j >= quarter, sin, 0.0), dtype=F32)]
    return out


def _rope(y, c, sa, sb, quarter):
    w = y.shape[-1]
    rep = w // 128
    if rep > 1:
        c, sa, sb = (jnp.concatenate([t] * rep, axis=-1) for t in (c, sa, sb))
    up = pltpu.roll(y, w - quarter, 1)
    dn = pltpu.roll(y, quarter, 1)
    return y * c + up * sa + dn * sb


def _inproj_kernel(*refs, latent):
    (x_ref, mod_ref, n1_ref, w_ref, g_ref, bd64_ref, bd32_ref, bd256_ref,
     sgn_ref, sgw_ref, sgb_ref) = refs[:11]
    o_ref = refs[-1]
    mod = mod_ref[0]
    h = _rms_mod(x_ref[...], n1_ref[...], mod[:, D_MODEL:2 * D_MODEL], mod[:, 0:D_MODEL])
    y = jnp.dot(h.astype(BF16), w_ref[...], preferred_element_type=F32)

    def qk_norm(lo, width, bd, gain_row):
        v = y[:, lo:lo + width]
        ms = _group_mean(v * v, bd)
        return v * lax.rsqrt(ms + EPS) * g_ref[gain_row:gain_row + 1, 0:width]

    bd64 = bd64_ref[...]
    bd32 = bd32_ref[...]
    if latent:
        c64, sa64, sb64, c32, sa32, sb32 = (r[...] for r in refs[11:17])
        rope64 = functools.partial(_rope, c=c64, sa=sa64, sb=sb64, quarter=HEAD_DIM // 4)
        rope32 = functools.partial(_rope, c=c32, sa=sa32, sb=sb32, quarter=DIFF_DIM // 4)
    else:
        rope64 = rope32 = lambda v: v
    o_ref[:, 0:256] = qk_norm(0, 256, bd64, 0)
    o_ref[:, 256:512] = qk_norm(256, 256, bd64, 1)
    o_ref[:, 512:768] = y[:, 512:768]
    o_ref[:, 768:1024] = rope64(qk_norm(768, 256, bd64, 2))
    o_ref[:, 1024:1152] = rope64(qk_norm(1024, 128, bd64[0:128, 0:128], 3))
    o_ref[:, 1152:1280] = y[:, 1152:1280]
    o_ref[:, 1280:1536] = rope32(qk_norm(1280, 256, bd32, 4))
    o_ref[:, 1536:1792] = rope32(qk_norm(1536, 256, bd32, 5))
    o_ref[:, 1792:2048] = y[:, 1792:2048]
    u = _gelu(y[:, 2048:2304])
    v = _gelu(y[:, 2304:2560])
    vn = v * lax.rsqrt(_group_mean(v * v, bd256_ref[...]) + EPS) * sgn_ref[...]
    vnb = vn.astype(BF16)
    lane_group = lax.broadcasted_iota(jnp.int32, (CHUNK, SGU_WIDTH), 1) // (SGU_WIDTH // SGU_GROUPS)
    for n in range(TB // CHUNK):
        vc = vnb[n * CHUNK:(n + 1) * CHUNK, :]
        s = sgb_ref[...]
        for g in range(SGU_GROUPS):
            sg = jnp.dot(sgw_ref[g], vc, preferred_element_type=F32)
            s = s + jnp.where(lane_group == g, sg, 0.0)
        o_ref[n * CHUNK:(n + 1) * CHUNK, 2048:2304] = u[n * CHUNK:(n + 1) * CHUNK, :] * s


def _in_projection(x, mod_l, lp, latent):
    n_tok = x.shape[0]
    in_specs = [pl.BlockSpec((TB, D_MODEL), lambda i: (i, 0)),
                _mod_spec(latent),
                _const_spec((1, D_MODEL)),
                _const_spec((D_MODEL, IN_W)),
                _const_spec((8, 256)),
                _const_spec((256, 256)), _const_spec((256, 256)), _const_spec((256, 256)),
                _const_spec((1, SGU_WIDTH)),
                _const_spec((SGU_GROUPS, CHUNK, CHUNK)),
                _const_spec((CHUNK, SGU_WIDTH))]
    args = [x, mod_l, lp["norm1_g"], lp["w_in"], lp["qk_gains"], lp["bd64"], lp["bd32"], lp["bd256"],
            lp["sgu_norm_g"], lp["sgu_w"], lp["sgu_b"]]
    if latent:
        in_specs += [pl.BlockSpec((TB, 128), lambda i: (i % (DEC_SEQ // TB), 0))] * 6
        args += lp["rope"]
    return pl.pallas_call(
        functools.partial(_inproj_kernel, latent=latent),
        grid=(n_tok // TB,),
        in_specs=in_specs,
        out_specs=pl.BlockSpec((TB, PROJ_W), lambda i: (i, 0)),
        out_shape=jax.ShapeDtypeStruct((n_tok, PROJ_W), F32),
        compiler_params=_cparams("arbitrary"),
        name="in_projection_lat" if latent else "in_projection_ctx",
    )(*args)


LOG2E = 1.4426950408889634


def _softmax_parts(scores):
    m = functools.reduce(jnp.maximum, [jnp.max(s, axis=-1, keepdims=True) for s in scores])
    ps = [jnp.exp2(s - m) for s in scores]
    l = functools.reduce(jnp.add, [jnp.sum(p, axis=-1, keepdims=True) for p in ps])
    return [p.astype(BF16) for p in ps], 1.0 / l


def _diff_lambda(lam_ref, lam_init):
    lv = lam_ref[...]
    a = jnp.sum(lv[0:1] * lv[1:2], axis=-1, keepdims=True)
    b = jnp.sum(lv[2:3] * lv[3:4], axis=-1, keepdims=True)
    return jnp.exp(a) - jnp.exp(b) + lam_init


def _head(ref_or_val, h, width=HEAD_DIM):
    return ref_or_val[:, h * width:(h + 1) * width]


def _mha(q, ks, vs, n_heads, kv_group, scale, biases=None):
    ks, vs = [k.astype(BF16) for k in ks], [v.astype(BF16) for v in vs]
    qs = (q * (scale * LOG2E)).astype(BF16)

    def scores_of(h):
        scores = [_dot_nt(_head(qs, h), _head(k, h // kv_group)) for k in ks]
        if biases is not None:
            scores = [s if b is None else s + b[h] * LOG2E for s, b in zip(scores, biases)]
        return scores

    def finish(h, ps, rl):
        return functools.reduce(jnp.add, [_dot(p, _head(v, h // kv_group)) for p, v in zip(ps, vs)]) * rl

    return jnp.concatenate(_softmax_pipeline(n_heads, scores_of, finish), axis=-1)


def _softmax_pipeline(n, scores_of, finish):
    outs = []
    scores = scores_of(0)
    for i in range(n):
        ps, rl = _softmax_parts(scores)
        if i + 1 < n:
            scores = scores_of(i + 1)
        outs.append(finish(i, ps, rl))
    return outs


def _diff_attn(q, ks, vs, lam, sub_gain, bd64, lam_init):
    qs = (q * (DIFF_DIM ** -0.5 * LOG2E)).astype(BF16)
    ks, vs = [k.astype(BF16) for k in ks], [v.astype(BF16) for v in vs]
    scores_of = lambda i: [_dot_nt(_head(qs, i, DIFF_DIM), _head(k, i, DIFF_DIM)) for k in ks]
    finish = lambda i, ps, rl: functools.reduce(jnp.add, [_dot(p, _head(v, i // 2)) for p, v in zip(ps, vs)]) * rl
    pv = _softmax_pipeline(2 * DIFF_HEADS, scores_of, finish)
    o = jnp.concatenate([pv[2 * h] - lam * pv[2 * h + 1] for h in range(DIFF_HEADS)], axis=-1)
    ms = _group_mean(o * o, bd64)
    return o * lax.rsqrt(ms + EPS) * sub_gain * (1.0 - lam_init)


def _ctx_attn_kernel(nq, nk, nv, gq, gk, gv, dq, dk, dv, lam_ref, subg_ref, bd64_ref, o_ref, *, lam_init):
    scale = HEAD_DIM ** -0.5
    o_ref[:, 0:256] = _mha(nq[...], [nk[...]], [nv[...]], NAT_HEADS, 1, scale)
    o_ref[:, 256:512] = _mha(gq[...], [gk[...]], [gv[...]], GQA_HEADS, GQA_HEADS // GQA_KV_HEADS, scale)
    lam = _diff_lambda(lam_ref, lam_init)
    o_ref[:, 512:768] = _diff_attn(dq[...], [dk[...]], [dv[...]], lam, subg_ref[...], bd64_ref[...], lam_init)


_COL = dict(nq=0, nk=1, nv=2, gq=3, gk=8, gv=9, dq=5, dk=6, dv=7, sgu=8)


def _ctx_attention(proj, lp, lam_init):
    blk = lambda name, w: pl.BlockSpec((SEQ, w), lambda b: (b, _COL[name]))
    return pl.pallas_call(
        functools.partial(_ctx_attn_kernel, lam_init=lam_init),
        grid=(BATCH,),
        in_specs=[blk("nq", 256), blk("nk", 256), blk("nv", 256),
                  blk("gq", 256), blk("gk", 128), blk("gv", 128),
                  blk("dq", 256), blk("dk", 256), blk("dv", 256),
                  _const_spec((4, DIFF_DIM)), _const_spec((1, 256)), _const_spec((256, 256))],
        out_specs=pl.BlockSpec((SEQ, 768), lambda b: (b, 0)),
        out_shape=jax.ShapeDtypeStruct((N_CTX, 768), F32),
        compiler_params=_cparams("arbitrary"),
        name="ctx_attention",
    )(*([proj] * 9), lp["diff_lambda"], lp["diff_sub_g"], lp["bd64"])


def _gqa_lat_kernel(q_ref, k_ref, v_ref, ck_ref, cv_ref, o_ref):
    o_ref[...] = _mha(q_ref[...], [k_ref[...], ck_ref[...]], [v_ref[...], cv_ref[...]],
                      GQA_HEADS, GQA_HEADS // GQA_KV_HEADS, HEAD_DIM ** -0.5)


def _diff_lat_kernel(q_ref, k_ref, v_ref, ck_ref, cv_ref, lam_ref, subg_ref, bd64_ref, o_ref, *, lam_init):
    lam = _diff_lambda(lam_ref, lam_init)
    o_ref[...] = _diff_attn(q_ref[...], [k_ref[...], ck_ref[...]], [v_ref[...], cv_ref[...]],
                            lam, subg_ref[...], bd64_ref[...], lam_init)


def _nat_lat_kernel(q_ref, k_ref, v_ref, ck_ref, cv_ref, bias_ref, o_ref):
    i = pl.program_id(1)
    k_row0 = jnp.clip(NAT_QROWS * i - NAT_ROWS // 2, 0, GRID_W - NAT_KROWS)
    start = pl.multiple_of(k_row0 * GRID_W, GRID_W)
    kw = k_ref[pl.ds(start, NAT_KROWS * GRID_W), :]
    vw = v_ref[pl.ds(start, NAT_KROWS * GRID_W), :]
    o_ref[...] = _mha(q_ref[...], [kw, ck_ref[...]], [vw, cv_ref[...]], NAT_HEADS, 1, HEAD_DIM ** -0.5,
                      biases=[bias_ref[0], None])


def _nat_bias_table(rpb):
    rows = DEC_SEQ // GRID_W
    nblk = rows // NAT_QROWS
    pad = jnp.pad(rpb, ((0, 0), (0, 0), (GRID_W - NAT_COLS, GRID_W - NAT_COLS)))
    toep = jnp.stack([pad[:, :, GRID_W - 1 - c:2 * GRID_W - 1 - c] for c in range(GRID_W)], axis=2)
    col = np.arange(GRID_W)
    cs = np.clip(col - NAT_COLS // 2, 0, GRID_W - NAT_COLS)
    col_ok = (col[None, :] >= cs[:, None]) & (col[None, :] < cs[:, None] + NAT_COLS)
    toep = jnp.where(col_ok, toep, NEG)
    masked = jnp.full((NAT_HEADS, GRID_W, GRID_W), NEG, F32)
    cases = []
    for blk in (0, 1, nblk - 1):
        r0 = blk * NAT_QROWS
        k0 = int(np.clip(r0 - NAT_ROWS // 2, 0, rows - NAT_KROWS))
        q_rows = []
        for qr in range(r0, r0 + NAT_QROWS):
            rs = int(np.clip(qr - NAT_ROWS // 2, 0, rows - NAT_ROWS))
            q_rows.append(jnp.concatenate(
                [toep[:, kr - qr + NAT_ROWS - 1] if rs <= kr < rs + NAT_ROWS else masked
                 for kr in range(k0, k0 + NAT_KROWS)], axis=-1))
        cases.append(jnp.concatenate(q_rows, axis=-2))
    return jnp.stack(cases, axis=0)


def _lat_attention(proj, caches, lp, l, lam_init):
    cnk, cnv, cgk, cgv, cdk, cdv = caches
    nq_blocks = DEC_SEQ // TQ
    qspec = lambda name: pl.BlockSpec((TQ, 256), lambda b, i: (b * nq_blocks + i, _COL[name]))
    kvspec = lambda name, w: pl.BlockSpec((DEC_SEQ, w), lambda b, i: (b, _COL[name]))
    cspec = lambda w: pl.BlockSpec((None, None, PAST_LEN, w), lambda b, i: (b, l, 0, 0))
    ospec = pl.BlockSpec((TQ, 256), lambda b, i: (b * nq_blocks + i, 0))
    oshape = jax.ShapeDtypeStruct((N_LAT, 256), F32)
    o_gqa = pl.pallas_call(
        _gqa_lat_kernel,
        grid=(DEC_BATCH, nq_blocks),
        in_specs=[qspec("gq"), kvspec("gk", 128), kvspec("gv", 128), cspec(128), cspec(128)],
        out_specs=ospec, out_shape=oshape,
        compiler_params=_cparams("arbitrary", "arbitrary"),
        name="gqa_lat_attention",
    )(proj, proj, proj, cgk.reshape(DEC_BATCH, DEPTH, PAST_LEN, 128), cgv.reshape(DEC_BATCH, DEPTH, PAST_LEN, 128))
    o_dif = pl.pallas_call(
        functools.partial(_diff_lat_kernel, lam_init=lam_init),
        grid=(DEC_BATCH, nq_blocks),
        in_specs=[qspec("dq"), kvspec("dk", 256), kvspec("dv", 256), cspec(256), cspec(256),
                  _const_spec((4, DIFF_DIM)), _const_spec((1, 256)), _const_spec((256, 256))],
        out_specs=ospec, out_shape=oshape,
        compiler_params=_cparams("arbitrary", "arbitrary"),
        name="diff_lat_attention",
    )(proj, proj, proj, cdk.reshape(DEC_BATCH, DEPTH, PAST_LEN, 256), cdv.reshape(DEC_BATCH, DEPTH, PAST_LEN, 256),
      lp["diff_lambda"], lp["diff_sub_g"], lp["bd64"])
    nblk = DEC_SEQ // (NAT_QROWS * GRID_W)
    nat_q = NAT_QROWS * GRID_W
    o_nat = pl.pallas_call(
        _nat_lat_kernel,
        grid=(DEC_BATCH, nblk),
        in_specs=[pl.BlockSpec((nat_q, 256), lambda b, i: (b * nblk + i, _COL["nq"])),
                  kvspec("nk", 256), kvspec("nv", 256), cspec(256), cspec(256),
                  pl.BlockSpec((1, NAT_HEADS, nat_q, NAT_KROWS * GRID_W),
                               lambda b, i: (jnp.where(i == 0, 0, jnp.where(i == nblk - 1, 2, 1)), 0, 0, 0))],
        out_specs=pl.BlockSpec((nat_q, 256), lambda b, i: (b * nblk + i, 0)),
        out_shape=oshape,
        compiler_params=_cparams("arbitrary", "arbitrary"),
        name="nat_lat_attention",
    )(proj, proj, proj, cnk.reshape(DEC_BATCH, DEPTH, PAST_LEN, 256), cnv.reshape(DEC_BATCH, DEPTH, PAST_LEN, 256),
      lp["nat_bias"])
    return o_nat, o_gqa, o_dif


def _merge_kernel(x_ref, mod_ref, n1_ref, b0_ref, b1_ref, b2_ref, b3_ref, wb_ref, wg_ref, bg_ref, wo_ref, o_ref):
    x = x_ref[...]
    mod = mod_ref[0]
    hb = _rms_mod(x, n1_ref[...], mod[:, D_MODEL:2 * D_MODEL], mod[:, 0:D_MODEL]).astype(BF16)
    merged = None
    for n, b_ref in enumerate((b0_ref, b1_ref, b2_ref, b3_ref)):
        cols = slice(n * D_MODEL, (n + 1) * D_MODEL)
        gate = _sigmoid(jnp.dot(hb, wg_ref[:, cols], preferred_element_type=F32) + bg_ref[:, cols])
        term = gate * _dot(b_ref[...], wb_ref[n])
        merged = term if merged is None else merged + term
    out = _dot(merged, wo_ref[...])
    o_ref[...] = x + mod[:, 2 * D_MODEL:3 * D_MODEL] * out


def _merge(x, mod_l, lp, branches, latent):
    n_tok = x.shape[0]
    return pl.pallas_call(
        _merge_kernel,
        grid=(n_tok // TB,),
        in_specs=[pl.BlockSpec((TB, D_MODEL), lambda i: (i, 0)), _mod_spec(latent), _const_spec((1, D_MODEL))]
                 + [pl.BlockSpec((TB, BRANCH_W), functools.partial(lambda i, c: (i, c), c=col)) for _, col in branches]
                 + [_const_spec((N_BRANCH, BRANCH_W, D_MODEL)), _const_spec((D_MODEL, N_BRANCH * D_MODEL)),
                    _const_spec((1, N_BRANCH * D_MODEL)), _const_spec((D_MODEL, D_MODEL))],
        out_specs=pl.BlockSpec((TB, D_MODEL), lambda i: (i, 0)),
        out_shape=jax.ShapeDtypeStruct((n_tok, D_MODEL), F32),
        compiler_params=_cparams("arbitrary"),
        name="merge_lat" if latent else "merge_ctx",
    )(x, mod_l, lp["norm1_g"], *[a for a, _ in branches], lp["w_branch"], lp["w_gate"], lp["b_gate"], lp["w_out"])


def _top_desc(s, k, with_rank=False):
    rows = []
    cur = s
    rank = jnp.full(s.shape, float(k), F32) if with_rank else None
    for i in range(k):
        m = jnp.max(cur, axis=0, keepdims=True)
        rows.append(m)
        hit = cur == m
        if with_rank:
            rank = jnp.where(hit, float(i), rank)
        cur = jnp.where(hit, NEG, cur)
    top = jnp.concatenate(rows, axis=0)
    return (top, rank) if with_rank else top


def _oddeven_merge_pairs(n):
    pairs = []
    p = 1
    while p < n:
        k = p
        while k >= 1:
            for j in range(k % p, n - k, 2 * k):
                for i in range(min(k, n - j - k)):
                    if (i + j) // (2 * p) == (i + j + k) // (2 * p):
                        pairs.append((i + j, i + j + k))
            k //= 2
        p *= 2
    return pairs


def _top_desc_sorted(s, k):
    n = s.shape[0] // SUBLANES
    tiles = [s[j * SUBLANES:(j + 1) * SUBLANES, :] for j in range(n)]
    for a, b in _oddeven_merge_pairs(n):
        tiles[a], tiles[b] = jnp.maximum(tiles[a], tiles[b]), jnp.minimum(tiles[a], tiles[b])
    rows = []
    for i in range(k):
        m = jnp.max(tiles[0], axis=0, keepdims=True)
        rows.append(m)
        hit = tiles[0] == m
        for j in range(min(n, k - 1 - i)):
            tiles[j] = jnp.where(hit, tiles[j + 1] if j + 1 < n else NEG, tiles[j])
    return jnp.concatenate(rows, axis=0)


def _count_above(t, x, strict):
    above = (lambda p: p > x) if strict else (lambda p: p >= x)
    row = lambda b: t[b:b + 1]
    b8 = above(row(7))
    b4 = above(jnp.where(b8, row(11), row(3)))
    b2 = above(jnp.where(b8, jnp.where(b4, row(13), row(9)), jnp.where(b4, row(5), row(1))))
    hi = jnp.where(b4, jnp.where(b2, row(14), row(12)), jnp.where(b2, row(10), row(8)))
    lo = jnp.where(b4, jnp.where(b2, row(6), row(4)), jnp.where(b2, row(2), row(0)))
    b1 = above(jnp.where(b8, hi, lo))
    count = jnp.where(b8, 8.0, 0.0) + jnp.where(b4, 4.0, 0.0) + jnp.where(b2, 2.0, 0.0) + jnp.where(b1, 1.0, 0.0)
    return count + jnp.where(above(row(15)), 1.0, 0.0)


def _peer_score_kernel(x_ref, mod_ref, n2_ref, wqt_ref, sk_ref, ht_ref, e1_ref, cnt_ref, rk_ref, e2_ref):
    mod = mod_ref[0]
    h2 = _rms_mod(x_ref[...], n2_ref[...], mod[:, 4 * D_MODEL:5 * D_MODEL], mod[:, 3 * D_MODEL:4 * D_MODEL])
    htb = h2.T.astype(BF16)
    ht_ref[...] = htb
    qt = jnp.dot(wqt_ref[...], htb, preferred_element_type=F32)
    k1 = PEER_TOPK + 1
    half = PEER_TOPK // 2
    row = lax.broadcasted_iota(jnp.int32, (k1, 1), 0)
    for h in range(PEER_HEADS):
        s1, s2 = (jnp.dot(sk_ref[p], qt[(2 * h + p) * PEER_KEYS:(2 * h + p + 1) * PEER_KEYS, :].astype(BF16),
                          preferred_element_type=F32) for p in range(2))
        t1 = _top_desc_sorted(s1, k1)
        t2 = _top_desc_sorted(s2, k1)
        rk2 = _count_above(t2, s2, strict=True)
        cands = [t1[0:1] + t2, jnp.where(row >= 1, t1 + t2[0:1], NEG)]
        for a in range(1, half):
            nb = k1 // (a + 1)
            cands.append(jnp.where((row[0:half] >= 1) & (row[0:half] < nb), t1[a:a + 1] + t2[0:half], NEG))
        best = _top_desc(jnp.concatenate(cands, axis=0), k1)
        z = jnp.sum(jnp.exp(best[0:PEER_TOPK] - best[0:1]), axis=0, keepdims=True)
        thr = 0.5 * (best[PEER_TOPK - 1:PEER_TOPK] + best[PEER_TOPK:k1])
        e1_ref[h] = jnp.exp(s1 - t1[0:1]) * (0.5 / z)
        cnt_ref[h] = _count_above(t2, thr - s1, strict=False)
        rk_ref[h] = pltpu.bitcast(rk2.astype(BF16), jnp.int32)
        e2_ref[h] = pltpu.bitcast(jnp.exp(s2 - t2[0:1]).astype(BF16), jnp.int32)


def _peer_dense_kernel(ht_ref, e1_ref, cnt_ref, rk_ref, e2_ref, u_ref, v_ref, vlast_ref, x_ref, mod_ref, o_ref,
                       acc_ref, a_ref, g_ref, w_ref, wt_ref):
    e = pl.program_id(1)
    n_tok = acc_ref.shape[0]

    @pl.when(e == 0)
    def _():
        acc_ref[...] = jnp.zeros_like(acc_ref)
        w_ref[...] = jnp.zeros(w_ref.shape, jnp.int32)

    wt_ref[...] = pltpu.bitcast(pltpu.bitcast(w_ref[...], BF16).T, jnp.int32)

    zero = jnp.zeros((), BF16)
    for rr in range(PEER_ROWS):
        r = (e % (SUBLANES // PEER_ROWS)) * PEER_ROWS + rr
        packed_rows = slice(rr * (PEER_KEYS // 2), (rr + 1) * (PEER_KEYS // 2))
        def tile_row(ref, h):
            t16 = jnp.broadcast_to(ref[h, pl.ds(r, 1), :], (BF16_ROWS, n_tok)).astype(BF16)
            return jnp.concatenate([t16] * (PEER_KEYS // BF16_ROWS), axis=0)
        cnt_rows = [tile_row(cnt_ref, h) for h in range(PEER_HEADS)]
        e1_rows = [tile_row(e1_ref, h) for h in range(PEER_HEADS)]
        for lt in range(n_tok // 128):
            lanes = slice(lt * 128, (lt + 1) * 128)
            g = None
            for h in range(PEER_HEADS):
                sel = pltpu.bitcast(rk_ref[h, :, lanes], BF16) < cnt_rows[h][:, lanes]
                term = jnp.where(sel, pltpu.bitcast(e2_ref[h, :, lanes], BF16) * e1_rows[h][:, lanes], zero)
                g = term if g is None else g + term
            g_ref[packed_rows, lanes] = pltpu.bitcast(g, jnp.int32)
    a_ref[...] = jnp.dot(u_ref[...], ht_ref[...], preferred_element_type=F32)
    for rr in range(PEER_ROWS):
        rows = slice(rr * PEER_KEYS, (rr + 1) * PEER_KEYS)
        packed_rows = slice(rr * (PEER_KEYS // 2), (rr + 1) * (PEER_KEYS // 2))
        for lt in range(n_tok // 128):
            lanes = slice(lt * 128, (lt + 1) * 128)
            a = a_ref[rows, lanes]
            t = jnp.tanh(a * (0.7978845608028654 + (0.7978845608028654 * 0.044715) * (a * a)))
            w_ref[packed_rows, lanes] = pltpu.bitcast(
                pltpu.bitcast(g_ref[packed_rows, lanes], BF16) * (a + a * t).astype(BF16), jnp.int32)
    acc_ref[...] += jnp.dot(pltpu.bitcast(wt_ref[...], BF16), v_ref[...], preferred_element_type=F32)

    @pl.when(e == pl.num_programs(1) - 1)
    def _():
        acc = acc_ref[...] + jnp.dot(pltpu.bitcast(w_ref[...], BF16).T, vlast_ref[...], preferred_element_type=F32)
        o_ref[...] = x_ref[...] + mod_ref[0][:, 5 * D_MODEL:6 * D_MODEL] * acc


def _peer(x, mod_l, lp, latent):
    n_tok = x.shape[0]

    def mod_spec(tb):
        if latent:
            return pl.BlockSpec((1, 1, 6 * D_MODEL), lambda i, *_: (1 + i // (DEC_SEQ // tb), 0, 0))
        return pl.BlockSpec((1, 1, 6 * D_MODEL), lambda i, *_: (0, 0, 0))

    n_keys2 = PEER_HEADS * PEER_QDIM
    f32_tab = lambda n: (PEER_HEADS, PEER_KEYS, n)
    bf16_tab = lambda n: (PEER_HEADS, PEER_KEYS // 2, n)
    ht, *tabs = pl.pallas_call(
        _peer_score_kernel,
        grid=(n_tok // TBS,),
        in_specs=[pl.BlockSpec((TBS, D_MODEL), lambda i: (i, 0)), mod_spec(TBS), _const_spec((1, D_MODEL)),
                  _const_spec((n_keys2, D_MODEL)), _const_spec((2, PEER_KEYS, PEER_QDIM // 2))],
        out_specs=[pl.BlockSpec((D_MODEL, TBS), lambda i: (0, i))]
                  + [pl.BlockSpec(f32_tab(TBS), lambda i: (0, 0, i))] * 2
                  + [pl.BlockSpec(bf16_tab(TBS), lambda i: (0, 0, i))] * 2,
        out_shape=[jax.ShapeDtypeStruct((D_MODEL, n_tok), BF16),
                   jax.ShapeDtypeStruct(f32_tab(n_tok), F32), jax.ShapeDtypeStruct(f32_tab(n_tok), F32),
                   jax.ShapeDtypeStruct(bf16_tab(n_tok), jnp.int32), jax.ShapeDtypeStruct(bf16_tab(n_tok), jnp.int32)],
        compiler_params=_cparams("arbitrary"),
        name="peer_scores_lat" if latent else "peer_scores_ctx",
    )(x, mod_l, lp["norm2_g"], lp["peer_wqt"], lp["peer_subkeys"])
    nb = n_tok // TBP
    n_et = PEER_KEYS * PEER_KEYS // PEER_ET
    row_spec = pl.BlockSpec((PEER_HEADS, SUBLANES, TBP), lambda i, e: (0, e // (SUBLANES // PEER_ROWS), i))
    layer = lp["layer"]
    tile_spec = lambda tile_of: pl.BlockSpec((None, PEER_ET, D_MODEL), lambda i, e: (layer, tile_of(e), 0))
    return pl.pallas_call(
        _peer_dense_kernel,
        grid=(nb, n_et),
        in_specs=[pl.BlockSpec((D_MODEL, TBP), lambda i, e: (0, i)), row_spec, row_spec]
                 + [pl.BlockSpec(bf16_tab(TBP), lambda i, e: (0, 0, i))] * 2
                 + [tile_spec(lambda e: e),
                    tile_spec(lambda e: jnp.maximum(e - 1, 0)),
                    tile_spec(lambda e: jnp.where(e == n_et - 1, n_et - 1, 0)),
                    pl.BlockSpec((TBP, D_MODEL), lambda i, e: (i, 0)),
                    mod_spec(TBP)],
        out_specs=pl.BlockSpec((TBP, D_MODEL), lambda i, e: (i, 0)),
        out_shape=jax.ShapeDtypeStruct((n_tok, D_MODEL), F32),
        scratch_shapes=[pltpu.VMEM((TBP, D_MODEL), F32), pltpu.VMEM((PEER_ET, TBP), F32),
                        pltpu.VMEM((PEER_ET // 2, TBP), jnp.int32), pltpu.VMEM((PEER_ET // 2, TBP), jnp.int32),
                        pltpu.VMEM((TBP // 2, PEER_ET), jnp.int32)],
        compiler_params=_cparams("arbitrary", "arbitrary"),
        name="peer_dense_lat" if latent else "peer_dense_ctx",
    )(ht, *tabs, lp["peer_u"], lp["peer_v"], lp["peer_v"], x, mod_l)


def _layer_params(l, w):
    tile = lambda g, n: jnp.tile(g, n)
    gains = jnp.stack([
        tile(w["nat_qk_g"][l, 0], 4), tile(w["nat_qk_g"][l, 1], 4),
        tile(w["gqa_qk_g"][l, 0], 4), tile(w["gqa_qk_g"][l, 1], 4),
        tile(w["diff_qk_g"][l, 0], 8), tile(w["diff_qk_g"][l, 1], 8),
        jnp.zeros((256,), F32), jnp.zeros((256,), F32)])
    return dict(
        norm1_g=w["norm1_g"][l].reshape(1, D_MODEL),
        norm2_g=w["norm2_g"][l].reshape(1, D_MODEL),
        w_in=w["w_in"][l].astype(BF16),
        qk_gains=gains,
        bd64=_block_ones(256, 64), bd32=_block_ones(256, 32), bd256=_block_ones(256, 256),
        rope=w["rope"],
        sgu_norm_g=w["sgu_norm_g"][l].reshape(1, SGU_WIDTH),
        sgu_w=w["sgu_w"][l].astype(BF16),
        sgu_b=jnp.repeat(w["sgu_b"][l].T, SGU_WIDTH // SGU_GROUPS, axis=1),
        diff_lambda=w["diff_lambda"][l],
        diff_sub_g=tile(w["diff_sub_g"][l], 4).reshape(1, 256),
        nat_bias=_nat_bias_table(w["nat_rpb"][l]),
        w_branch=w["w_branch"][l].astype(BF16),
        w_gate=w["w_gate"][l].astype(BF16),
        b_gate=w["b_gate"][l].reshape(1, N_BRANCH * D_MODEL),
        w_out=w["w_out"][l].astype(BF16),
        peer_wqt=w["peer_wq"][l].T.astype(BF16),
        peer_subkeys=w["peer_subkeys"][l].astype(BF16),
        layer=l, peer_u=w["peer_u_bf16"], peer_v=w["peer_v_bf16"],
    )


def kernel(x_prompt, x_sample, c, cache_nat_k, cache_nat_v, cache_gqa_k, cache_gqa_v, cache_diff_k, cache_diff_v, c_ctx, w_mod, b_mod, norm1_g, norm2_g, w_in, nat_qk_g, nat_rpb, gqa_qk_g, diff_qk_g, diff_lambda, diff_sub_g, sgu_norm_g, sgu_w, sgu_b, w_branch, w_gate, b_gate, w_out, peer_wq, peer_subkeys, peer_u, peer_v):
    w = dict(norm1_g=norm1_g, norm2_g=norm2_g, w_in=w_in, nat_qk_g=nat_qk_g, nat_rpb=nat_rpb,
             gqa_qk_g=gqa_qk_g, diff_qk_g=diff_qk_g, diff_lambda=diff_lambda, diff_sub_g=diff_sub_g,
             sgu_norm_g=sgu_norm_g, sgu_w=sgu_w, sgu_b=sgu_b, w_branch=w_branch, w_gate=w_gate,
             b_gate=b_gate, w_out=w_out, peer_wq=peer_wq, peer_subkeys=peer_subkeys,
             peer_u_bf16=peer_u.astype(BF16), peer_v_bf16=peer_v.astype(BF16), rope=_rope_tables())
    cvec = jnp.concatenate([c_ctx[None], c, jnp.zeros((MOD_ROWS - 1 - DEC_BATCH, D_MODEL), F32)], axis=0)
    mod = _modulation(cvec, w_mod, b_mod).reshape(DEPTH, MOD_ROWS, 1, 6 * D_MODEL)
    xp = x_prompt.reshape(N_CTX, D_MODEL)
    xs = x_sample.reshape(N_LAT, D_MODEL)
    new = []
    for l in range(DEPTH):
        lp = _layer_params(l, w)
        lam_init = 0.8 - 0.6 * math.exp(-0.3 * l)
        caches = (cache_nat_k, cache_nat_v, cache_gqa_k, cache_gqa_v, cache_diff_k, cache_diff_v)
        proj = _in_projection(xp, mod[l], lp, latent=False)
        new.append(proj)
        attn = _ctx_attention(proj, lp, lam_init)
        xp = _merge(xp, mod[l], lp, [(attn, 0), (attn, 1), (attn, 2), (proj, _COL["sgu"])], latent=False)
        xp = _peer(xp, mod[l], lp, latent=False)
        proj = _in_projection(xs, mod[l], lp, latent=True)
        o_nat, o_gqa, o_dif = _lat_attention(proj, caches, lp, l, lam_init)
        xs = _merge(xs, mod[l], lp, [(o_nat, 0), (o_gqa, 0), (o_dif, 0), (proj, _COL["sgu"])], latent=True)
        xs = _peer(xs, mod[l], lp, latent=True)

    def cache_out(lo, width, tail):
        a = jnp.stack([p[:, lo:lo + width].reshape(BATCH, SEQ, width) for p in new], axis=1)
        return a.reshape((BATCH, DEPTH, SEQ) + tail)

    return (xp.reshape(BATCH, SEQ, D_MODEL), xs.reshape(DEC_BATCH, DEC_SEQ, D_MODEL),
            cache_out(256, 256, (NAT_HEADS, HEAD_DIM)), cache_out(512, 256, (NAT_HEADS, HEAD_DIM)),
            cache_out(1024, 128, (GQA_KV_HEADS, HEAD_DIM)), cache_out(1152, 128, (GQA_KV_HEADS, HEAD_DIM)),
            cache_out(1536, 256, (DIFF_HEADS, 2, DIFF_DIM)), cache_out(1792, 256, (DIFF_HEADS, HEAD_DIM)))
```

```python
import functools
import math

import numpy as np
import jax
import jax.numpy as jnp
from jax import lax
from jax.experimental import pallas as pl
from jax.experimental.pallas import tpu as pltpu

F32 = jnp.float32
BF16 = jnp.bfloat16

D_MODEL = 1024
BATCH = 16
SEQ = 256
DEPTH = 2
DEC_BATCH = 2
DEC_SEQ = 4096
PAST_LEN = 512
GRID_W = 64
HEAD_DIM = 64
NAT_HEADS = 4
NAT_ROWS = 8
NAT_COLS = 16
GQA_HEADS = 4
GQA_KV_HEADS = 2
DIFF_HEADS = 4
DIFF_DIM = 32
SGU_WIDTH = 256
SGU_GROUPS = 4
CHUNK = 128
N_BRANCH = 4
BRANCH_W = 256
IN_W = 2560
PEER_HEADS = 8
PEER_KEYS = 128
PEER_QDIM = 256
PEER_TOPK = 16
ROPE_BASE = 10000.0
EPS = 1e-6

N_CTX = BATCH * SEQ
N_LAT = DEC_BATCH * DEC_SEQ
MOD_ROWS = 8
TB = 512
TQ = 256
PROJ_W = 2304
NAT_QROWS = 4
NAT_KROWS = NAT_QROWS + NAT_ROWS
TBS = 512
TBP = 1024
PEER_ET = 512
PEER_ROWS = PEER_ET // PEER_KEYS
SUBLANES = 8
LANES = 128
GELU_C0 = 0.7978845608028654
GELU_C1 = GELU_C0 * 0.044715
BF16_ROWS = 16
VMEM_LIMIT = 56 * 1024 * 1024
NEG = -1e30


def _cparams(*sem):
    return pltpu.CompilerParams(dimension_semantics=sem, vmem_limit_bytes=VMEM_LIMIT)


def _const_spec(shape):
    return pl.BlockSpec(shape, lambda *_: (0,) * len(shape))


def _gelu(x):
    return 0.5 * x * (1.0 + jnp.tanh(x * (GELU_C0 + GELU_C1 * (x * x))))


def _sigmoid(x):
    return 1.0 / (1.0 + jnp.exp(-x))


def _dot(a, b):
    return jnp.dot(a.astype(BF16), b.astype(BF16), preferred_element_type=F32)


def _dot_nt(a, b):
    return lax.dot_general(a.astype(BF16), b.astype(BF16), (((1,), (1,)), ((), ())),
                           preferred_element_type=F32)


def _group_mean(y2, ones_bd):
    hi = y2.astype(BF16)
    lo = (y2 - hi.astype(F32)).astype(BF16)
    return (jnp.dot(hi, ones_bd, preferred_element_type=F32)
            + jnp.dot(lo, ones_bd, preferred_element_type=F32))


def _block_ones(width, group):
    idx = np.arange(width) // group
    return jnp.asarray((idx[:, None] == idx[None, :]).astype(np.float32) / group, dtype=BF16)


def _rms_mod(x, gain, scale, shift):
    xn = x * lax.rsqrt(jnp.mean(x * x, axis=-1, keepdims=True) + EPS) * gain
    return xn * (1.0 + scale) + shift


def _mod_spec(latent):
    if latent:
        return pl.BlockSpec((1, 1, 6 * D_MODEL), lambda i, *_: (1 + i // (DEC_SEQ // TB), 0, 0))
    return pl.BlockSpec((1, 1, 6 * D_MODEL), lambda i, *_: (0, 0, 0))


def _mod_kernel(c_ref, w_ref, b_ref, o_ref):
    c = c_ref[...]
    s = c * _sigmoid(c)
    o_ref[0] = _dot(s, w_ref[0]) + b_ref[0]


def _modulation(cvec, w_mod, b_mod):
    tn = 1536
    return pl.pallas_call(
        _mod_kernel,
        grid=(DEPTH, 6 * D_MODEL // tn),
        in_specs=[pl.BlockSpec((MOD_ROWS, D_MODEL), lambda l, j: (0, 0)),
                  pl.BlockSpec((1, D_MODEL, tn), lambda l, j: (l, 0, j)),
                  pl.BlockSpec((1, 1, tn), lambda l, j: (l, 0, j))],
        out_specs=pl.BlockSpec((1, MOD_ROWS, tn), lambda l, j: (l, 0, j)),
        out_shape=jax.ShapeDtypeStruct((DEPTH, MOD_ROWS, 6 * D_MODEL), F32),
        compiler_params=_cparams("arbitrary", "arbitrary"),
        name="modulation",
    )(cvec, w_mod, b_mod.reshape(DEPTH, 1, 6 * D_MODEL))


def _rope_tables():
    t = np.arange(DEC_SEQ)
    pos = (t // GRID_W, t % GRID_W)
    out = []
    for d in (HEAD_DIM, DIFF_DIM):
        half, quarter = d // 2, d // 4
        inv = ROPE_BASE ** (-np.arange(quarter, dtype=np.float32) * 2.0 / half)
        lane = np.arange(LANES) % d
        part, j = lane // half, lane % half
        ang = np.stack([pos[0][:, None] * inv[None, :], pos[1][:, None] * inv[None, :]], axis=1)
        a = ang[:, part, j % quarter].astype(np.float32)
        cos, sin = np.cos(a), np.sin(a)
        out += [jnp.asarray(cos, dtype=F32),
                jnp.asarray(np.where(j < quarter, -sin, 0.0), dtype=F32),
                jnp.asarray(np.where(j >= quarter, sin, 0.0), dtype=F32)]
    return out


def _rope(y, c, sa, sb, quarter):
    w = y.shape[-1]
    rep = w // LANES
    if rep > 1:
        c, sa, sb = (jnp.concatenate([t] * rep, axis=-1) for t in (c, sa, sb))
    up = pltpu.roll(y, w - quarter, 1)
    dn = pltpu.roll(y, quarter, 1)
    return y * c + up * sa + dn * sb


def _inproj_kernel(*refs, latent):
    (x_ref, mod_ref, n1_ref, w_ref, g_ref, bd64_ref, bd32_ref, bd256_ref,
     sgn_ref, sgw_ref, sgb_ref) = refs[:11]
    o_ref = refs[-1]
    mod = mod_ref[0]
    h = _rms_mod(x_ref[...], n1_ref[...], mod[:, D_MODEL:2 * D_MODEL], mod[:, 0:D_MODEL])
    y = jnp.dot(h.astype(BF16), w_ref[...], preferred_element_type=F32)

    def qk_norm(lo, width, bd, gain_row):
        v = y[:, lo:lo + width]
        ms = _group_mean(v * v, bd)
        return v * lax.rsqrt(ms + EPS) * g_ref[gain_row:gain_row + 1, 0:width]

    bd64 = bd64_ref[...]
    bd32 = bd32_ref[...]
    if latent:
        c64, sa64, sb64, c32, sa32, sb32 = (r[...] for r in refs[11:17])
        rope64 = functools.partial(_rope, c=c64, sa=sa64, sb=sb64, quarter=HEAD_DIM // 4)
        rope32 = functools.partial(_rope, c=c32, sa=sa32, sb=sb32, quarter=DIFF_DIM // 4)
    else:
        rope64 = rope32 = lambda v: v
    o_ref[:, 0:256] = qk_norm(0, 256, bd64, 0)
    o_ref[:, 256:512] = qk_norm(256, 256, bd64, 1)
    o_ref[:, 512:768] = y[:, 512:768]
    o_ref[:, 768:1024] = rope64(qk_norm(768, 256, bd64, 2))
    o_ref[:, 1024:1152] = rope64(qk_norm(1024, 128, bd64[0:128, 0:128], 3))
    o_ref[:, 1152:1280] = y[:, 1152:1280]
    o_ref[:, 1280:1536] = rope32(qk_norm(1280, 256, bd32, 4))
    o_ref[:, 1536:1792] = rope32(qk_norm(1536, 256, bd32, 5))
    o_ref[:, 1792:2048] = y[:, 1792:2048]
    u = _gelu(y[:, 2048:2304])
    v = _gelu(y[:, 2304:2560])
    vn = v * lax.rsqrt(_group_mean(v * v, bd256_ref[...]) + EPS) * sgn_ref[...]
    vnb = vn.astype(BF16)
    lane_group = lax.broadcasted_iota(jnp.int32, (CHUNK, SGU_WIDTH), 1) // (SGU_WIDTH // SGU_GROUPS)
    for n in range(TB // CHUNK):
        vc = vnb[n * CHUNK:(n + 1) * CHUNK, :]
        s = sgb_ref[...]
        for g in range(SGU_GROUPS):
            sg = jnp.dot(sgw_ref[g], vc, preferred_element_type=F32)
            s = s + jnp.where(lane_group == g, sg, 0.0)
        o_ref[n * CHUNK:(n + 1) * CHUNK, 2048:2304] = u[n * CHUNK:(n + 1) * CHUNK, :] * s


def _in_projection(x, mod_l, lp, latent):
    n_tok = x.shape[0]
    in_specs = [pl.BlockSpec((TB, D_MODEL), lambda i: (i, 0)),
                _mod_spec(latent),
                _const_spec((1, D_MODEL)),
                _const_spec((D_MODEL, IN_W)),
                _const_spec((8, 256)),
                _const_spec((256, 256)), _const_spec((256, 256)), _const_spec((256, 256)),
                _const_spec((1, SGU_WIDTH)),
                _const_spec((SGU_GROUPS, CHUNK, CHUNK)),
                _const_spec((CHUNK, SGU_WIDTH))]
    args = [x, mod_l, lp["norm1_g"], lp["w_in"], lp["qk_gains"], lp["bd64"], lp["bd32"], lp["bd256"],
            lp["sgu_norm_g"], lp["sgu_w"], lp["sgu_b"]]
    if latent:
        in_specs += [pl.BlockSpec((TB, LANES), lambda i: (i % (DEC_SEQ // TB), 0))] * 6
        args += lp["rope"]
    return pl.pallas_call(
        functools.partial(_inproj_kernel, latent=latent),
        grid=(n_tok // TB,),
        in_specs=in_specs,
        out_specs=pl.BlockSpec((TB, PROJ_W), lambda i: (i, 0)),
        out_shape=jax.ShapeDtypeStruct((n_tok, PROJ_W), F32),
        compiler_params=_cparams("arbitrary"),
        name="in_projection_lat" if latent else "in_projection_ctx",
    )(*args)


LOG2E = 1.4426950408889634


def _softmax_parts(scores):
    m = functools.reduce(jnp.maximum, [jnp.max(s, axis=-1, keepdims=True) for s in scores])
    ps = [jnp.exp2(s - m) for s in scores]
    l = functools.reduce(jnp.add, [jnp.sum(p, axis=-1, keepdims=True) for p in ps])
    return [p.astype(BF16) for p in ps], 1.0 / l


def _diff_lambda(lam_ref, lam_init):
    lv = lam_ref[...]
    a = jnp.sum(lv[0:1] * lv[1:2], axis=-1, keepdims=True)
    b = jnp.sum(lv[2:3] * lv[3:4], axis=-1, keepdims=True)
    return jnp.exp(a) - jnp.exp(b) + lam_init


def _head(ref_or_val, h, width=HEAD_DIM):
    return ref_or_val[:, h * width:(h + 1) * width]


def _mha(q, ks, vs, n_heads, kv_group, scale, biases=None):
    ks, vs = [k.astype(BF16) for k in ks], [v.astype(BF16) for v in vs]
    qs = (q * (scale * LOG2E)).astype(BF16)

    def scores_of(h):
        scores = [_dot_nt(_head(qs, h), _head(k, h // kv_group)) for k in ks]
        if biases is not None:
            scores = [s if b is None else s + b[h] * LOG2E for s, b in zip(scores, biases)]
        return scores

    def finish(h, ps, rl):
        return functools.reduce(jnp.add, [_dot(p, _head(v, h // kv_group)) for p, v in zip(ps, vs)]) * rl

    return jnp.concatenate(_softmax_pipeline(n_heads, scores_of, finish), axis=-1)


def _softmax_pipeline(n, scores_of, finish):
    outs = []
    scores = scores_of(0)
    for i in range(n):
        ps, rl = _softmax_parts(scores)
        if i + 1 < n:
            scores = scores_of(i + 1)
        outs.append(finish(i, ps, rl))
    return outs


def _diff_attn(q, ks, vs, lam, sub_gain, bd64, lam_init):
    qs = (q * (DIFF_DIM ** -0.5 * LOG2E)).astype(BF16)
    ks, vs = [k.astype(BF16) for k in ks], [v.astype(BF16) for v in vs]
    scores_of = lambda i: [_dot_nt(_head(qs, i, DIFF_DIM), _head(k, i, DIFF_DIM)) for k in ks]
    finish = lambda i, ps, rl: functools.reduce(jnp.add, [_dot(p, _head(v, i // 2)) for p, v in zip(ps, vs)]) * rl
    pv = _softmax_pipeline(2 * DIFF_HEADS, scores_of, finish)
    o = jnp.concatenate([pv[2 * h] - lam * pv[2 * h + 1] for h in range(DIFF_HEADS)], axis=-1)
    ms = _group_mean(o * o, bd64)
    return o * lax.rsqrt(ms + EPS) * sub_gain * (1.0 - lam_init)


def _ctx_attn_kernel(nq, nk, nv, gq, gk, gv, dq, dk, dv, lam_ref, subg_ref, bd64_ref, o_ref, *, lam_init):
    scale = HEAD_DIM ** -0.5
    o_ref[:, 0:256] = _mha(nq[...], [nk[...]], [nv[...]], NAT_HEADS, 1, scale)
    o_ref[:, 256:512] = _mha(gq[...], [gk[...]], [gv[...]], GQA_HEADS, GQA_HEADS // GQA_KV_HEADS, scale)
    lam = _diff_lambda(lam_ref, lam_init)
    o_ref[:, 512:768] = _diff_attn(dq[...], [dk[...]], [dv[...]], lam, subg_ref[...], bd64_ref[...], lam_init)


_COL = dict(nq=0, nk=1, nv=2, gq=3, gk=8, gv=9, dq=5, dk=6, dv=7, sgu=8)


def _ctx_attention(proj, lp, lam_init):
    blk = lambda name, w: pl.BlockSpec((SEQ, w), lambda b: (b, _COL[name]))
    return pl.pallas_call(
        functools.partial(_ctx_attn_kernel, lam_init=lam_init),
        grid=(BATCH,),
        in_specs=[blk("nq", 256), blk("nk", 256), blk("nv", 256),
                  blk("gq", 256), blk("gk", 128), blk("gv", 128),
                  blk("dq", 256), blk("dk", 256), blk("dv", 256),
                  _const_spec((4, DIFF_DIM)), _const_spec((1, 256)), _const_spec((256, 256))],
        out_specs=pl.BlockSpec((SEQ, 768), lambda b: (b, 0)),
        out_shape=jax.ShapeDtypeStruct((N_CTX, 768), F32),
        compiler_params=_cparams("arbitrary"),
        name="ctx_attention",
    )(*([proj] * 9), lp["diff_lambda"], lp["diff_sub_g"], lp["bd64"])


def _gqa_lat_kernel(q_ref, k_ref, v_ref, ck_ref, cv_ref, o_ref):
    o_ref[...] = _mha(q_ref[...], [k_ref[...], ck_ref[...]], [v_ref[...], cv_ref[...]],
                      GQA_HEADS, GQA_HEADS // GQA_KV_HEADS, HEAD_DIM ** -0.5)


def _diff_lat_kernel(q_ref, k_ref, v_ref, ck_ref, cv_ref, lam_ref, subg_ref, bd64_ref, o_ref, *, lam_init):
    lam = _diff_lambda(lam_ref, lam_init)
    o_ref[...] = _diff_attn(q_ref[...], [k_ref[...], ck_ref[...]], [v_ref[...], cv_ref[...]],
                            lam, subg_ref[...], bd64_ref[...], lam_init)


def _nat_lat_kernel(q_ref, k_ref, v_ref, ck_ref, cv_ref, bias_ref, o_ref):
    i = pl.program_id(1)
    k_row0 = jnp.clip(NAT_QROWS * i - NAT_ROWS // 2, 0, GRID_W - NAT_KROWS)
    start = pl.multiple_of(k_row0 * GRID_W, GRID_W)
    kw = k_ref[pl.ds(start, NAT_KROWS * GRID_W), :]
    vw = v_ref[pl.ds(start, NAT_KROWS * GRID_W), :]
    o_ref[...] = _mha(q_ref[...], [kw, ck_ref[...]], [vw, cv_ref[...]], NAT_HEADS, 1, HEAD_DIM ** -0.5,
                      biases=[bias_ref[0], None])


def _nat_bias_table(rpb):
    rows = DEC_SEQ // GRID_W
    nblk = rows // NAT_QROWS
    pad = jnp.pad(rpb, ((0, 0), (0, 0), (GRID_W - NAT_COLS, GRID_W - NAT_COLS)))
    toep = jnp.stack([pad[:, :, GRID_W - 1 - c:2 * GRID_W - 1 - c] for c in range(GRID_W)], axis=2)
    col = np.arange(GRID_W)
    cs = np.clip(col - NAT_COLS // 2, 0, GRID_W - NAT_COLS)
    col_ok = (col[None, :] >= cs[:, None]) & (col[None, :] < cs[:, None] + NAT_COLS)
    toep = jnp.where(col_ok, toep, NEG)
    masked = jnp.full((NAT_HEADS, GRID_W, GRID_W), NEG, F32)
    cases = []
    for blk in (0, 1, nblk - 1):
        r0 = blk * NAT_QROWS
        k0 = int(np.clip(r0 - NAT_ROWS // 2, 0, rows - NAT_KROWS))
        q_rows = []
        for qr in range(r0, r0 + NAT_QROWS):
            rs = int(np.clip(qr - NAT_ROWS // 2, 0, rows - NAT_ROWS))
            q_rows.append(jnp.concatenate(
                [toep[:, kr - qr + NAT_ROWS - 1] if rs <= kr < rs + NAT_ROWS else masked
                 for kr in range(k0, k0 + NAT_KROWS)], axis=-1))
        cases.append(jnp.concatenate(q_rows, axis=-2))
    return jnp.stack(cases, axis=0)


def _lat_attention(proj, caches, lp, l, lam_init):
    cnk, cnv, cgk, cgv, cdk, cdv = caches
    nq_blocks = DEC_SEQ // TQ
    qspec = lambda name: pl.BlockSpec((TQ, 256), lambda b, i: (b * nq_blocks + i, _COL[name]))
    kvspec = lambda name, w: pl.BlockSpec((DEC_SEQ, w), lambda b, i: (b, _COL[name]))
    cspec = lambda w: pl.BlockSpec((None, None, PAST_LEN, w), lambda b, i: (b, l, 0, 0))
    ospec = pl.BlockSpec((TQ, 256), lambda b, i: (b * nq_blocks + i, 0))
    oshape = jax.ShapeDtypeStruct((N_LAT, 256), F32)
    o_gqa = pl.pallas_call(
        _gqa_lat_kernel,
        grid=(DEC_BATCH, nq_blocks),
        in_specs=[qspec("gq"), kvspec("gk", 128), kvspec("gv", 128), cspec(128), cspec(128)],
        out_specs=ospec, out_shape=oshape,
        compiler_params=_cparams("arbitrary", "arbitrary"),
        name="gqa_lat_attention",
    )(proj, proj, proj, cgk.reshape(DEC_BATCH, DEPTH, PAST_LEN, 128), cgv.reshape(DEC_BATCH, DEPTH, PAST_LEN, 128))
    o_dif = pl.pallas_call(
        functools.partial(_diff_lat_kernel, lam_init=lam_init),
        grid=(DEC_BATCH, nq_blocks),
        in_specs=[qspec("dq"), kvspec("dk", 256), kvspec("dv", 256), cspec(256), cspec(256),
                  _const_spec((4, DIFF_DIM)), _const_spec((1, 256)), _const_spec((256, 256))],
        out_specs=ospec, out_shape=oshape,
        compiler_params=_cparams("arbitrary", "arbitrary"),
        name="diff_lat_attention",
    )(proj, proj, proj, cdk.reshape(DEC_BATCH, DEPTH, PAST_LEN, 256), cdv.reshape(DEC_BATCH, DEPTH, PAST_LEN, 256),
      lp["diff_lambda"], lp["diff_sub_g"], lp["bd64"])
    nblk = DEC_SEQ // (NAT_QROWS * GRID_W)
    nat_q = NAT_QROWS * GRID_W
    o_nat = pl.pallas_call(
        _nat_lat_kernel,
        grid=(DEC_BATCH, nblk),
        in_specs=[pl.BlockSpec((nat_q, 256), lambda b, i: (b * nblk + i, _COL["nq"])),
                  kvspec("nk", 256), kvspec("nv", 256), cspec(256), cspec(256),
                  pl.BlockSpec((1, NAT_HEADS, nat_q, NAT_KROWS * GRID_W),
                               lambda b, i: (jnp.where(i == 0, 0, jnp.where(i == nblk - 1, 2, 1)), 0, 0, 0))],
        out_specs=pl.BlockSpec((nat_q, 256), lambda b, i: (b * nblk + i, 0)),
        out_shape=oshape,
        compiler_params=_cparams("arbitrary", "arbitrary"),
        name="nat_lat_attention",
    )(proj, proj, proj, cnk.reshape(DEC_BATCH, DEPTH, PAST_LEN, 256), cnv.reshape(DEC_BATCH, DEPTH, PAST_LEN, 256),
      lp["nat_bias"])
    return o_nat, o_gqa, o_dif


def _merge_kernel(x_ref, mod_ref, n1_ref, b0_ref, b1_ref, b2_ref, b3_ref, wb_ref, wg_ref, bg_ref, wo_ref, o_ref):
    x = x_ref[...]
    mod = mod_ref[0]
    hb = _rms_mod(x, n1_ref[...], mod[:, D_MODEL:2 * D_MODEL], mod[:, 0:D_MODEL]).astype(BF16)
    merged = None
    for n, b_ref in enumerate((b0_ref, b1_ref, b2_ref, b3_ref)):
        cols = slice(n * D_MODEL, (n + 1) * D_MODEL)
        gate = _sigmoid(jnp.dot(hb, wg_ref[:, cols], preferred_element_type=F32) + bg_ref[:, cols])
        term = gate * _dot(b_ref[...], wb_ref[n])
        merged = term if merged is None else merged + term
    out = _dot(merged, wo_ref[...])
    o_ref[...] = x + mod[:, 2 * D_MODEL:3 * D_MODEL] * out


def _merge(x, mod_l, lp, branches, latent):
    n_tok = x.shape[0]
    return pl.pallas_call(
        _merge_kernel,
        grid=(n_tok // TB,),
        in_specs=[pl.BlockSpec((TB, D_MODEL), lambda i: (i, 0)), _mod_spec(latent), _const_spec((1, D_MODEL))]
                 + [pl.BlockSpec((TB, BRANCH_W), functools.partial(lambda i, c: (i, c), c=col)) for _, col in branches]
                 + [_const_spec((N_BRANCH, BRANCH_W, D_MODEL)), _const_spec((D_MODEL, N_BRANCH * D_MODEL)),
                    _const_spec((1, N_BRANCH * D_MODEL)), _const_spec((D_MODEL, D_MODEL))],
        out_specs=pl.BlockSpec((TB, D_MODEL), lambda i: (i, 0)),
        out_shape=jax.ShapeDtypeStruct((n_tok, D_MODEL), F32),
        compiler_params=_cparams("arbitrary"),
        name="merge_lat" if latent else "merge_ctx",
    )(x, mod_l, lp["norm1_g"], *[a for a, _ in branches], lp["w_branch"], lp["w_gate"], lp["b_gate"], lp["w_out"])


def _oddeven_merge_pairs(n):
    pairs = []
    p = 1
    while p < n:
        k = p
        while k >= 1:
            for j in range(k % p, n - k, 2 * k):
                for i in range(min(k, n - j - k)):
                    if (i + j) // (2 * p) == (i + j + k) // (2 * p):
                        pairs.append((i + j, i + j + k))
            k //= 2
        p *= 2
    return pairs


def _top_desc_sorted(s, k):
    n = s.shape[0] // SUBLANES
    tiles = [s[j * SUBLANES:(j + 1) * SUBLANES, :] for j in range(n)]
    for a, b in _oddeven_merge_pairs(n):
        tiles[a], tiles[b] = jnp.maximum(tiles[a], tiles[b]), jnp.minimum(tiles[a], tiles[b])
    rows = []
    for i in range(k):
        m = jnp.max(tiles[0], axis=0, keepdims=True)
        rows.append(m)
        hit = tiles[0] == m
        for j in range(min(n, k - 1 - i)):
            tiles[j] = jnp.where(hit, tiles[j + 1] if j + 1 < n else NEG, tiles[j])
    return jnp.concatenate(rows, axis=0)


def _count_above(t, x, strict):
    assert PEER_TOPK == 16, "the search tree below is written out for 16 sorted rows"
    above = (lambda p: p > x) if strict else (lambda p: p >= x)
    row = lambda b: t[b:b + 1]
    b8 = above(row(7))
    b4 = above(jnp.where(b8, row(11), row(3)))
    b2 = above(jnp.where(b8, jnp.where(b4, row(13), row(9)), jnp.where(b4, row(5), row(1))))
    hi = jnp.where(b4, jnp.where(b2, row(14), row(12)), jnp.where(b2, row(10), row(8)))
    lo = jnp.where(b4, jnp.where(b2, row(6), row(4)), jnp.where(b2, row(2), row(0)))
    b1 = above(jnp.where(b8, hi, lo))
    count = jnp.where(b8, 8.0, 0.0) + jnp.where(b4, 4.0, 0.0) + jnp.where(b2, 2.0, 0.0) + jnp.where(b1, 1.0, 0.0)
    return count + jnp.where(above(row(15)), 1.0, 0.0)


def _peer_score_kernel(x_ref, mod_ref, n2_ref, wqt_ref, sk_ref, ht_ref, e1_ref, cnt_ref, rk_ref, e2_ref):
    mod = mod_ref[0]
    h2 = _rms_mod(x_ref[...], n2_ref[...], mod[:, 4 * D_MODEL:5 * D_MODEL], mod[:, 3 * D_MODEL:4 * D_MODEL])
    htb = h2.T.astype(BF16)
    ht_ref[...] = htb
    qt = jnp.dot(wqt_ref[...], htb, preferred_element_type=F32)
    k1 = PEER_TOPK + 1
    half = PEER_TOPK // 2
    row = lax.broadcasted_iota(jnp.int32, (half, 1), 0)
    neg_tile = jnp.full((half, qt.shape[-1]), NEG, F32)
    for h in range(PEER_HEADS):
        s1, s2 = (jnp.dot(sk_ref[p], qt[(2 * h + p) * PEER_KEYS:(2 * h + p + 1) * PEER_KEYS, :].astype(BF16),
                          preferred_element_type=F32) for p in range(2))
        t1 = _top_desc_sorted(s1, k1)
        t2 = _top_desc_sorted(s2, k1)
        rk2 = _count_above(t2, s2, strict=True)
        cands = [t1[0:1] + t2[0:half], t1[0:1] + t2[half:PEER_TOPK],
                 jnp.where(row >= 1, t1[0:half] + t2[0:1], NEG), t1[half:PEER_TOPK] + t2[0:1],
                 jnp.where(row == 0, t1[0:1] + t2[PEER_TOPK:k1],
                           jnp.where(row == 1, t1[PEER_TOPK:k1] + t2[0:1], NEG))]
        for a in range(1, half):
            nb = k1 // (a + 1)
            cands.append(jnp.where((row >= 1) & (row < nb), t1[a:a + 1] + t2[0:half], NEG))
        cands += [neg_tile] * (PEER_TOPK - len(cands))
        best = _top_desc_sorted(jnp.concatenate(cands, axis=0), k1)
        z = jnp.sum(jnp.exp(best[0:PEER_TOPK] - best[0:1]), axis=0, keepdims=True)
        thr = 0.5 * (best[PEER_TOPK - 1:PEER_TOPK] + best[PEER_TOPK:k1])
        e1_ref[h] = jnp.exp(s1 - t1[0:1]) * (0.5 / z)
        cnt_ref[h] = _count_above(t2, thr - s1, strict=False)
        rk_ref[h] = pltpu.bitcast(rk2.astype(BF16), jnp.int32)
        e2_ref[h] = pltpu.bitcast(jnp.exp(s2 - t2[0:1]).astype(BF16), jnp.int32)


def _peer_dense_kernel(ht_ref, e1_ref, cnt_ref, rk_ref, e2_ref, u_ref, v_ref, vlast_ref, x_ref, mod_ref, o_ref,
                       acc_ref, a_ref, g_ref, w_ref, wt_ref):
    e = pl.program_id(1)
    n_tok = acc_ref.shape[0]

    @pl.when(e == 0)
    def _():
        acc_ref[...] = jnp.zeros_like(acc_ref)
        w_ref[...] = jnp.zeros(w_ref.shape, jnp.int32)

    wt_ref[...] = pltpu.bitcast(pltpu.bitcast(w_ref[...], BF16).T, jnp.int32)

    zero = jnp.zeros((), BF16)
    for rr in range(PEER_ROWS):
        r = (e % (SUBLANES // PEER_ROWS)) * PEER_ROWS + rr
        packed_rows = slice(rr * (PEER_KEYS // 2), (rr + 1) * (PEER_KEYS // 2))
        def tile_row(ref, h):
            t16 = jnp.broadcast_to(ref[h, pl.ds(r, 1), :], (BF16_ROWS, n_tok)).astype(BF16)
            return jnp.concatenate([t16] * (PEER_KEYS // BF16_ROWS), axis=0)
        cnt_rows = [tile_row(cnt_ref, h) for h in range(PEER_HEADS)]
        e1_rows = [tile_row(e1_ref, h) for h in range(PEER_HEADS)]
        for lt in range(n_tok // LANES):
            lanes = slice(lt * LANES, (lt + 1) * LANES)
            g = None
            for h in range(PEER_HEADS):
                sel = pltpu.bitcast(rk_ref[h, :, lanes], BF16) < cnt_rows[h][:, lanes]
                term = jnp.where(sel, pltpu.bitcast(e2_ref[h, :, lanes], BF16) * e1_rows[h][:, lanes], zero)
                g = term if g is None else g + term
            g_ref[packed_rows, lanes] = pltpu.bitcast(g, jnp.int32)
    a_ref[...] = jnp.dot(u_ref[...], ht_ref[...], preferred_element_type=F32)
    for rr in range(PEER_ROWS):
        rows = slice(rr * PEER_KEYS, (rr + 1) * PEER_KEYS)
        packed_rows = slice(rr * (PEER_KEYS // 2), (rr + 1) * (PEER_KEYS // 2))
        for lt in range(n_tok // LANES):
            lanes = slice(lt * LANES, (lt + 1) * LANES)
            a = a_ref[rows, lanes]
            t = jnp.tanh(a * (GELU_C0 + GELU_C1 * (a * a)))
            w_ref[packed_rows, lanes] = pltpu.bitcast(
                pltpu.bitcast(g_ref[packed_rows, lanes], BF16) * (a + a * t).astype(BF16), jnp.int32)
    acc_ref[...] += jnp.dot(pltpu.bitcast(wt_ref[...], BF16), v_ref[...], preferred_element_type=F32)

    @pl.when(e == pl.num_programs(1) - 1)
    def _():
        acc = acc_ref[...] + jnp.dot(pltpu.bitcast(w_ref[...], BF16).T, vlast_ref[...], preferred_element_type=F32)
        o_ref[...] = x_ref[...] + mod_ref[0][:, 5 * D_MODEL:6 * D_MODEL] * acc


def _peer(x, mod_l, lp, latent):
    n_tok = x.shape[0]

    def mod_spec(tb):
        if latent:
            return pl.BlockSpec((1, 1, 6 * D_MODEL), lambda i, *_: (1 + i // (DEC_SEQ // tb), 0, 0))
        return pl.BlockSpec((1, 1, 6 * D_MODEL), lambda i, *_: (0, 0, 0))

    n_keys2 = PEER_HEADS * PEER_QDIM
    f32_tab = lambda n: (PEER_HEADS, PEER_KEYS, n)
    bf16_tab = lambda n: (PEER_HEADS, PEER_KEYS // 2, n)
    ht, *tabs = pl.pallas_call(
        _peer_score_kernel,
        grid=(n_tok // TBS,),
        in_specs=[pl.BlockSpec((TBS, D_MODEL), lambda i: (i, 0)), mod_spec(TBS), _const_spec((1, D_MODEL)),
                  _const_spec((n_keys2, D_MODEL)), _const_spec((2, PEER_KEYS, PEER_QDIM // 2))],
        out_specs=[pl.BlockSpec((D_MODEL, TBS), lambda i: (0, i))]
                  + [pl.BlockSpec(f32_tab(TBS), lambda i: (0, 0, i))] * 2
                  + [pl.BlockSpec(bf16_tab(TBS), lambda i: (0, 0, i))] * 2,
        out_shape=[jax.ShapeDtypeStruct((D_MODEL, n_tok), BF16),
                   jax.ShapeDtypeStruct(f32_tab(n_tok), F32), jax.ShapeDtypeStruct(f32_tab(n_tok), F32),
                   jax.ShapeDtypeStruct(bf16_tab(n_tok), jnp.int32), jax.ShapeDtypeStruct(bf16_tab(n_tok), jnp.int32)],
        compiler_params=_cparams("arbitrary"),
        name="peer_scores_lat" if latent else "peer_scores_ctx",
    )(x, mod_l, lp["norm2_g"], lp["peer_wqt"], lp["peer_subkeys"])
    nb = n_tok // TBP
    n_et = PEER_KEYS * PEER_KEYS // PEER_ET
    row_spec = pl.BlockSpec((PEER_HEADS, SUBLANES, TBP), lambda i, e: (0, e // (SUBLANES // PEER_ROWS), i))
    layer = lp["layer"]
    tile_spec = lambda tile_of: pl.BlockSpec((None, PEER_ET, D_MODEL), lambda i, e: (layer, tile_of(e), 0))
    return pl.pallas_call(
        _peer_dense_kernel,
        grid=(nb, n_et),
        in_specs=[pl.BlockSpec((D_MODEL, TBP), lambda i, e: (0, i)), row_spec, row_spec]
                 + [pl.BlockSpec(bf16_tab(TBP), lambda i, e: (0, 0, i))] * 2
                 + [tile_spec(lambda e: e),
                    tile_spec(lambda e: jnp.maximum(e - 1, 0)),
                    tile_spec(lambda e: jnp.where(e == n_et - 1, n_et - 1, 0)),
                    pl.BlockSpec((TBP, D_MODEL), lambda i, e: (i, 0)),
                    mod_spec(TBP)],
        out_specs=pl.BlockSpec((TBP, D_MODEL), lambda i, e: (i, 0)),
        out_shape=jax.ShapeDtypeStruct((n_tok, D_MODEL), F32),
        scratch_shapes=[pltpu.VMEM((TBP, D_MODEL), F32), pltpu.VMEM((PEER_ET, TBP), F32),
                        pltpu.VMEM((PEER_ET // 2, TBP), jnp.int32), pltpu.VMEM((PEER_ET // 2, TBP), jnp.int32),
                        pltpu.VMEM((TBP // 2, PEER_ET), jnp.int32)],
        compiler_params=_cparams("arbitrary", "arbitrary"),
        name="peer_dense_lat" if latent else "peer_dense_ctx",
    )(ht, *tabs, lp["peer_u"], lp["peer_v"], lp["peer_v"], x, mod_l)


def _layer_params(l, w):
    tile = lambda g, n: jnp.tile(g, n)
    gains = jnp.stack([
        tile(w["nat_qk_g"][l, 0], 4), tile(w["nat_qk_g"][l, 1], 4),
        tile(w["gqa_qk_g"][l, 0], 4), tile(w["gqa_qk_g"][l, 1], 4),
        tile(w["diff_qk_g"][l, 0], 8), tile(w["diff_qk_g"][l, 1], 8),
        jnp.zeros((256,), F32), jnp.zeros((256,), F32)])
    return dict(
        norm1_g=w["norm1_g"][l].reshape(1, D_MODEL),
        norm2_g=w["norm2_g"][l].reshape(1, D_MODEL),
        w_in=w["w_in"][l].astype(BF16),
        qk_gains=gains,
        bd64=_block_ones(256, 64), bd32=_block_ones(256, 32), bd256=_block_ones(256, 256),
        rope=w["rope"],
        sgu_norm_g=w["sgu_norm_g"][l].reshape(1, SGU_WIDTH),
        sgu_w=w["sgu_w"][l].astype(BF16),
        sgu_b=jnp.repeat(w["sgu_b"][l].T, SGU_WIDTH // SGU_GROUPS, axis=1),
        diff_lambda=w["diff_lambda"][l],
        diff_sub_g=tile(w["diff_sub_g"][l], 4).reshape(1, 256),
        nat_bias=_nat_bias_table(w["nat_rpb"][l]),
        w_branch=w["w_branch"][l].astype(BF16),
        w_gate=w["w_gate"][l].astype(BF16),
        b_gate=w["b_gate"][l].reshape(1, N_BRANCH * D_MODEL),
        w_out=w["w_out"][l].astype(BF16),
        peer_wqt=w["peer_wq"][l].T.astype(BF16),
        peer_subkeys=w["peer_subkeys"][l].astype(BF16),
        layer=l, peer_u=w["peer_u_bf16"], peer_v=w["peer_v_bf16"],
    )


def kernel(x_prompt, x_sample, c, cache_nat_k, cache_nat_v, cache_gqa_k, cache_gqa_v, cache_diff_k, cache_diff_v, c_ctx, w_mod, b_mod, norm1_g, norm2_g, w_in, nat_qk_g, nat_rpb, gqa_qk_g, diff_qk_g, diff_lambda, diff_sub_g, sgu_norm_g, sgu_w, sgu_b, w_branch, w_gate, b_gate, w_out, peer_wq, peer_subkeys, peer_u, peer_v):
    w = dict(norm1_g=norm1_g, norm2_g=norm2_g, w_in=w_in, nat_qk_g=nat_qk_g, nat_rpb=nat_rpb,
             gqa_qk_g=gqa_qk_g, diff_qk_g=diff_qk_g, diff_lambda=diff_lambda, diff_sub_g=diff_sub_g,
             sgu_norm_g=sgu_norm_g, sgu_w=sgu_w, sgu_b=sgu_b, w_branch=w_branch, w_gate=w_gate,
             b_gate=b_gate, w_out=w_out, peer_wq=peer_wq, peer_subkeys=peer_subkeys,
             peer_u_bf16=peer_u.astype(BF16), peer_v_bf16=peer_v.astype(BF16), rope=_rope_tables())
    cvec = jnp.concatenate([c_ctx[None], c, jnp.zeros((MOD_ROWS - 1 - DEC_BATCH, D_MODEL), F32)], axis=0)
    mod = _modulation(cvec, w_mod, b_mod).reshape(DEPTH, MOD_ROWS, 1, 6 * D_MODEL)
    xp = x_prompt.reshape(N_CTX, D_MODEL)
    xs = x_sample.reshape(N_LAT, D_MODEL)
    new = []
    for l in range(DEPTH):
        lp = _layer_params(l, w)
        lam_init = 0.8 - 0.6 * math.exp(-0.3 * l)
        caches = (cache_nat_k, cache_nat_v, cache_gqa_k, cache_gqa_v, cache_diff_k, cache_diff_v)
        proj = _in_projection(xp, mod[l], lp, latent=False)
        new.append(proj)
        attn = _ctx_attention(proj, lp, lam_init)
        xp = _merge(xp, mod[l], lp, [(attn, 0), (attn, 1), (attn, 2), (proj, _COL["sgu"])], latent=False)
        xp = _peer(xp, mod[l], lp, latent=False)
        proj = _in_projection(xs, mod[l], lp, latent=True)
        o_nat, o_gqa, o_dif = _lat_attention(proj, caches, lp, l, lam_init)
        xs = _merge(xs, mod[l], lp, [(o_nat, 0), (o_gqa, 0), (o_dif, 0), (proj, _COL["sgu"])], latent=True)
        xs = _peer(xs, mod[l], lp, latent=True)

    def cache_out(lo, width, tail):
        a = jnp.stack([p[:, lo:lo + width].reshape(BATCH, SEQ, width) for p in new], axis=1)
        return a.reshape((BATCH, DEPTH, SEQ) + tail)

    return (xp.reshape(BATCH, SEQ, D_MODEL), xs.reshape(DEC_BATCH, DEC_SEQ, D_MODEL),
            cache_out(256, 256, (NAT_HEADS, HEAD_DIM)), cache_out(512, 256, (NAT_HEADS, HEAD_DIM)),
            cache_out(1024, 128, (GQA_KV_HEADS, HEAD_DIM)), cache_out(1152, 128, (GQA_KV_HEADS, HEAD_DIM)),
            cache_out(1536, 256, (DIFF_HEADS, 2, DIFF_DIM)), cache_out(1792, 256, (DIFF_HEADS, HEAD_DIM)))
```

```python
import functools
import math

import numpy as np
import jax
import jax.numpy as jnp
from jax import lax
from jax.experimental import pallas as pl
from jax.experimental.pallas import tpu as pltpu

F32 = jnp.float32
BF16 = jnp.bfloat16

D_MODEL = 1024
BATCH = 16
SEQ = 256
DEPTH = 2
DEC_BATCH = 2
DEC_SEQ = 4096
PAST_LEN = 512
GRID_W = 64
HEAD_DIM = 64
NAT_HEADS = 4
NAT_ROWS = 8
NAT_COLS = 16
GQA_HEADS = 4
GQA_KV_HEADS = 2
DIFF_HEADS = 4
DIFF_DIM = 32
SGU_WIDTH = 256
SGU_GROUPS = 4
CHUNK = 128
N_BRANCH = 4
BRANCH_W = 256
IN_W = 2560
PEER_HEADS = 8
PEER_KEYS = 128
PEER_QDIM = 256
PEER_TOPK = 16
ROPE_BASE = 10000.0
EPS = 1e-6

N_CTX = BATCH * SEQ
N_LAT = DEC_BATCH * DEC_SEQ
MOD_ROWS = 8
TB = 512
TQ = 256
PROJ_W = 2304
NAT_QROWS = 4
NAT_KROWS = NAT_QROWS + NAT_ROWS
TBS = 512
TBP = 1024
PEER_ET = 1024
PEER_ROWS = PEER_ET // PEER_KEYS
SUBLANES = 8
LANES = 128
GELU_C0 = 0.7978845608028654
GELU_C1 = GELU_C0 * 0.044715
BF16_ROWS = 16
VMEM_LIMIT = 56 * 1024 * 1024
NEG = -1e30


def _cparams(*sem):
    return pltpu.CompilerParams(dimension_semantics=sem, vmem_limit_bytes=VMEM_LIMIT)


def _const_spec(shape):
    return pl.BlockSpec(shape, lambda *_: (0,) * len(shape))


def _gelu(x):
    return 0.5 * x * (1.0 + jnp.tanh(x * (GELU_C0 + GELU_C1 * (x * x))))


def _sigmoid(x):
    return 1.0 / (1.0 + jnp.exp(-x))


def _dot(a, b):
    return jnp.dot(a.astype(BF16), b.astype(BF16), preferred_element_type=F32)


def _dot_nt(a, b):
    return lax.dot_general(a.astype(BF16), b.astype(BF16), (((1,), (1,)), ((), ())),
                           preferred_element_type=F32)


def _group_mean(y2, ones_bd):
    hi = y2.astype(BF16)
    lo = (y2 - hi.astype(F32)).astype(BF16)
    return (jnp.dot(hi, ones_bd, preferred_element_type=F32)
            + jnp.dot(lo, ones_bd, preferred_element_type=F32))


def _block_ones(width, group):
    idx = np.arange(width) // group
    return jnp.asarray((idx[:, None] == idx[None, :]).astype(np.float32) / group, dtype=BF16)


def _rms_mod(x, gain, scale, shift):
    xn = x * lax.rsqrt(jnp.mean(x * x, axis=-1, keepdims=True) + EPS) * gain
    return xn * (1.0 + scale) + shift


def _mod_spec(latent):
    if latent:
        return pl.BlockSpec((1, 1, 6 * D_MODEL), lambda i, *_: (1 + i // (DEC_SEQ // TB), 0, 0))
    return pl.BlockSpec((1, 1, 6 * D_MODEL), lambda i, *_: (0, 0, 0))


def _mod_kernel(c_ref, w_ref, b_ref, o_ref):
    c = c_ref[...]
    s = c * _sigmoid(c)
    o_ref[0] = _dot(s, w_ref[0]) + b_ref[0]


def _modulation(cvec, w_mod, b_mod):
    tn = 1536
    return pl.pallas_call(
        _mod_kernel,
        grid=(DEPTH, 6 * D_MODEL // tn),
        in_specs=[pl.BlockSpec((MOD_ROWS, D_MODEL), lambda l, j: (0, 0)),
                  pl.BlockSpec((1, D_MODEL, tn), lambda l, j: (l, 0, j)),
                  pl.BlockSpec((1, 1, tn), lambda l, j: (l, 0, j))],
        out_specs=pl.BlockSpec((1, MOD_ROWS, tn), lambda l, j: (l, 0, j)),
        out_shape=jax.ShapeDtypeStruct((DEPTH, MOD_ROWS, 6 * D_MODEL), F32),
        compiler_params=_cparams("arbitrary", "arbitrary"),
        name="modulation",
    )(cvec, w_mod, b_mod.reshape(DEPTH, 1, 6 * D_MODEL))


def _rope_tables():
    t = np.arange(DEC_SEQ)
    pos = (t // GRID_W, t % GRID_W)
    out = []
    for d in (HEAD_DIM, DIFF_DIM):
        half, quarter = d // 2, d // 4
        inv = ROPE_BASE ** (-np.arange(quarter, dtype=np.float32) * 2.0 / half)
        lane = np.arange(LANES) % d
        part, j = lane // half, lane % half
        ang = np.stack([pos[0][:, None] * inv[None, :], pos[1][:, None] * inv[None, :]], axis=1)
        a = ang[:, part, j % quarter].astype(np.float32)
        cos, sin = np.cos(a), np.sin(a)
        out += [jnp.asarray(cos, dtype=F32),
                jnp.asarray(np.where(j < quarter, -sin, 0.0), dtype=F32),
                jnp.asarray(np.where(j >= quarter, sin, 0.0), dtype=F32)]
    return out


def _rope(y, c, sa, sb, quarter):
    w = y.shape[-1]
    rep = w // LANES
    if rep > 1:
        c, sa, sb = (jnp.concatenate([t] * rep, axis=-1) for t in (c, sa, sb))
    up = pltpu.roll(y, w - quarter, 1)
    dn = pltpu.roll(y, quarter, 1)
    return y * c + up * sa + dn * sb


def _inproj_kernel(*refs, latent):
    (x_ref, mod_ref, n1_ref, w_ref, g_ref, bd64_ref, bd32_ref, bd256_ref,
     sgn_ref, sgw_ref, sgb_ref) = refs[:11]
    o_ref = refs[-1]
    mod = mod_ref[0]
    h = _rms_mod(x_ref[...], n1_ref[...], mod[:, D_MODEL:2 * D_MODEL], mod[:, 0:D_MODEL])
    y = jnp.dot(h.astype(BF16), w_ref[...], preferred_element_type=F32)

    def qk_norm(lo, width, bd, gain_row):
        v = y[:, lo:lo + width]
        ms = _group_mean(v * v, bd)
        return v * lax.rsqrt(ms + EPS) * g_ref[gain_row:gain_row + 1, 0:width]

    bd64 = bd64_ref[...]
    bd32 = bd32_ref[...]
    if latent:
        c64, sa64, sb64, c32, sa32, sb32 = (r[...] for r in refs[11:17])
        rope64 = functools.partial(_rope, c=c64, sa=sa64, sb=sb64, quarter=HEAD_DIM // 4)
        rope32 = functools.partial(_rope, c=c32, sa=sa32, sb=sb32, quarter=DIFF_DIM // 4)
    else:
        rope64 = rope32 = lambda v: v
    o_ref[:, 0:256] = qk_norm(0, 256, bd64, 0)
    o_ref[:, 256:512] = qk_norm(256, 256, bd64, 1)
    o_ref[:, 512:768] = y[:, 512:768]
    o_ref[:, 768:1024] = rope64(qk_norm(768, 256, bd64, 2))
    o_ref[:, 1024:1152] = rope64(qk_norm(1024, 128, bd64[0:128, 0:128], 3))
    o_ref[:, 1152:1280] = y[:, 1152:1280]
    o_ref[:, 1280:1536] = rope32(qk_norm(1280, 256, bd32, 4))
    o_ref[:, 1536:1792] = rope32(qk_norm(1536, 256, bd32, 5))
    o_ref[:, 1792:2048] = y[:, 1792:2048]
    u = _gelu(y[:, 2048:2304])
    v = _gelu(y[:, 2304:2560])
    vn = v * lax.rsqrt(_group_mean(v * v, bd256_ref[...]) + EPS) * sgn_ref[...]
    vnb = vn.astype(BF16)
    lane_group = lax.broadcasted_iota(jnp.int32, (CHUNK, SGU_WIDTH), 1) // (SGU_WIDTH // SGU_GROUPS)
    for n in range(TB // CHUNK):
        vc = vnb[n * CHUNK:(n + 1) * CHUNK, :]
        s = sgb_ref[...]
        for g in range(SGU_GROUPS):
            sg = jnp.dot(sgw_ref[g], vc, preferred_element_type=F32)
            s = s + jnp.where(lane_group == g, sg, 0.0)
        o_ref[n * CHUNK:(n + 1) * CHUNK, 2048:2304] = u[n * CHUNK:(n + 1) * CHUNK, :] * s


def _in_projection(x, mod_l, lp, latent):
    n_tok = x.shape[0]
    in_specs = [pl.BlockSpec((TB, D_MODEL), lambda i: (i, 0)),
                _mod_spec(latent),
                _const_spec((1, D_MODEL)),
                _const_spec((D_MODEL, IN_W)),
                _const_spec((8, 256)),
                _const_spec((256, 256)), _const_spec((256, 256)), _const_spec((256, 256)),
                _const_spec((1, SGU_WIDTH)),
                _const_spec((SGU_GROUPS, CHUNK, CHUNK)),
                _const_spec((CHUNK, SGU_WIDTH))]
    args = [x, mod_l, lp["norm1_g"], lp["w_in"], lp["qk_gains"], lp["bd64"], lp["bd32"], lp["bd256"],
            lp["sgu_norm_g"], lp["sgu_w"], lp["sgu_b"]]
    if latent:
        in_specs += [pl.BlockSpec((TB, LANES), lambda i: (i % (DEC_SEQ // TB), 0))] * 6
        args += lp["rope"]
    return pl.pallas_call(
        functools.partial(_inproj_kernel, latent=latent),
        grid=(n_tok // TB,),
        in_specs=in_specs,
        out_specs=pl.BlockSpec((TB, PROJ_W), lambda i: (i, 0)),
        out_shape=jax.ShapeDtypeStruct((n_tok, PROJ_W), F32),
        compiler_params=_cparams("arbitrary"),
        name="in_projection_lat" if latent else "in_projection_ctx",
    )(*args)


LOG2E = 1.4426950408889634


def _softmax_parts(scores):
    m = functools.reduce(jnp.maximum, [jnp.max(s, axis=-1, keepdims=True) for s in scores])
    ps = [jnp.exp2(s - m) for s in scores]
    l = functools.reduce(jnp.add, [jnp.sum(p, axis=-1, keepdims=True) for p in ps])
    return [p.astype(BF16) for p in ps], 1.0 / l


def _diff_lambda(lam_ref, lam_init):
    lv = lam_ref[...]
    a = jnp.sum(lv[0:1] * lv[1:2], axis=-1, keepdims=True)
    b = jnp.sum(lv[2:3] * lv[3:4], axis=-1, keepdims=True)
    return jnp.exp(a) - jnp.exp(b) + lam_init


def _head(ref_or_val, h, width=HEAD_DIM):
    return ref_or_val[:, h * width:(h + 1) * width]


def _mha(q, ks, vs, n_heads, kv_group, scale, biases=None):
    ks, vs = [k.astype(BF16) for k in ks], [v.astype(BF16) for v in vs]
    qs = (q * (scale * LOG2E)).astype(BF16)

    def scores_of(h):
        scores = [_dot_nt(_head(qs, h), _head(k, h // kv_group)) for k in ks]
        if biases is not None:
            scores = [s if b is None else s + b[h] * LOG2E for s, b in zip(scores, biases)]
        return scores

    def finish(h, ps, rl):
        return functools.reduce(jnp.add, [_dot(p, _head(v, h // kv_group)) for p, v in zip(ps, vs)]) * rl

    return jnp.concatenate(_softmax_pipeline(n_heads, scores_of, finish), axis=-1)


def _softmax_pipeline(n, scores_of, finish):
    outs = []
    scores = scores_of(0)
    for i in range(n):
        ps, rl = _softmax_parts(scores)
        if i + 1 < n:
            scores = scores_of(i + 1)
        outs.append(finish(i, ps, rl))
    return outs


def _diff_attn(q, ks, vs, lam, sub_gain, bd64, lam_init):
    qs = (q * (DIFF_DIM ** -0.5 * LOG2E)).astype(BF16)
    ks, vs = [k.astype(BF16) for k in ks], [v.astype(BF16) for v in vs]
    scores_of = lambda i: [_dot_nt(_head(qs, i, DIFF_DIM), _head(k, i, DIFF_DIM)) for k in ks]
    finish = lambda i, ps, rl: functools.reduce(jnp.add, [_dot(p, _head(v, i // 2)) for p, v in zip(ps, vs)]) * rl
    pv = _softmax_pipeline(2 * DIFF_HEADS, scores_of, finish)
    o = jnp.concatenate([pv[2 * h] - lam * pv[2 * h + 1] for h in range(DIFF_HEADS)], axis=-1)
    ms = _group_mean(o * o, bd64)
    return o * lax.rsqrt(ms + EPS) * sub_gain * (1.0 - lam_init)


def _ctx_attn_kernel(nq, nk, nv, gq, gk, gv, dq, dk, dv, lam_ref, subg_ref, bd64_ref, o_ref, *, lam_init):
    scale = HEAD_DIM ** -0.5
    o_ref[:, 0:256] = _mha(nq[...], [nk[...]], [nv[...]], NAT_HEADS, 1, scale)
    o_ref[:, 256:512] = _mha(gq[...], [gk[...]], [gv[...]], GQA_HEADS, GQA_HEADS // GQA_KV_HEADS, scale)
    lam = _diff_lambda(lam_ref, lam_init)
    o_ref[:, 512:768] = _diff_attn(dq[...], [dk[...]], [dv[...]], lam, subg_ref[...], bd64_ref[...], lam_init)


_COL = dict(nq=0, nk=1, nv=2, gq=3, gk=8, gv=9, dq=5, dk=6, dv=7, sgu=8)


def _ctx_attention(proj, lp, lam_init):
    blk = lambda name, w: pl.BlockSpec((SEQ, w), lambda b: (b, _COL[name]))
    return pl.pallas_call(
        functools.partial(_ctx_attn_kernel, lam_init=lam_init),
        grid=(BATCH,),
        in_specs=[blk("nq", 256), blk("nk", 256), blk("nv", 256),
                  blk("gq", 256), blk("gk", 128), blk("gv", 128),
                  blk("dq", 256), blk("dk", 256), blk("dv", 256),
                  _const_spec((4, DIFF_DIM)), _const_spec((1, 256)), _const_spec((256, 256))],
        out_specs=pl.BlockSpec((SEQ, 768), lambda b: (b, 0)),
        out_shape=jax.ShapeDtypeStruct((N_CTX, 768), F32),
        compiler_params=_cparams("arbitrary"),
        name="ctx_attention",
    )(*([proj] * 9), lp["diff_lambda"], lp["diff_sub_g"], lp["bd64"])


def _gqa_lat_kernel(q_ref, k_ref, v_ref, ck_ref, cv_ref, o_ref):
    o_ref[...] = _mha(q_ref[...], [k_ref[...], ck_ref[...]], [v_ref[...], cv_ref[...]],
                      GQA_HEADS, GQA_HEADS // GQA_KV_HEADS, HEAD_DIM ** -0.5)


def _diff_lat_kernel(q_ref, k_ref, v_ref, ck_ref, cv_ref, lam_ref, subg_ref, bd64_ref, o_ref, *, lam_init):
    lam = _diff_lambda(lam_ref, lam_init)
    o_ref[...] = _diff_attn(q_ref[...], [k_ref[...], ck_ref[...]], [v_ref[...], cv_ref[...]],
                            lam, subg_ref[...], bd64_ref[...], lam_init)


def _nat_lat_kernel(q_ref, k_ref, v_ref, ck_ref, cv_ref, bias_ref, o_ref):
    i = pl.program_id(1)
    k_row0 = jnp.clip(NAT_QROWS * i - NAT_ROWS // 2, 0, GRID_W - NAT_KROWS)
    start = pl.multiple_of(k_row0 * GRID_W, GRID_W)
    kw = k_ref[pl.ds(start, NAT_KROWS * GRID_W), :]
    vw = v_ref[pl.ds(start, NAT_KROWS * GRID_W), :]
    o_ref[...] = _mha(q_ref[...], [kw, ck_ref[...]], [vw, cv_ref[...]], NAT_HEADS, 1, HEAD_DIM ** -0.5,
                      biases=[bias_ref[0], None])


def _nat_bias_table(rpb):
    rows = DEC_SEQ // GRID_W
    nblk = rows // NAT_QROWS
    pad = jnp.pad(rpb, ((0, 0), (0, 0), (GRID_W - NAT_COLS, GRID_W - NAT_COLS)))
    toep = jnp.stack([pad[:, :, GRID_W - 1 - c:2 * GRID_W - 1 - c] for c in range(GRID_W)], axis=2)
    col = np.arange(GRID_W)
    cs = np.clip(col - NAT_COLS // 2, 0, GRID_W - NAT_COLS)
    col_ok = (col[None, :] >= cs[:, None]) & (col[None, :] < cs[:, None] + NAT_COLS)
    toep = jnp.where(col_ok, toep, NEG)
    masked = jnp.full((NAT_HEADS, GRID_W, GRID_W), NEG, F32)
    cases = []
    for blk in (0, 1, nblk - 1):
        r0 = blk * NAT_QROWS
        k0 = int(np.clip(r0 - NAT_ROWS // 2, 0, rows - NAT_KROWS))
        q_rows = []
        for qr in range(r0, r0 + NAT_QROWS):
            rs = int(np.clip(qr - NAT_ROWS // 2, 0, rows - NAT_ROWS))
            q_rows.append(jnp.concatenate(
                [toep[:, kr - qr + NAT_ROWS - 1] if rs <= kr < rs + NAT_ROWS else masked
                 for kr in range(k0, k0 + NAT_KROWS)], axis=-1))
        cases.append(jnp.concatenate(q_rows, axis=-2))
    return jnp.stack(cases, axis=0)


def _lat_attention(proj, caches, lp, l, lam_init):
    cnk, cnv, cgk, cgv, cdk, cdv = caches
    nq_blocks = DEC_SEQ // TQ
    qspec = lambda name: pl.BlockSpec((TQ, 256), lambda b, i: (b * nq_blocks + i, _COL[name]))
    kvspec = lambda name, w: pl.BlockSpec((DEC_SEQ, w), lambda b, i: (b, _COL[name]))
    cspec = lambda w: pl.BlockSpec((None, None, PAST_LEN, w), lambda b, i: (b, l, 0, 0))
    ospec = pl.BlockSpec((TQ, 256), lambda b, i: (b * nq_blocks + i, 0))
    oshape = jax.ShapeDtypeStruct((N_LAT, 256), F32)
    o_gqa = pl.pallas_call(
        _gqa_lat_kernel,
        grid=(DEC_BATCH, nq_blocks),
        in_specs=[qspec("gq"), kvspec("gk", 128), kvspec("gv", 128), cspec(128), cspec(128)],
        out_specs=ospec, out_shape=oshape,
        compiler_params=_cparams("arbitrary", "arbitrary"),
        name="gqa_lat_attention",
    )(proj, proj, proj, cgk.reshape(DEC_BATCH, DEPTH, PAST_LEN, 128), cgv.reshape(DEC_BATCH, DEPTH, PAST_LEN, 128))
    o_dif = pl.pallas_call(
        functools.partial(_diff_lat_kernel, lam_init=lam_init),
        grid=(DEC_BATCH, nq_blocks),
        in_specs=[qspec("dq"), kvspec("dk", 256), kvspec("dv", 256), cspec(256), cspec(256),
                  _const_spec((4, DIFF_DIM)), _const_spec((1, 256)), _const_spec((256, 256))],
        out_specs=ospec, out_shape=oshape,
        compiler_params=_cparams("arbitrary", "arbitrary"),
        name="diff_lat_attention",
    )(proj, proj, proj, cdk.reshape(DEC_BATCH, DEPTH, PAST_LEN, 256), cdv.reshape(DEC_BATCH, DEPTH, PAST_LEN, 256),
      lp["diff_lambda"], lp["diff_sub_g"], lp["bd64"])
    nblk = DEC_SEQ // (NAT_QROWS * GRID_W)
    nat_q = NAT_QROWS * GRID_W
    o_nat = pl.pallas_call(
        _nat_lat_kernel,
        grid=(DEC_BATCH, nblk),
        in_specs=[pl.BlockSpec((nat_q, 256), lambda b, i: (b * nblk + i, _COL["nq"])),
                  kvspec("nk", 256), kvspec("nv", 256), cspec(256), cspec(256),
                  pl.BlockSpec((1, NAT_HEADS, nat_q, NAT_KROWS * GRID_W),
                               lambda b, i: (jnp.where(i == 0, 0, jnp.where(i == nblk - 1, 2, 1)), 0, 0, 0))],
        out_specs=pl.BlockSpec((nat_q, 256), lambda b, i: (b * nblk + i, 0)),
        out_shape=oshape,
        compiler_params=_cparams("arbitrary", "arbitrary"),
        name="nat_lat_attention",
    )(proj, proj, proj, cnk.reshape(DEC_BATCH, DEPTH, PAST_LEN, 256), cnv.reshape(DEC_BATCH, DEPTH, PAST_LEN, 256),
      lp["nat_bias"])
    return o_nat, o_gqa, o_dif


def _merge_kernel(x_ref, mod_ref, n1_ref, b0_ref, b1_ref, b2_ref, b3_ref, wb_ref, wg_ref, bg_ref, wo_ref, o_ref):
    x = x_ref[...]
    mod = mod_ref[0]
    hb = _rms_mod(x, n1_ref[...], mod[:, D_MODEL:2 * D_MODEL], mod[:, 0:D_MODEL]).astype(BF16)
    merged = None
    for n, b_ref in enumerate((b0_ref, b1_ref, b2_ref, b3_ref)):
        cols = slice(n * D_MODEL, (n + 1) * D_MODEL)
        gate = _sigmoid(jnp.dot(hb, wg_ref[:, cols], preferred_element_type=F32) + bg_ref[:, cols])
        term = gate * _dot(b_ref[...], wb_ref[n])
        merged = term if merged is None else merged + term
    out = _dot(merged, wo_ref[...])
    o_ref[...] = x + mod[:, 2 * D_MODEL:3 * D_MODEL] * out


def _merge(x, mod_l, lp, branches, latent):
    n_tok = x.shape[0]
    return pl.pallas_call(
        _merge_kernel,
        grid=(n_tok // TB,),
        in_specs=[pl.BlockSpec((TB, D_MODEL), lambda i: (i, 0)), _mod_spec(latent), _const_spec((1, D_MODEL))]
                 + [pl.BlockSpec((TB, BRANCH_W), functools.partial(lambda i, c: (i, c), c=col)) for _, col in branches]
                 + [_const_spec((N_BRANCH, BRANCH_W, D_MODEL)), _const_spec((D_MODEL, N_BRANCH * D_MODEL)),
                    _const_spec((1, N_BRANCH * D_MODEL)), _const_spec((D_MODEL, D_MODEL))],
        out_specs=pl.BlockSpec((TB, D_MODEL), lambda i: (i, 0)),
        out_shape=jax.ShapeDtypeStruct((n_tok, D_MODEL), F32),
        compiler_params=_cparams("arbitrary"),
        name="merge_lat" if latent else "merge_ctx",
    )(x, mod_l, lp["norm1_g"], *[a for a, _ in branches], lp["w_branch"], lp["w_gate"], lp["b_gate"], lp["w_out"])


def _oddeven_merge_pairs(n):
    pairs = []
    p = 1
    while p < n:
        k = p
        while k >= 1:
            for j in range(k % p, n - k, 2 * k):
                for i in range(min(k, n - j - k)):
                    if (i + j) // (2 * p) == (i + j + k) // (2 * p):
                        pairs.append((i + j, i + j + k))
            k //= 2
        p *= 2
    return pairs


def _top_desc_sorted(s, k):
    n = s.shape[0] // SUBLANES
    tiles = [s[j * SUBLANES:(j + 1) * SUBLANES, :] for j in range(n)]
    for a, b in _oddeven_merge_pairs(n):
        tiles[a], tiles[b] = jnp.maximum(tiles[a], tiles[b]), jnp.minimum(tiles[a], tiles[b])
    rows = []
    for i in range(k):
        m = jnp.max(tiles[0], axis=0, keepdims=True)
        rows.append(m)
        hit = tiles[0] == m
        for j in range(min(n, k - 1 - i)):
            tiles[j] = jnp.where(hit, tiles[j + 1] if j + 1 < n else NEG, tiles[j])
    return jnp.concatenate(rows, axis=0)


def _count_above(t, x, strict):
    assert PEER_TOPK == 16, "the search tree below is written out for 16 sorted rows"
    above = (lambda p: p > x) if strict else (lambda p: p >= x)
    row = lambda b: t[b:b + 1]
    b8 = above(row(7))
    b4 = above(jnp.where(b8, row(11), row(3)))
    b2 = above(jnp.where(b8, jnp.where(b4, row(13), row(9)), jnp.where(b4, row(5), row(1))))
    hi = jnp.where(b4, jnp.where(b2, row(14), row(12)), jnp.where(b2, row(10), row(8)))
    lo = jnp.where(b4, jnp.where(b2, row(6), row(4)), jnp.where(b2, row(2), row(0)))
    b1 = above(jnp.where(b8, hi, lo))
    count = jnp.where(b8, 8.0, 0.0) + jnp.where(b4, 4.0, 0.0) + jnp.where(b2, 2.0, 0.0) + jnp.where(b1, 1.0, 0.0)
    return count + jnp.where(above(row(15)), 1.0, 0.0)


def _peer_score_kernel(x_ref, mod_ref, n2_ref, wqt_ref, sk_ref, ht_ref, e1_ref, cnt_ref, rk_ref, e2_ref):
    mod = mod_ref[0]
    h2 = _rms_mod(x_ref[...], n2_ref[...], mod[:, 4 * D_MODEL:5 * D_MODEL], mod[:, 3 * D_MODEL:4 * D_MODEL])
    htb = h2.T.astype(BF16)
    ht_ref[...] = htb
    qt = jnp.dot(wqt_ref[...], htb, preferred_element_type=F32)
    k1 = PEER_TOPK + 1
    half = PEER_TOPK // 2
    row = lax.broadcasted_iota(jnp.int32, (half, 1), 0)
    neg_tile = jnp.full((half, qt.shape[-1]), NEG, F32)
    for h in range(PEER_HEADS):
        s1, s2 = (jnp.dot(sk_ref[p], qt[(2 * h + p) * PEER_KEYS:(2 * h + p + 1) * PEER_KEYS, :].astype(BF16),
                          preferred_element_type=F32) for p in range(2))
        t1 = _top_desc_sorted(s1, k1)
        t2 = _top_desc_sorted(s2, k1)
        rk2 = _count_above(t2, s2, strict=True)
        cands = [t1[0:1] + t2[0:half], t1[0:1] + t2[half:PEER_TOPK],
                 jnp.where(row >= 1, t1[0:half] + t2[0:1], NEG), t1[half:PEER_TOPK] + t2[0:1],
                 jnp.where(row == 0, t1[0:1] + t2[PEER_TOPK:k1],
                           jnp.where(row == 1, t1[PEER_TOPK:k1] + t2[0:1], NEG))]
        for a in range(1, half):
            nb = k1 // (a + 1)
            cands.append(jnp.where((row >= 1) & (row < nb), t1[a:a + 1] + t2[0:half], NEG))
        cands += [neg_tile] * (PEER_TOPK - len(cands))
        best = _top_desc_sorted(jnp.concatenate(cands, axis=0), k1)
        z = jnp.sum(jnp.exp(best[0:PEER_TOPK] - best[0:1]), axis=0, keepdims=True)
        thr = 0.5 * (best[PEER_TOPK - 1:PEER_TOPK] + best[PEER_TOPK:k1])
        e1_ref[h] = jnp.exp(s1 - t1[0:1]) * (0.5 / z)
        cnt_ref[h] = _count_above(t2, thr - s1, strict=False)
        rk_ref[h] = pltpu.bitcast(rk2.astype(BF16), jnp.int32)
        e2_ref[h] = pltpu.bitcast(jnp.exp(s2 - t2[0:1]).astype(BF16), jnp.int32)


def _peer_dense_kernel(ht_ref, e1_ref, cnt_ref, rk_ref, e2_ref, u_ref, v_ref, vlast_ref, x_ref, mod_ref, o_ref,
                       acc_ref, a_ref, g_ref, w_ref, wt_ref):
    e = pl.program_id(1)
    n_tok = acc_ref.shape[0]

    @pl.when(e == 0)
    def _():
        acc_ref[...] = jnp.zeros_like(acc_ref)
        w_ref[...] = jnp.zeros(w_ref.shape, jnp.int32)

    wt_ref[...] = pltpu.bitcast(pltpu.bitcast(w_ref[...], BF16).T, jnp.int32)

    zero = jnp.zeros((), BF16)
    for rr in range(PEER_ROWS):
        r = (e % (SUBLANES // PEER_ROWS)) * PEER_ROWS + rr
        packed_rows = slice(rr * (PEER_KEYS // 2), (rr + 1) * (PEER_KEYS // 2))
        def tile_row(ref, h):
            t16 = jnp.broadcast_to(ref[h, pl.ds(r, 1), :], (BF16_ROWS, n_tok)).astype(BF16)
            return jnp.concatenate([t16] * (PEER_KEYS // BF16_ROWS), axis=0)
        cnt_rows = [tile_row(cnt_ref, h) for h in range(PEER_HEADS)]
        e1_rows = [tile_row(e1_ref, h) for h in range(PEER_HEADS)]
        for lt in range(n_tok // LANES):
            lanes = slice(lt * LANES, (lt + 1) * LANES)
            g = None
            for h in range(PEER_HEADS):
                sel = pltpu.bitcast(rk_ref[h, :, lanes], BF16) < cnt_rows[h][:, lanes]
                term = jnp.where(sel, pltpu.bitcast(e2_ref[h, :, lanes], BF16) * e1_rows[h][:, lanes], zero)
                g = term if g is None else g + term
            g_ref[packed_rows, lanes] = pltpu.bitcast(g, jnp.int32)
    a_ref[...] = jnp.dot(u_ref[...], ht_ref[...], preferred_element_type=F32)
    for rr in range(PEER_ROWS):
        rows = slice(rr * PEER_KEYS, (rr + 1) * PEER_KEYS)
        packed_rows = slice(rr * (PEER_KEYS // 2), (rr + 1) * (PEER_KEYS // 2))
        for lt in range(n_tok // LANES):
            lanes = slice(lt * LANES, (lt + 1) * LANES)
            a = a_ref[rows, lanes]
            t = jnp.tanh(a * (GELU_C0 + GELU_C1 * (a * a)))
            w_ref[packed_rows, lanes] = pltpu.bitcast(
                pltpu.bitcast(g_ref[packed_rows, lanes], BF16) * (a + a * t).astype(BF16), jnp.int32)
    acc_ref[...] += jnp.dot(pltpu.bitcast(wt_ref[...], BF16), v_ref[...], preferred_element_type=F32)

    @pl.when(e == pl.num_programs(1) - 1)
    def _():
        acc = acc_ref[...] + jnp.dot(pltpu.bitcast(w_ref[...], BF16).T, vlast_ref[...], preferred_element_type=F32)
        o_ref[...] = x_ref[...] + mod_ref[0][:, 5 * D_MODEL:6 * D_MODEL] * acc


def _peer(x, mod_l, lp, latent):
    n_tok = x.shape[0]

    def mod_spec(tb):
        if latent:
            return pl.BlockSpec((1, 1, 6 * D_MODEL), lambda i, *_: (1 + i // (DEC_SEQ // tb), 0, 0))
        return pl.BlockSpec((1, 1, 6 * D_MODEL), lambda i, *_: (0, 0, 0))

    n_keys2 = PEER_HEADS * PEER_QDIM
    f32_tab = lambda n: (PEER_HEADS, PEER_KEYS, n)
    bf16_tab = lambda n: (PEER_HEADS, PEER_KEYS // 2, n)
    ht, *tabs = pl.pallas_call(
        _peer_score_kernel,
        grid=(n_tok // TBS,),
        in_specs=[pl.BlockSpec((TBS, D_MODEL), lambda i: (i, 0)), mod_spec(TBS), _const_spec((1, D_MODEL)),
                  _const_spec((n_keys2, D_MODEL)), _const_spec((2, PEER_KEYS, PEER_QDIM // 2))],
        out_specs=[pl.BlockSpec((D_MODEL, TBS), lambda i: (0, i))]
                  + [pl.BlockSpec(f32_tab(TBS), lambda i: (0, 0, i))] * 2
                  + [pl.BlockSpec(bf16_tab(TBS), lambda i: (0, 0, i))] * 2,
        out_shape=[jax.ShapeDtypeStruct((D_MODEL, n_tok), BF16),
                   jax.ShapeDtypeStruct(f32_tab(n_tok), F32), jax.ShapeDtypeStruct(f32_tab(n_tok), F32),
                   jax.ShapeDtypeStruct(bf16_tab(n_tok), jnp.int32), jax.ShapeDtypeStruct(bf16_tab(n_tok), jnp.int32)],
        compiler_params=_cparams("arbitrary"),
        name="peer_scores_lat" if latent else "peer_scores_ctx",
    )(x, mod_l, lp["norm2_g"], lp["peer_wqt"], lp["peer_subkeys"])
    nb = n_tok // TBP
    n_et = PEER_KEYS * PEER_KEYS // PEER_ET
    row_spec = pl.BlockSpec((PEER_HEADS, SUBLANES, TBP), lambda i, e: (0, e // (SUBLANES // PEER_ROWS), i))
    layer = lp["layer"]
    tile_spec = lambda tile_of, **kw: pl.BlockSpec((None, PEER_ET, D_MODEL), lambda i, e: (layer, tile_of(e), 0), **kw)
    once = dict(pipeline_mode=pl.Buffered(1))
    return pl.pallas_call(
        _peer_dense_kernel,
        grid=(nb, n_et),
        in_specs=[pl.BlockSpec((D_MODEL, TBP), lambda i, e: (0, i), **once), row_spec, row_spec]
                 + [pl.BlockSpec(bf16_tab(TBP), lambda i, e: (0, 0, i), **once)] * 2
                 + [tile_spec(lambda e: e),
                    tile_spec(lambda e: jnp.maximum(e - 1, 0)),
                    tile_spec(lambda e: jnp.where(e == n_et - 1, n_et - 1, 0), **once),
                    pl.BlockSpec((TBP, D_MODEL), lambda i, e: (i, 0), **once),
                    mod_spec(TBP)],
        out_specs=pl.BlockSpec((TBP, D_MODEL), lambda i, e: (i, 0)),
        out_shape=jax.ShapeDtypeStruct((n_tok, D_MODEL), F32),
        scratch_shapes=[pltpu.VMEM((TBP, D_MODEL), F32), pltpu.VMEM((PEER_ET, TBP), F32),
                        pltpu.VMEM((PEER_ET // 2, TBP), jnp.int32), pltpu.VMEM((PEER_ET // 2, TBP), jnp.int32),
                        pltpu.VMEM((TBP // 2, PEER_ET), jnp.int32)],
        compiler_params=_cparams("arbitrary", "arbitrary"),
        name="peer_dense_lat" if latent else "peer_dense_ctx",
    )(ht, *tabs, lp["peer_u"], lp["peer_v"], lp["peer_v"], x, mod_l)


def _layer_params(l, w):
    tile = lambda g, n: jnp.tile(g, n)
    gains = jnp.stack([
        tile(w["nat_qk_g"][l, 0], 4), tile(w["nat_qk_g"][l, 1], 4),
        tile(w["gqa_qk_g"][l, 0], 4), tile(w["gqa_qk_g"][l, 1], 4),
        tile(w["diff_qk_g"][l, 0], 8), tile(w["diff_qk_g"][l, 1], 8),
        jnp.zeros((256,), F32), jnp.zeros((256,), F32)])
    return dict(
        norm1_g=w["norm1_g"][l].reshape(1, D_MODEL),
        norm2_g=w["norm2_g"][l].reshape(1, D_MODEL),
        w_in=w["w_in"][l].astype(BF16),
        qk_gains=gains,
        bd64=_block_ones(256, 64), bd32=_block_ones(256, 32), bd256=_block_ones(256, 256),
        rope=w["rope"],
        sgu_norm_g=w["sgu_norm_g"][l].reshape(1, SGU_WIDTH),
        sgu_w=w["sgu_w"][l].astype(BF16),
        sgu_b=jnp.repeat(w["sgu_b"][l].T, SGU_WIDTH // SGU_GROUPS, axis=1),
        diff_lambda=w["diff_lambda"][l],
        diff_sub_g=tile(w["diff_sub_g"][l], 4).reshape(1, 256),
        nat_bias=_nat_bias_table(w["nat_rpb"][l]),
        w_branch=w["w_branch"][l].astype(BF16),
        w_gate=w["w_gate"][l].astype(BF16),
        b_gate=w["b_gate"][l].reshape(1, N_BRANCH * D_MODEL),
        w_out=w["w_out"][l].astype(BF16),
        peer_wqt=w["peer_wq"][l].T.astype(BF16),
        peer_subkeys=w["peer_subkeys"][l].astype(BF16),
        layer=l, peer_u=w["peer_u_bf16"], peer_v=w["peer_v_bf16"],
    )


def kernel(x_prompt, x_sample, c, cache_nat_k, cache_nat_v, cache_gqa_k, cache_gqa_v, cache_diff_k, cache_diff_v, c_ctx, w_mod, b_mod, norm1_g, norm2_g, w_in, nat_qk_g, nat_rpb, gqa_qk_g, diff_qk_g, diff_lambda, diff_sub_g, sgu_norm_g, sgu_w, sgu_b, w_branch, w_gate, b_gate, w_out, peer_wq, peer_subkeys, peer_u, peer_v):
    w = dict(norm1_g=norm1_g, norm2_g=norm2_g, w_in=w_in, nat_qk_g=nat_qk_g, nat_rpb=nat_rpb,
             gqa_qk_g=gqa_qk_g, diff_qk_g=diff_qk_g, diff_lambda=diff_lambda, diff_sub_g=diff_sub_g,
             sgu_norm_g=sgu_norm_g, sgu_w=sgu_w, sgu_b=sgu_b, w_branch=w_branch, w_gate=w_gate,
             b_gate=b_gate, w_out=w_out, peer_wq=peer_wq, peer_subkeys=peer_subkeys,
             peer_u_bf16=peer_u.astype(BF16), peer_v_bf16=peer_v.astype(BF16), rope=_rope_tables())
    cvec = jnp.concatenate([c_ctx[None], c, jnp.zeros((MOD_ROWS - 1 - DEC_BATCH, D_MODEL), F32)], axis=0)
    mod = _modulation(cvec, w_mod, b_mod).reshape(DEPTH, MOD_ROWS, 1, 6 * D_MODEL)
    xp = x_prompt.reshape(N_CTX, D_MODEL)
    xs = x_sample.reshape(N_LAT, D_MODEL)
    new = []
    for l in range(DEPTH):
        lp = _layer_params(l, w)
        lam_init = 0.8 - 0.6 * math.exp(-0.3 * l)
        caches = (cache_nat_k, cache_nat_v, cache_gqa_k, cache_gqa_v, cache_diff_k, cache_diff_v)
        proj = _in_projection(xp, mod[l], lp, latent=False)
        new.append(proj)
        attn = _ctx_attention(proj, lp, lam_init)
        xp = _merge(xp, mod[l], lp, [(attn, 0), (attn, 1), (attn, 2), (proj, _COL["sgu"])], latent=False)
        xp = _peer(xp, mod[l], lp, latent=False)
        proj = _in_projection(xs, mod[l], lp, latent=True)
        o_nat, o_gqa, o_dif = _lat_attention(proj, caches, lp, l, lam_init)
        xs = _merge(xs, mod[l], lp, [(o_nat, 0), (o_gqa, 0), (o_dif, 0), (proj, _COL["sgu"])], latent=True)
        xs = _peer(xs, mod[l], lp, latent=True)

    def cache_out(lo, width, tail):
        a = jnp.stack([p[:, lo:lo + width].reshape(BATCH, SEQ, width) for p in new], axis=1)
        return a.reshape((BATCH, DEPTH, SEQ) + tail)

    return (xp.reshape(BATCH, SEQ, D_MODEL), xs.reshape(DEC_BATCH, DEC_SEQ, D_MODEL),
            cache_out(256, 256, (NAT_HEADS, HEAD_DIM)), cache_out(512, 256, (NAT_HEADS, HEAD_DIM)),
            cache_out(1024, 128, (GQA_KV_HEADS, HEAD_DIM)), cache_out(1152, 128, (GQA_KV_HEADS, HEAD_DIM)),
            cache_out(1536, 256, (DIFF_HEADS, 2, DIFF_DIM)), cache_out(1792, 256, (DIFF_HEADS, HEAD_DIM)))
```

```python
import functools
import math

import numpy as np
import jax
import jax.numpy as jnp
from jax import lax
from jax.experimental import pallas as pl
from jax.experimental.pallas import tpu as pltpu

F32 = jnp.float32
BF16 = jnp.bfloat16

D_MODEL = 1024
BATCH = 16
SEQ = 256
DEPTH = 2
DEC_BATCH = 2
DEC_SEQ = 4096
PAST_LEN = 512
GRID_W = 64
HEAD_DIM = 64
NAT_HEADS = 4
NAT_ROWS = 8
NAT_COLS = 16
GQA_HEADS = 4
GQA_KV_HEADS = 2
DIFF_HEADS = 4
DIFF_DIM = 32
SGU_WIDTH = 256
SGU_GROUPS = 4
CHUNK = 128
N_BRANCH = 4
BRANCH_W = 256
IN_W = 2560
PEER_HEADS = 8
PEER_KEYS = 128
PEER_QDIM = 256
PEER_TOPK = 16
ROPE_BASE = 10000.0
EPS = 1e-6

N_CTX = BATCH * SEQ
N_LAT = DEC_BATCH * DEC_SEQ
MOD_ROWS = 8
TB = 512
TQ = 256
PROJ_W = 2304
NAT_QROWS = 4
NAT_KROWS = NAT_QROWS + NAT_ROWS
TBS = 512
TBP = 1024
PEER_ET = 512
PEER_ROWS = PEER_ET // PEER_KEYS
SUBLANES = 8
LANES = 128
GELU_C0 = 0.7978845608028654
GELU_C1 = GELU_C0 * 0.044715
BF16_ROWS = 16
VMEM_LIMIT = 56 * 1024 * 1024
NEG = -1e30


def _cparams(*sem):
    return pltpu.CompilerParams(dimension_semantics=sem, vmem_limit_bytes=VMEM_LIMIT)


def _const_spec(shape):
    return pl.BlockSpec(shape, lambda *_: (0,) * len(shape))


def _layer_spec(layer, shape):
    return pl.BlockSpec((None,) + shape, lambda *_: (layer,) + (0,) * len(shape))


def _gelu(x):
    return 0.5 * x * (1.0 + jnp.tanh(x * (GELU_C0 + GELU_C1 * (x * x))))


def _sigmoid(x):
    return 1.0 / (1.0 + jnp.exp(-x))


def _dot(a, b):
    return jnp.dot(a.astype(BF16), b.astype(BF16), preferred_element_type=F32)


def _dot_nt(a, b):
    return lax.dot_general(a.astype(BF16), b.astype(BF16), (((1,), (1,)), ((), ())),
                           preferred_element_type=F32)


def _group_mean(y2, ones_bd):
    hi = y2.astype(BF16)
    lo = (y2 - hi.astype(F32)).astype(BF16)
    return (jnp.dot(hi, ones_bd, preferred_element_type=F32)
            + jnp.dot(lo, ones_bd, preferred_element_type=F32))


def _block_ones(width, group):
    idx = np.arange(width) // group
    return jnp.asarray((idx[:, None] == idx[None, :]).astype(np.float32) / group, dtype=BF16)


def _rms_mod(x, gain, scale, shift):
    xn = x * lax.rsqrt(jnp.mean(x * x, axis=-1, keepdims=True) + EPS) * gain
    return xn * (1.0 + scale) + shift


def _mod_spec(latent):
    if latent:
        return pl.BlockSpec((1, 1, 6 * D_MODEL), lambda i, *_: (1 + i // (DEC_SEQ // TB), 0, 0))
    return pl.BlockSpec((1, 1, 6 * D_MODEL), lambda i, *_: (0, 0, 0))


def _mod_kernel(c_ref, w_ref, b_ref, o_ref):
    c = c_ref[...]
    s = c * _sigmoid(c)
    o_ref[0] = _dot(s, w_ref[0]) + b_ref[0]


def _modulation(cvec, w_mod, b_mod):
    tn = 1536
    return pl.pallas_call(
        _mod_kernel,
        grid=(DEPTH, 6 * D_MODEL // tn),
        in_specs=[pl.BlockSpec((MOD_ROWS, D_MODEL), lambda l, j: (0, 0)),
                  pl.BlockSpec((1, D_MODEL, tn), lambda l, j: (l, 0, j)),
                  pl.BlockSpec((1, 1, tn), lambda l, j: (l, 0, j))],
        out_specs=pl.BlockSpec((1, MOD_ROWS, tn), lambda l, j: (l, 0, j)),
        out_shape=jax.ShapeDtypeStruct((DEPTH, MOD_ROWS, 6 * D_MODEL), F32),
        compiler_params=_cparams("arbitrary", "arbitrary"),
        name="modulation",
    )(cvec, w_mod, b_mod.reshape(DEPTH, 1, 6 * D_MODEL))


def _rope_tables():
    t = np.arange(DEC_SEQ)
    pos = (t // GRID_W, t % GRID_W)
    out = []
    for d in (HEAD_DIM, DIFF_DIM):
        half, quarter = d // 2, d // 4
        inv = ROPE_BASE ** (-np.arange(quarter, dtype=np.float32) * 2.0 / half)
        lane = np.arange(LANES) % d
        part, j = lane // half, lane % half
        ang = np.stack([pos[0][:, None] * inv[None, :], pos[1][:, None] * inv[None, :]], axis=1)
        a = ang[:, part, j % quarter].astype(np.float32)
        cos, sin = np.cos(a), np.sin(a)
        out += [jnp.asarray(cos, dtype=F32),
                jnp.asarray(np.where(j < quarter, -sin, 0.0), dtype=F32),
                jnp.asarray(np.where(j >= quarter, sin, 0.0), dtype=F32)]
    return out


def _rope(y, c, sa, sb, quarter):
    w = y.shape[-1]
    rep = w // LANES
    if rep > 1:
        c, sa, sb = (jnp.concatenate([t] * rep, axis=-1) for t in (c, sa, sb))
    up = pltpu.roll(y, w - quarter, 1)
    dn = pltpu.roll(y, quarter, 1)
    return y * c + up * sa + dn * sb


def _inproj_kernel(*refs, latent):
    (x_ref, mod_ref, n1_ref, w_ref, g_ref, bd64_ref, bd32_ref, bd256_ref,
     sgn_ref, sgw_ref, sgb_ref) = refs[:11]
    o_ref = refs[-1]
    mod = mod_ref[0]
    h = _rms_mod(x_ref[...], n1_ref[...], mod[:, D_MODEL:2 * D_MODEL], mod[:, 0:D_MODEL])
    y = jnp.dot(h.astype(BF16), w_ref[...], preferred_element_type=F32)

    def qk_norm(lo, width, bd, gain_row):
        v = y[:, lo:lo + width]
        ms = _group_mean(v * v, bd)
        return v * lax.rsqrt(ms + EPS) * g_ref[gain_row:gain_row + 1, 0:width]

    bd64 = bd64_ref[...]
    bd32 = bd32_ref[...]
    if latent:
        c64, sa64, sb64, c32, sa32, sb32 = (r[...] for r in refs[11:17])
        rope64 = functools.partial(_rope, c=c64, sa=sa64, sb=sb64, quarter=HEAD_DIM // 4)
        rope32 = functools.partial(_rope, c=c32, sa=sa32, sb=sb32, quarter=DIFF_DIM // 4)
    else:
        rope64 = rope32 = lambda v: v
    o_ref[:, 0:256] = qk_norm(0, 256, bd64, 0)
    o_ref[:, 256:512] = qk_norm(256, 256, bd64, 1)
    o_ref[:, 512:768] = y[:, 512:768]
    o_ref[:, 768:1024] = rope64(qk_norm(768, 256, bd64, 2))
    o_ref[:, 1024:1152] = rope64(qk_norm(1024, 128, bd64[0:128, 0:128], 3))
    o_ref[:, 1152:1280] = y[:, 1152:1280]
    o_ref[:, 1280:1536] = rope32(qk_norm(1280, 256, bd32, 4))
    o_ref[:, 1536:1792] = rope32(qk_norm(1536, 256, bd32, 5))
    o_ref[:, 1792:2048] = y[:, 1792:2048]
    u = _gelu(y[:, 2048:2304])
    v = _gelu(y[:, 2304:2560])
    vn = v * lax.rsqrt(_group_mean(v * v, bd256_ref[...]) + EPS) * sgn_ref[...]
    vnb = vn.astype(BF16)
    lane_group = lax.broadcasted_iota(jnp.int32, (CHUNK, SGU_WIDTH), 1) // (SGU_WIDTH // SGU_GROUPS)
    for n in range(TB // CHUNK):
        vc = vnb[n * CHUNK:(n + 1) * CHUNK, :]
        s = sgb_ref[...]
        for g in range(SGU_GROUPS):
            sg = jnp.dot(sgw_ref[g], vc, preferred_element_type=F32)
            s = s + jnp.where(lane_group == g, sg, 0.0)
        o_ref[n * CHUNK:(n + 1) * CHUNK, 2048:2304] = u[n * CHUNK:(n + 1) * CHUNK, :] * s


def _in_projection(x, mod_l, lp, latent):
    n_tok = x.shape[0]
    in_specs = [pl.BlockSpec((TB, D_MODEL), lambda i: (i, 0)),
                _mod_spec(latent),
                _const_spec((1, D_MODEL)),
                _layer_spec(lp["layer"], (D_MODEL, IN_W)),
                _const_spec((8, 256)),
                _const_spec((256, 256)), _const_spec((256, 256)), _const_spec((256, 256)),
                _const_spec((1, SGU_WIDTH)),
                _layer_spec(lp["layer"], (SGU_GROUPS, CHUNK, CHUNK)),
                _const_spec((CHUNK, SGU_WIDTH))]
    args = [x, mod_l, lp["norm1_g"], lp["w_in"], lp["qk_gains"], lp["bd64"], lp["bd32"], lp["bd256"],
            lp["sgu_norm_g"], lp["sgu_w"], lp["sgu_b"]]
    if latent:
        in_specs += [pl.BlockSpec((TB, LANES), lambda i: (i % (DEC_SEQ // TB), 0))] * 6
        args += lp["rope"]
    return pl.pallas_call(
        functools.partial(_inproj_kernel, latent=latent),
        grid=(n_tok // TB,),
        in_specs=in_specs,
        out_specs=pl.BlockSpec((TB, PROJ_W), lambda i: (i, 0)),
        out_shape=jax.ShapeDtypeStruct((n_tok, PROJ_W), F32),
        compiler_params=_cparams("arbitrary"),
        name="in_projection_lat" if latent else "in_projection_ctx",
    )(*args)


LOG2E = 1.4426950408889634


def _softmax_parts(scores):
    m = functools.reduce(jnp.maximum, [jnp.max(s, axis=-1, keepdims=True) for s in scores])
    ps = [jnp.exp2(s - m) for s in scores]
    l = functools.reduce(jnp.add, [jnp.sum(p, axis=-1, keepdims=True) for p in ps])
    return [p.astype(BF16) for p in ps], 1.0 / l


def _diff_lambda(lam_ref, lam_init):
    lv = lam_ref[...]
    a = jnp.sum(lv[0:1] * lv[1:2], axis=-1, keepdims=True)
    b = jnp.sum(lv[2:3] * lv[3:4], axis=-1, keepdims=True)
    return jnp.exp(a) - jnp.exp(b) + lam_init


def _head(ref_or_val, h, width=HEAD_DIM):
    return ref_or_val[:, h * width:(h + 1) * width]


def _mha(q, ks, vs, n_heads, kv_group, scale, biases=None):
    ks, vs = [k.astype(BF16) for k in ks], [v.astype(BF16) for v in vs]
    qs = (q * (scale * LOG2E)).astype(BF16)

    def scores_of(h):
        scores = [_dot_nt(_head(qs, h), _head(k, h // kv_group)) for k in ks]
        if biases is not None:
            scores = [s if b is None else s + b[h] * LOG2E for s, b in zip(scores, biases)]
        return scores

    def finish(h, ps, rl):
        return functools.reduce(jnp.add, [_dot(p, _head(v, h // kv_group)) for p, v in zip(ps, vs)]) * rl

    return jnp.concatenate(_softmax_pipeline(n_heads, scores_of, finish), axis=-1)


def _softmax_pipeline(n, scores_of, finish):
    outs = []
    scores = scores_of(0)
    for i in range(n):
        ps, rl = _softmax_parts(scores)
        if i + 1 < n:
            scores = scores_of(i + 1)
        outs.append(finish(i, ps, rl))
    return outs


def _diff_attn(q, ks, vs, lam, sub_gain, bd64, lam_init):
    qs = (q * (DIFF_DIM ** -0.5 * LOG2E)).astype(BF16)
    ks, vs = [k.astype(BF16) for k in ks], [v.astype(BF16) for v in vs]
    scores_of = lambda i: [_dot_nt(_head(qs, i, DIFF_DIM), _head(k, i, DIFF_DIM)) for k in ks]
    finish = lambda i, ps, rl: functools.reduce(jnp.add, [_dot(p, _head(v, i // 2)) for p, v in zip(ps, vs)]) * rl
    pv = _softmax_pipeline(2 * DIFF_HEADS, scores_of, finish)
    o = jnp.concatenate([pv[2 * h] - lam * pv[2 * h + 1] for h in range(DIFF_HEADS)], axis=-1)
    ms = _group_mean(o * o, bd64)
    return o * lax.rsqrt(ms + EPS) * sub_gain * (1.0 - lam_init)


def _ctx_attn_kernel(nq, nk, nv, gq, gk, gv, dq, dk, dv, lam_ref, subg_ref, bd64_ref, o_ref, *, lam_init):
    scale = HEAD_DIM ** -0.5
    o_ref[:, 0:256] = _mha(nq[...], [nk[...]], [nv[...]], NAT_HEADS, 1, scale)
    o_ref[:, 256:512] = _mha(gq[...], [gk[...]], [gv[...]], GQA_HEADS, GQA_HEADS // GQA_KV_HEADS, scale)
    lam = _diff_lambda(lam_ref, lam_init)
    o_ref[:, 512:768] = _diff_attn(dq[...], [dk[...]], [dv[...]], lam, subg_ref[...], bd64_ref[...], lam_init)


_COL = dict(nq=0, nk=1, nv=2, gq=3, gk=8, gv=9, dq=5, dk=6, dv=7, sgu=8)


def _ctx_attention(proj, lp, lam_init):
    blk = lambda name, w: pl.BlockSpec((SEQ, w), lambda b: (b, _COL[name]))
    return pl.pallas_call(
        functools.partial(_ctx_attn_kernel, lam_init=lam_init),
        grid=(BATCH,),
        in_specs=[blk("nq", 256), blk("nk", 256), blk("nv", 256),
                  blk("gq", 256), blk("gk", 128), blk("gv", 128),
                  blk("dq", 256), blk("dk", 256), blk("dv", 256),
                  _const_spec((4, DIFF_DIM)), _const_spec((1, 256)), _const_spec((256, 256))],
        out_specs=pl.BlockSpec((SEQ, 768), lambda b: (b, 0)),
        out_shape=jax.ShapeDtypeStruct((N_CTX, 768), F32),
        compiler_params=_cparams("arbitrary"),
        name="ctx_attention",
    )(*([proj] * 9), lp["diff_lambda"], lp["diff_sub_g"], lp["bd64"])


def _gqa_lat_kernel(q_ref, k_ref, v_ref, ck_ref, cv_ref, o_ref):
    o_ref[...] = _mha(q_ref[...], [k_ref[...], ck_ref[...]], [v_ref[...], cv_ref[...]],
                      GQA_HEADS, GQA_HEADS // GQA_KV_HEADS, HEAD_DIM ** -0.5)


def _diff_lat_kernel(q_ref, k_ref, v_ref, ck_ref, cv_ref, lam_ref, subg_ref, bd64_ref, o_ref, *, lam_init):
    lam = _diff_lambda(lam_ref, lam_init)
    o_ref[...] = _diff_attn(q_ref[...], [k_ref[...], ck_ref[...]], [v_ref[...], cv_ref[...]],
                            lam, subg_ref[...], bd64_ref[...], lam_init)


def _nat_lat_kernel(q_ref, k_ref, v_ref, ck_ref, cv_ref, bias_ref, o_ref):
    i = pl.program_id(1)
    k_row0 = jnp.clip(NAT_QROWS * i - NAT_ROWS // 2, 0, GRID_W - NAT_KROWS)
    start = pl.multiple_of(k_row0 * GRID_W, GRID_W)
    kw = k_ref[pl.ds(start, NAT_KROWS * GRID_W), :]
    vw = v_ref[pl.ds(start, NAT_KROWS * GRID_W), :]
    o_ref[...] = _mha(q_ref[...], [kw, ck_ref[...]], [vw, cv_ref[...]], NAT_HEADS, 1, HEAD_DIM ** -0.5,
                      biases=[bias_ref[0], None])


def _nat_bias_table(rpb):
    rows = DEC_SEQ // GRID_W
    nblk = rows // NAT_QROWS
    pad = jnp.pad(rpb, ((0, 0), (0, 0), (GRID_W - NAT_COLS, GRID_W - NAT_COLS)))
    toep = jnp.stack([pad[:, :, GRID_W - 1 - c:2 * GRID_W - 1 - c] for c in range(GRID_W)], axis=2)
    col = np.arange(GRID_W)
    cs = np.clip(col - NAT_COLS // 2, 0, GRID_W - NAT_COLS)
    col_ok = (col[None, :] >= cs[:, None]) & (col[None, :] < cs[:, None] + NAT_COLS)
    toep = jnp.where(col_ok, toep, NEG)
    masked = jnp.full((NAT_HEADS, GRID_W, GRID_W), NEG, F32)
    cases = []
    for blk in (0, 1, nblk - 1):
        r0 = blk * NAT_QROWS
        k0 = int(np.clip(r0 - NAT_ROWS // 2, 0, rows - NAT_KROWS))
        q_rows = []
        for qr in range(r0, r0 + NAT_QROWS):
            rs = int(np.clip(qr - NAT_ROWS // 2, 0, rows - NAT_ROWS))
            q_rows.append(jnp.concatenate(
                [toep[:, kr - qr + NAT_ROWS - 1] if rs <= kr < rs + NAT_ROWS else masked
                 for kr in range(k0, k0 + NAT_KROWS)], axis=-1))
        cases.append(jnp.concatenate(q_rows, axis=-2))
    return jnp.stack(cases, axis=0)


def _lat_attention(proj, caches, lp, l, lam_init):
    cnk, cnv, cgk, cgv, cdk, cdv = caches
    nq_blocks = DEC_SEQ // TQ
    qspec = lambda name: pl.BlockSpec((TQ, 256), lambda b, i: (b * nq_blocks + i, _COL[name]))
    kvspec = lambda name, w: pl.BlockSpec((DEC_SEQ, w), lambda b, i: (b, _COL[name]))
    cspec = lambda w: pl.BlockSpec((None, None, PAST_LEN, w), lambda b, i: (b, l, 0, 0))
    ospec = pl.BlockSpec((TQ, 256), lambda b, i: (b * nq_blocks + i, 0))
    oshape = jax.ShapeDtypeStruct((N_LAT, 256), F32)
    o_gqa = pl.pallas_call(
        _gqa_lat_kernel,
        grid=(DEC_BATCH, nq_blocks),
        in_specs=[qspec("gq"), kvspec("gk", 128), kvspec("gv", 128), cspec(128), cspec(128)],
        out_specs=ospec, out_shape=oshape,
        compiler_params=_cparams("arbitrary", "arbitrary"),
        name="gqa_lat_attention",
    )(proj, proj, proj, cgk.reshape(DEC_BATCH, DEPTH, PAST_LEN, 128), cgv.reshape(DEC_BATCH, DEPTH, PAST_LEN, 128))
    o_dif = pl.pallas_call(
        functools.partial(_diff_lat_kernel, lam_init=lam_init),
        grid=(DEC_BATCH, nq_blocks),
        in_specs=[qspec("dq"), kvspec("dk", 256), kvspec("dv", 256), cspec(256), cspec(256),
                  _const_spec((4, DIFF_DIM)), _const_spec((1, 256)), _const_spec((256, 256))],
        out_specs=ospec, out_shape=oshape,
        compiler_params=_cparams("arbitrary", "arbitrary"),
        name="diff_lat_attention",
    )(proj, proj, proj, cdk.reshape(DEC_BATCH, DEPTH, PAST_LEN, 256), cdv.reshape(DEC_BATCH, DEPTH, PAST_LEN, 256),
      lp["diff_lambda"], lp["diff_sub_g"], lp["bd64"])
    nblk = DEC_SEQ // (NAT_QROWS * GRID_W)
    nat_q = NAT_QROWS * GRID_W
    o_nat = pl.pallas_call(
        _nat_lat_kernel,
        grid=(DEC_BATCH, nblk),
        in_specs=[pl.BlockSpec((nat_q, 256), lambda b, i: (b * nblk + i, _COL["nq"])),
                  kvspec("nk", 256), kvspec("nv", 256), cspec(256), cspec(256),
                  pl.BlockSpec((1, NAT_HEADS, nat_q, NAT_KROWS * GRID_W),
                               lambda b, i: (jnp.where(i == 0, 0, jnp.where(i == nblk - 1, 2, 1)), 0, 0, 0))],
        out_specs=pl.BlockSpec((nat_q, 256), lambda b, i: (b * nblk + i, 0)),
        out_shape=oshape,
        compiler_params=_cparams("arbitrary", "arbitrary"),
        name="nat_lat_attention",
    )(proj, proj, proj, cnk.reshape(DEC_BATCH, DEPTH, PAST_LEN, 256), cnv.reshape(DEC_BATCH, DEPTH, PAST_LEN, 256),
      lp["nat_bias"])
    return o_nat, o_gqa, o_dif


def _merge_kernel(x_ref, mod_ref, n1_ref, b0_ref, b1_ref, b2_ref, b3_ref, wb_ref, wg_ref, bg_ref, wo_ref, o_ref):
    x = x_ref[...]
    mod = mod_ref[0]
    hb = _rms_mod(x, n1_ref[...], mod[:, D_MODEL:2 * D_MODEL], mod[:, 0:D_MODEL]).astype(BF16)
    merged = None
    for n, b_ref in enumerate((b0_ref, b1_ref, b2_ref, b3_ref)):
        cols = slice(n * D_MODEL, (n + 1) * D_MODEL)
        gate = _sigmoid(jnp.dot(hb, wg_ref[:, cols], preferred_element_type=F32) + bg_ref[:, cols])
        term = gate * _dot(b_ref[...], wb_ref[n])
        merged = term if merged is None else merged + term
    out = _dot(merged, wo_ref[...])
    o_ref[...] = x + mod[:, 2 * D_MODEL:3 * D_MODEL] * out


def _merge(x, mod_l, lp, branches, latent):
    n_tok = x.shape[0]
    return pl.pallas_call(
        _merge_kernel,
        grid=(n_tok // TB,),
        in_specs=[pl.BlockSpec((TB, D_MODEL), lambda i: (i, 0)), _mod_spec(latent), _const_spec((1, D_MODEL))]
                 + [pl.BlockSpec((TB, BRANCH_W), functools.partial(lambda i, c: (i, c), c=col)) for _, col in branches]
                 + [_layer_spec(lp["layer"], (N_BRANCH, BRANCH_W, D_MODEL)),
                    _layer_spec(lp["layer"], (D_MODEL, N_BRANCH * D_MODEL)),
                    _const_spec((1, N_BRANCH * D_MODEL)), _layer_spec(lp["layer"], (D_MODEL, D_MODEL))],
        out_specs=pl.BlockSpec((TB, D_MODEL), lambda i: (i, 0)),
        out_shape=jax.ShapeDtypeStruct((n_tok, D_MODEL), F32),
        compiler_params=_cparams("arbitrary"),
        name="merge_lat" if latent else "merge_ctx",
    )(x, mod_l, lp["norm1_g"], *[a for a, _ in branches], lp["w_branch"], lp["w_gate"], lp["b_gate"], lp["w_out"])


def _oddeven_merge_pairs(n):
    pairs = []
    p = 1
    while p < n:
        k = p
        while k >= 1:
            for j in range(k % p, n - k, 2 * k):
                for i in range(min(k, n - j - k)):
                    if (i + j) // (2 * p) == (i + j + k) // (2 * p):
                        pairs.append((i + j, i + j + k))
            k //= 2
        p *= 2
    return pairs


def _top_desc_sorted(s, k):
    n = s.shape[0] // SUBLANES
    tiles = [s[j * SUBLANES:(j + 1) * SUBLANES, :] for j in range(n)]
    for a, b in _oddeven_merge_pairs(n):
        tiles[a], tiles[b] = jnp.maximum(tiles[a], tiles[b]), jnp.minimum(tiles[a], tiles[b])
    rows = []
    for i in range(k):
        m = jnp.max(tiles[0], axis=0, keepdims=True)
        rows.append(m)
        hit = tiles[0] == m
        for j in range(min(n, k - 1 - i)):
            tiles[j] = jnp.where(hit, tiles[j + 1] if j + 1 < n else NEG, tiles[j])
    return jnp.concatenate(rows, axis=0)


def _count_above(t, x, strict):
    assert PEER_TOPK == 16, "the search tree below is written out for 16 sorted rows"
    above = (lambda p: p > x) if strict else (lambda p: p >= x)
    row = lambda b: t[b:b + 1]
    b8 = above(row(7))
    b4 = above(jnp.where(b8, row(11), row(3)))
    b2 = above(jnp.where(b8, jnp.where(b4, row(13), row(9)), jnp.where(b4, row(5), row(1))))
    hi = jnp.where(b4, jnp.where(b2, row(14), row(12)), jnp.where(b2, row(10), row(8)))
    lo = jnp.where(b4, jnp.where(b2, row(6), row(4)), jnp.where(b2, row(2), row(0)))
    b1 = above(jnp.where(b8, hi, lo))
    count = jnp.where(b8, 8.0, 0.0) + jnp.where(b4, 4.0, 0.0) + jnp.where(b2, 2.0, 0.0) + jnp.where(b1, 1.0, 0.0)
    return count + jnp.where(above(row(15)), 1.0, 0.0)


def _peer_score_kernel(x_ref, mod_ref, n2_ref, wqt_ref, sk_ref, ht_ref, e1_ref, cnt_ref, rk_ref, e2_ref):
    mod = mod_ref[0]
    h2 = _rms_mod(x_ref[...], n2_ref[...], mod[:, 4 * D_MODEL:5 * D_MODEL], mod[:, 3 * D_MODEL:4 * D_MODEL])
    htb = h2.T.astype(BF16)
    ht_ref[...] = htb
    qt = jnp.dot(wqt_ref[...], htb, preferred_element_type=F32)
    k1 = PEER_TOPK + 1
    half = PEER_TOPK // 2
    row = lax.broadcasted_iota(jnp.int32, (half, 1), 0)
    neg_tile = jnp.full((half, qt.shape[-1]), NEG, F32)
    for h in range(PEER_HEADS):
        s1, s2 = (jnp.dot(sk_ref[p], qt[(2 * h + p) * PEER_KEYS:(2 * h + p + 1) * PEER_KEYS, :].astype(BF16),
                          preferred_element_type=F32) for p in range(2))
        t1 = _top_desc_sorted(s1, k1)
        t2 = _top_desc_sorted(s2, k1)
        rk2 = _count_above(t2, s2, strict=True)
        cands = [t1[0:1] + t2[0:half], t1[0:1] + t2[half:PEER_TOPK],
                 jnp.where(row >= 1, t1[0:half] + t2[0:1], NEG), t1[half:PEER_TOPK] + t2[0:1],
                 jnp.where(row == 0, t1[0:1] + t2[PEER_TOPK:k1],
                           jnp.where(row == 1, t1[PEER_TOPK:k1] + t2[0:1], NEG))]
        for a in range(1, half):
            nb = k1 // (a + 1)
            cands.append(jnp.where((row >= 1) & (row < nb), t1[a:a + 1] + t2[0:half], NEG))
        cands += [neg_tile] * (PEER_TOPK - len(cands))
        best = _top_desc_sorted(jnp.concatenate(cands, axis=0), k1)
        z = jnp.sum(jnp.exp(best[0:PEER_TOPK] - best[0:1]), axis=0, keepdims=True)
        thr = 0.5 * (best[PEER_TOPK - 1:PEER_TOPK] + best[PEER_TOPK:k1])
        e1_ref[h] = jnp.exp(s1 - t1[0:1]) * (0.5 / z)
        cnt_ref[h] = _count_above(t2, thr - s1, strict=False)
        rk_ref[h] = pltpu.bitcast(rk2.astype(BF16), jnp.int32)
        e2_ref[h] = pltpu.bitcast(jnp.exp(s2 - t2[0:1]).astype(BF16), jnp.int32)


def _peer_dense_kernel(ht_ref, e1_ref, cnt_ref, rk_ref, e2_ref, u_ref, v_ref, vlast_ref, x_ref, mod_ref, o_ref,
                       acc_ref, a_ref, g_ref, w_ref, wt_ref):
    e = pl.program_id(1)
    n_tok = acc_ref.shape[0]

    @pl.when(e == 0)
    def _():
        acc_ref[...] = jnp.zeros_like(acc_ref)
        w_ref[...] = jnp.zeros(w_ref.shape, jnp.int32)

    wt_ref[...] = pltpu.bitcast(pltpu.bitcast(w_ref[...], BF16).T, jnp.int32)

    zero = jnp.zeros((), BF16)
    for rr in range(PEER_ROWS):
        r = (e % (SUBLANES // PEER_ROWS)) * PEER_ROWS + rr
        packed_rows = slice(rr * (PEER_KEYS // 2), (rr + 1) * (PEER_KEYS // 2))
        def tile_row(ref, h):
            t16 = jnp.broadcast_to(ref[h, pl.ds(r, 1), :], (BF16_ROWS, n_tok)).astype(BF16)
            return jnp.concatenate([t16] * (PEER_KEYS // BF16_ROWS), axis=0)
        cnt_rows = [tile_row(cnt_ref, h) for h in range(PEER_HEADS)]
        e1_rows = [tile_row(e1_ref, h) for h in range(PEER_HEADS)]
        for lt in range(n_tok // LANES):
            lanes = slice(lt * LANES, (lt + 1) * LANES)
            g = None
            for h in range(PEER_HEADS):
                sel = pltpu.bitcast(rk_ref[h, :, lanes], BF16) < cnt_rows[h][:, lanes]
                term = jnp.where(sel, pltpu.bitcast(e2_ref[h, :, lanes], BF16) * e1_rows[h][:, lanes], zero)
                g = term if g is None else g + term
            g_ref[packed_rows, lanes] = pltpu.bitcast(g, jnp.int32)
    a_ref[...] = jnp.dot(u_ref[...], ht_ref[...], preferred_element_type=F32)
    for rr in range(PEER_ROWS):
        rows = slice(rr * PEER_KEYS, (rr + 1) * PEER_KEYS)
        packed_rows = slice(rr * (PEER_KEYS // 2), (rr + 1) * (PEER_KEYS // 2))
        for lt in range(n_tok // LANES):
            lanes = slice(lt * LANES, (lt + 1) * LANES)
            a = a_ref[rows, lanes]
            t = jnp.tanh(a * (GELU_C0 + GELU_C1 * (a * a)))
            w_ref[packed_rows, lanes] = pltpu.bitcast(
                pltpu.bitcast(g_ref[packed_rows, lanes], BF16) * (a + a * t).astype(BF16), jnp.int32)
    acc_ref[...] += jnp.dot(pltpu.bitcast(wt_ref[...], BF16), v_ref[...], preferred_element_type=F32)

    @pl.when(e == pl.num_programs(1) - 1)
    def _():
        acc = acc_ref[...] + jnp.dot(pltpu.bitcast(w_ref[...], BF16).T, vlast_ref[...], preferred_element_type=F32)
        o_ref[...] = x_ref[...] + mod_ref[0][:, 5 * D_MODEL:6 * D_MODEL] * acc


def _peer(x, mod_l, lp, latent):
    n_tok = x.shape[0]

    def mod_spec(tb):
        if latent:
            return pl.BlockSpec((1, 1, 6 * D_MODEL), lambda i, *_: (1 + i // (DEC_SEQ // tb), 0, 0))
        return pl.BlockSpec((1, 1, 6 * D_MODEL), lambda i, *_: (0, 0, 0))

    n_keys2 = PEER_HEADS * PEER_QDIM
    f32_tab = lambda n: (PEER_HEADS, PEER_KEYS, n)
    bf16_tab = lambda n: (PEER_HEADS, PEER_KEYS // 2, n)
    ht, *tabs = pl.pallas_call(
        _peer_score_kernel,
        grid=(n_tok // TBS,),
        in_specs=[pl.BlockSpec((TBS, D_MODEL), lambda i: (i, 0)), mod_spec(TBS), _const_spec((1, D_MODEL)),
                  _const_spec((n_keys2, D_MODEL)), _const_spec((2, PEER_KEYS, PEER_QDIM // 2))],
        out_specs=[pl.BlockSpec((D_MODEL, TBS), lambda i: (0, i))]
                  + [pl.BlockSpec(f32_tab(TBS), lambda i: (0, 0, i))] * 2
                  + [pl.BlockSpec(bf16_tab(TBS), lambda i: (0, 0, i))] * 2,
        out_shape=[jax.ShapeDtypeStruct((D_MODEL, n_tok), BF16),
                   jax.ShapeDtypeStruct(f32_tab(n_tok), F32), jax.ShapeDtypeStruct(f32_tab(n_tok), F32),
                   jax.ShapeDtypeStruct(bf16_tab(n_tok), jnp.int32), jax.ShapeDtypeStruct(bf16_tab(n_tok), jnp.int32)],
        compiler_params=_cparams("arbitrary"),
        name="peer_scores_lat" if latent else "peer_scores_ctx",
    )(x, mod_l, lp["norm2_g"], lp["peer_wqt"], lp["peer_subkeys"])
    nb = n_tok // TBP
    n_et = PEER_KEYS * PEER_KEYS // PEER_ET
    row_spec = pl.BlockSpec((PEER_HEADS, SUBLANES, TBP), lambda i, e: (0, e // (SUBLANES // PEER_ROWS), i))
    layer = lp["layer"]
    tile_spec = lambda tile_of: pl.BlockSpec((None, PEER_ET, D_MODEL), lambda i, e: (layer, tile_of(e), 0))
    return pl.pallas_call(
        _peer_dense_kernel,
        grid=(nb, n_et),
        in_specs=[pl.BlockSpec((D_MODEL, TBP), lambda i, e: (0, i)), row_spec, row_spec]
                 + [pl.BlockSpec(bf16_tab(TBP), lambda i, e: (0, 0, i))] * 2
                 + [tile_spec(lambda e: e),
                    tile_spec(lambda e: jnp.maximum(e - 1, 0)),
                    tile_spec(lambda e: jnp.where(e == n_et - 1, n_et - 1, 0)),
                    pl.BlockSpec((TBP, D_MODEL), lambda i, e: (i, 0)),
                    mod_spec(TBP)],
        out_specs=pl.BlockSpec((TBP, D_MODEL), lambda i, e: (i, 0)),
        out_shape=jax.ShapeDtypeStruct((n_tok, D_MODEL), F32),
        scratch_shapes=[pltpu.VMEM((TBP, D_MODEL), F32), pltpu.VMEM((PEER_ET, TBP), F32),
                        pltpu.VMEM((PEER_ET // 2, TBP), jnp.int32), pltpu.VMEM((PEER_ET // 2, TBP), jnp.int32),
                        pltpu.VMEM((TBP // 2, PEER_ET), jnp.int32)],
        compiler_params=_cparams("arbitrary", "arbitrary"),
        name="peer_dense_lat" if latent else "peer_dense_ctx",
    )(ht, *tabs, lp["peer_u"], lp["peer_v"], lp["peer_v"], x, mod_l)


def _layer_params(l, w):
    tile = lambda g, n: jnp.tile(g, n)
    gains = jnp.stack([
        tile(w["nat_qk_g"][l, 0], 4), tile(w["nat_qk_g"][l, 1], 4),
        tile(w["gqa_qk_g"][l, 0], 4), tile(w["gqa_qk_g"][l, 1], 4),
        tile(w["diff_qk_g"][l, 0], 8), tile(w["diff_qk_g"][l, 1], 8),
        jnp.zeros((256,), F32), jnp.zeros((256,), F32)])
    return dict(
        norm1_g=w["norm1_g"][l].reshape(1, D_MODEL),
        norm2_g=w["norm2_g"][l].reshape(1, D_MODEL),
        qk_gains=gains,
        bd64=_block_ones(256, 64), bd32=_block_ones(256, 32), bd256=_block_ones(256, 256),
        rope=w["rope"],
        sgu_norm_g=w["sgu_norm_g"][l].reshape(1, SGU_WIDTH),
        sgu_b=jnp.repeat(w["sgu_b"][l].T, SGU_WIDTH // SGU_GROUPS, axis=1),
        diff_lambda=w["diff_lambda"][l],
        diff_sub_g=tile(w["diff_sub_g"][l], 4).reshape(1, 256),
        nat_bias=_nat_bias_table(w["nat_rpb"][l]),
        b_gate=w["b_gate"][l].reshape(1, N_BRANCH * D_MODEL),
        peer_wqt=w["peer_wq"][l].T.astype(BF16),
        peer_subkeys=w["peer_subkeys"][l].astype(BF16),
        layer=l, w_in=w["w_in_bf16"], sgu_w=w["sgu_w_bf16"], w_branch=w["w_branch_bf16"], w_gate=w["w_gate_bf16"],
        w_out=w["w_out_bf16"], peer_u=w["peer_u_bf16"], peer_v=w["peer_v_bf16"],
    )


def kernel(x_prompt, x_sample, c, cache_nat_k, cache_nat_v, cache_gqa_k, cache_gqa_v, cache_diff_k, cache_diff_v, c_ctx, w_mod, b_mod, norm1_g, norm2_g, w_in, nat_qk_g, nat_rpb, gqa_qk_g, diff_qk_g, diff_lambda, diff_sub_g, sgu_norm_g, sgu_w, sgu_b, w_branch, w_gate, b_gate, w_out, peer_wq, peer_subkeys, peer_u, peer_v):
    w = dict(norm1_g=norm1_g, norm2_g=norm2_g, nat_qk_g=nat_qk_g, nat_rpb=nat_rpb,
             gqa_qk_g=gqa_qk_g, diff_qk_g=diff_qk_g, diff_lambda=diff_lambda, diff_sub_g=diff_sub_g,
             sgu_norm_g=sgu_norm_g, sgu_b=sgu_b, b_gate=b_gate, peer_wq=peer_wq, peer_subkeys=peer_subkeys,
             w_in_bf16=w_in.astype(BF16), sgu_w_bf16=sgu_w.astype(BF16), w_branch_bf16=w_branch.astype(BF16),
             w_gate_bf16=w_gate.astype(BF16), w_out_bf16=w_out.astype(BF16),
             peer_u_bf16=peer_u.astype(BF16), peer_v_bf16=peer_v.astype(BF16), rope=_rope_tables())
    cvec = jnp.concatenate([c_ctx[None], c, jnp.zeros((MOD_ROWS - 1 - DEC_BATCH, D_MODEL), F32)], axis=0)
    mod = _modulation(cvec, w_mod, b_mod).reshape(DEPTH, MOD_ROWS, 1, 6 * D_MODEL)
    xp = x_prompt.reshape(N_CTX, D_MODEL)
    xs = x_sample.reshape(N_LAT, D_MODEL)
    new = []
    for l in range(DEPTH):
        lp = _layer_params(l, w)
        lam_init = 0.8 - 0.6 * math.exp(-0.3 * l)
        caches = (cache_nat_k, cache_nat_v, cache_gqa_k, cache_gqa_v, cache_diff_k, cache_diff_v)
        proj = _in_projection(xp, mod[l], lp, latent=False)
        new.append(proj)
        attn = _ctx_attention(proj, lp, lam_init)
        xp = _merge(xp, mod[l], lp, [(attn, 0), (attn, 1), (attn, 2), (proj, _COL["sgu"])], latent=False)
        xp = _peer(xp, mod[l], lp, latent=False)
        proj = _in_projection(xs, mod[l], lp, latent=True)
        o_nat, o_gqa, o_dif = _lat_attention(proj, caches, lp, l, lam_init)
        xs = _merge(xs, mod[l], lp, [(o_nat, 0), (o_gqa, 0), (o_dif, 0), (proj, _COL["sgu"])], latent=True)
        xs = _peer(xs, mod[l], lp, latent=True)

    def cache_out(lo, width, tail):
        a = jnp.stack([p[:, lo:lo + width].reshape(BATCH, SEQ, width) for p in new], axis=1)
        return a.reshape((BATCH, DEPTH, SEQ) + tail)

    return (xp.reshape(BATCH, SEQ, D_MODEL), xs.reshape(DEC_BATCH, DEC_SEQ, D_MODEL),
            cache_out(256, 256, (NAT_HEADS, HEAD_DIM)), cache_out(512, 256, (NAT_HEADS, HEAD_DIM)),
            cache_out(1024, 128, (GQA_KV_HEADS, HEAD_DIM)), cache_out(1152, 128, (GQA_KV_HEADS, HEAD_DIM)),
            cache_out(1536, 256, (DIFF_HEADS, 2, DIFF_DIM)), cache_out(1792, 256, (DIFF_HEADS, HEAD_DIM)))
```
